```python
import math
import jax, jax.numpy as jnp
from jax import lax
import numpy as np

D_MODEL = 1024
BATCH = 8
SEQ = 2048
DEPTH = 1

N_HEADS = 8
Q_LORA = 256
KV_LORA = 128
QK_NOPE = 64
QK_ROPE = 32
V_HEAD = 64
QK_HEAD = QK_NOPE + QK_ROPE
ATTN_SCALE = 1.0 / math.sqrt(QK_HEAD)
ROPE_THETA = 10000.0
Q_BLOCK = 128
CONV_CH = 512
CONV_K = 31
D_FF = ((8 * D_MODEL // 3 + 255) // 256) * 256
N_BRANCH = 2
D_IN = Q_LORA + KV_LORA + QK_ROPE + 2 * CONV_CH + N_BRANCH * D_MODEL
IN_SPLITS = (Q_LORA, Q_LORA + KV_LORA, Q_LORA + KV_LORA + QK_ROPE,
             Q_LORA + KV_LORA + QK_ROPE + 2 * CONV_CH)
N_MOD = 6
EPS_RMS = 1e-6
EPS_LN = 1e-5

kernel_name = "hybrid_mla_conformer_gated_encoder_block"


def rms_norm(x, g):
    xf = x.astype(jnp.float32)
    y = xf * lax.rsqrt(jnp.mean(xf * xf, axis=-1, keepdims=True) + EPS_RMS)
    return (y * g.astype(jnp.float32)).astype(x.dtype)


def layer_norm(x, g, b):
    xf = x.astype(jnp.float32)
    mu = jnp.mean(xf, axis=-1, keepdims=True)
    var = jnp.mean(jnp.square(xf - mu), axis=-1, keepdims=True)
    y = (xf - mu) * lax.rsqrt(var + EPS_LN)
    return (y * g.astype(jnp.float32) + b.astype(jnp.float32)).astype(x.dtype)


def rope_cos_sin(positions):
    inv_freq = ROPE_THETA ** (-jnp.arange(0, QK_ROPE, 2, dtype=jnp.float32) / QK_ROPE)
    ang = positions.astype(jnp.float32)[..., None] * inv_freq
    return jnp.cos(ang), jnp.sin(ang)


def apply_rope(x, cos, sin):
    shape = cos.shape[:2] + (1,) * (x.ndim - 3) + cos.shape[-1:]
    cos = cos.reshape(shape)
    sin = sin.reshape(shape)
    xf = x.astype(jnp.float32)
    x1, x2 = jnp.split(xf, 2, axis=-1)
    out = jnp.concatenate([x1 * cos - x2 * sin, x2 * cos + x1 * sin], axis=-1)
    return out.astype(x.dtype)


def mla_attention(q_nope, q_rope, k_nope, k_rope, v):
    B, S, H, _ = q_nope.shape
    nb = S // Q_BLOCK
    qn = q_nope.reshape(B, nb, Q_BLOCK, H, QK_NOPE).transpose(1, 0, 2, 3, 4)
    qr = q_rope.reshape(B, nb, Q_BLOCK, H, QK_ROPE).transpose(1, 0, 2, 3, 4)

    def block(args):
        qn_b, qr_b = args
        s = (jnp.einsum('bqhd,bkhd->bhqk', qn_b, k_nope)
             + jnp.einsum('bqhr,bkr->bhqk', qr_b, k_rope))
        p = jax.nn.softmax(s.astype(jnp.float32) * ATTN_SCALE, axis=-1).astype(v.dtype)
        return jnp.einsum('bhqk,bkhd->bqhd', p, v)

    o = lax.map(block, (qn, qr))
    return o.transpose(1, 0, 2, 3, 4).reshape(B, S, H * V_HEAD)


def depthwise_conv(z, w, b):
    out = lax.conv_general_dilated(
        z, w[:, None, :].astype(z.dtype), window_strides=(1,),
        padding=[(CONV_K // 2, CONV_K // 2)],
        dimension_numbers=('NWC', 'WIO', 'NWC'),
        feature_group_count=z.shape[-1])
    return out + b.astype(z.dtype)


def setup_inputs(seed: int = 0) -> dict:
    key = jax.random.key(seed)
    ks = jax.random.split(key, 24)
    f32 = jnp.float32

    def w(k, shape, fan_in, mult=1.0):
        return jax.random.normal(k, shape, f32) * (mult * fan_in ** -0.5)

    def gain(k, shape):
        return 1.0 + 0.02 * jax.random.normal(k, shape, f32)

    x = jax.random.normal(ks[0], (BATCH, SEQ, D_MODEL), f32)
    c = jax.random.normal(ks[1], (BATCH, D_MODEL), f32)
    offsets = jax.random.randint(ks[2], (BATCH, 1), 0, 4096, dtype=jnp.int32)
    positions = (jnp.arange(SEQ, dtype=jnp.int32)[None, :] + offsets).astype(jnp.int32)
    L = DEPTH
    return {
        "x": x,
        "c": c,
        "positions": positions,
        "w_ada": w(ks[3], (L, D_MODEL, N_MOD * D_MODEL), D_MODEL, 0.5),
        "b_ada": 0.02 * jax.random.normal(ks[4], (L, N_MOD * D_MODEL), f32),
        "g_norm_mix": gain(ks[5], (L, D_MODEL)),
        "w_in": w(ks[6], (L, D_MODEL, D_IN), D_MODEL),
        "g_q_a": gain(ks[7], (L, Q_LORA)),
        "w_q_up": w(ks[8], (L, Q_LORA, N_HEADS * QK_HEAD), Q_LORA),
        "g_kv_a": gain(ks[9], (L, KV_LORA)),
        "w_kv_up": w(ks[10], (L, KV_LORA, N_HEADS * (QK_NOPE + V_HEAD)), KV_LORA),
        "w_attn_o": w(ks[11], (L, N_HEADS * V_HEAD, D_MODEL), N_HEADS * V_HEAD),
        "w_dw": w(ks[12], (L, CONV_K, CONV_CH), CONV_K),
        "b_dw": 0.02 * jax.random.normal(ks[13], (L, CONV_CH), f32),
        "g_conv_ln": gain(ks[14], (L, CONV_CH)),
        "b_conv_ln": 0.02 * jax.random.normal(ks[15], (L, CONV_CH), f32),
        "w_conv_out": w(ks[16], (L, CONV_CH, D_MODEL), CONV_CH),
        "w_out": w(ks[17], (L, D_MODEL, D_MODEL), D_MODEL),
        "g_norm_ffn": gain(ks[18], (L, D_MODEL)),
        "w_ffn_gate": w(ks[19], (L, D_MODEL, D_FF), D_MODEL),
        "w_ffn_up": w(ks[20], (L, D_MODEL, D_FF), D_MODEL),
        "w_ffn_down": w(ks[21], (L, D_FF, D_MODEL), D_FF),
        "g_final": gain(ks[22], (D_MODEL,)),
    }


def reference(x, c, positions, w_ada, b_ada, g_norm_mix, w_in, g_q_a, w_q_up, g_kv_a, w_kv_up,
              w_attn_o, w_dw, b_dw, g_conv_ln, b_conv_ln, w_conv_out, w_out, g_norm_ffn,
              w_ffn_gate, w_ffn_up, w_ffn_down, g_final):
    B, S, D = x.shape
    cos, sin = rope_cos_sin(positions)
    c_act = jax.nn.silu(c)

    for l in range(DEPTH):
        mod = c_act @ w_ada[l] + b_ada[l]
        shift_m, scale_m, gate_m, shift_f, scale_f, gate_f = [
            m[:, None, :] for m in jnp.split(mod, N_MOD, axis=-1)]

        h = rms_norm(x, g_norm_mix[l]) * (1.0 + scale_m) + shift_m
        proj = h @ w_in[l]
        q_a, kv_a, k_rope, conv_in, gate_logits = jnp.split(proj, IN_SPLITS, axis=-1)

        q = (rms_norm(q_a, g_q_a[l]) @ w_q_up[l]).reshape(B, S, N_HEADS, QK_HEAD)
        q_nope, q_rope = q[..., :QK_NOPE], apply_rope(q[..., QK_NOPE:], cos, sin)
        kv = (rms_norm(kv_a, g_kv_a[l]) @ w_kv_up[l]).reshape(B, S, N_HEADS, QK_NOPE + V_HEAD)
        k_nope, v = kv[..., :QK_NOPE], kv[..., QK_NOPE:]
        k_rope = apply_rope(k_rope, cos, sin)
        y_a = mla_attention(q_nope, q_rope, k_nope, k_rope, v) @ w_attn_o[l]

        u, u_gate = jnp.split(conv_in, 2, axis=-1)
        z = u * jax.nn.sigmoid(u_gate)
        z = depthwise_conv(z, w_dw[l], b_dw[l])
        z = jax.nn.silu(layer_norm(z, g_conv_ln[l], b_conv_ln[l]))
        y_b = z @ w_conv_out[l]

        g_a, g_b = jnp.split(gate_logits, N_BRANCH, axis=-1)
        merged = jax.nn.sigmoid(g_a) * y_a + jax.nn.sigmoid(g_b) * y_b
        x = x + gate_m * (merged @ w_out[l])

        h2 = rms_norm(x, g_norm_ffn[l]) * (1.0 + scale_f) + shift_f
        ff = (jax.nn.silu(h2 @ w_ffn_gate[l]) * (h2 @ w_ffn_up[l])) @ w_ffn_down[l]
        x = x + gate_f * ff

    return rms_norm(x, g_final)
```

```python
import functools
import math

import jax
import jax.numpy as jnp
from jax import lax
from jax.experimental import pallas as pl
from jax.experimental.pallas import tpu as pltpu

F32 = jnp.float32
BF16 = jnp.bfloat16

D_MODEL = 1024
N_HEADS = 8
Q_LORA = 256
KV_LORA = 128
QK_NOPE = 64
QK_ROPE = 32
HALF_ROPE = QK_ROPE // 2
V_HEAD = 64
QK_HEAD = QK_NOPE + QK_ROPE
ATTN_SCALE = 1.0 / math.sqrt(QK_HEAD)
ROPE_THETA = 10000.0
CONV_CH = 512
CONV_K = 31
N_MOD = 6
EPS_RMS = 1e-6
EPS_LN = 1e-5

LANES = 128
HEAD_PAD = LANES
VMEM_LIMIT = 56 * 1024 * 1024

C_QA = 0
C_KVA = C_QA + Q_LORA
C_KR = C_KVA + KV_LORA
C_CONV = C_KR + 2 * HEAD_PAD
C_GATE = C_CONV + 2 * CONV_CH
D_IN_EXT = C_GATE + 2 * D_MODEL

TS_IN = 256
TQ = 256
TS_MIX = 256
HALO = 16
TM_FFN = 512


def _sigmoid(x):
    return 1.0 / (1.0 + jnp.exp(-x))


def _rms(x, g):
    return x * lax.rsqrt(jnp.mean(x * x, axis=-1, keepdims=True) + EPS_RMS) * g


def _dot(a, b):
    return jnp.dot(a, b, preferred_element_type=F32)


def _const_spec(shape):
    nd = len(shape)
    return pl.BlockSpec(shape, lambda *_: (0,) * nd, pipeline_mode=pl.Buffered(1))


def _rope_kernel(pos_ref, freq_ref, cos_ref, sin_ref):
    ang = pos_ref[...] * freq_ref[...]
    cos_ref[...] = jnp.cos(ang)
    sin_ref[...] = jnp.sin(ang)


def _rope_tables(positions):
    B, S = positions.shape
    inv_freq = ROPE_THETA ** (-jnp.arange(0, QK_ROPE, 2, dtype=F32) / QK_ROPE)
    rows = B * S * HALF_ROPE // LANES
    pos_rep = jnp.repeat(positions.reshape(-1).astype(F32), HALF_ROPE).reshape(rows, LANES)
    freq = jnp.tile(inv_freq, LANES // HALF_ROPE)[None, :]
    cos, sin = pl.pallas_call(
        _rope_kernel,
        out_shape=(jax.ShapeDtypeStruct((rows, LANES), F32),) * 2,
        name="rope_tables",
    )(pos_rep, freq)
    cos = cos.reshape(B, S, HALF_ROPE)
    sin = sin.reshape(B, S, HALF_ROPE)
    ones = jnp.ones((B, S, QK_NOPE), F32)
    zeros_n = jnp.zeros((B, S, QK_NOPE), F32)
    zeros_p = jnp.zeros((B, S, HEAD_PAD - QK_HEAD), F32)
    ctab = jnp.concatenate([ones, cos, cos, zeros_p], axis=-1)
    stab = jnp.concatenate([zeros_n, sin, sin, zeros_p], axis=-1)
    return ctab, stab


def _mod_kernel(c_ref, w_ref, b_ref, o_ref):
    c = c_ref[...]
    c_act = (c * _sigmoid(c)).astype(BF16)
    o_ref[...] = _dot(c_act, w_ref[...].astype(BF16)) + b_ref[...]


def _modulation(c, w_ada, b_ada):
    B, D = c.shape
    n = w_ada.shape[1]
    bn = 1536
    return pl.pallas_call(
        _mod_kernel,
        out_shape=jax.ShapeDtypeStruct((B, n), F32),
        grid=(n // bn,),
        in_specs=[
            pl.BlockSpec((B, D), lambda j: (0, 0)),
            pl.BlockSpec((D, bn), lambda j: (0, j)),
            pl.BlockSpec((1, bn), lambda j: (0, j)),
        ],
        out_specs=pl.BlockSpec((B, bn), lambda j: (0, j)),
        compiler_params=pltpu.CompilerParams(
            dimension_semantics=("parallel",), vmem_limit_bytes=VMEM_LIMIT),
        name="adaln_modulation",
    )(c, w_ada, b_ada)


def _inproj_kernel(x_ref, mod_ref, gmix_ref, win_ref, gq_ref, wq_ref, gkv_ref, wkv_ref,
                   ctab_ref, stab_ref,
                   q_ref, k_ref, v_ref, z_ref, gates_ref):
    D = D_MODEL
    shift = mod_ref[:, 0:D]
    scale = mod_ref[:, D:2 * D]
    h = (_rms(x_ref[...], gmix_ref[...]) * (1.0 + scale) + shift).astype(BF16)
    ctab = ctab_ref[...]
    stab = stab_ref[...]

    qa = _dot(h, win_ref[:, C_QA:C_QA + Q_LORA])
    qn = _rms(qa, gq_ref[...]).astype(BF16)
    nq = N_HEADS * HEAD_PAD
    q_main = _dot(qn, wq_ref[:, 0:nq])
    q_swap = _dot(qn, wq_ref[:, nq:2 * nq])
    for hd in range(N_HEADS):
        sl = slice(hd * HEAD_PAD, (hd + 1) * HEAD_PAD)
        q_ref[:, sl] = ((q_main[:, sl] * ctab + q_swap[:, sl] * stab) * ATTN_SCALE).astype(BF16)

    kva = _dot(h, win_ref[:, C_KVA:C_KVA + KV_LORA])
    kvn = _rms(kva, gkv_ref[...]).astype(BF16)
    k_pad = _dot(kvn, wkv_ref[:, 0:nq])
    v_ref[...] = _dot(kvn, wkv_ref[:, nq:nq + N_HEADS * V_HEAD]).astype(BF16)
    kr = _dot(h, win_ref[:, C_KR:C_KR + 2 * HEAD_PAD])
    kr_rot = kr[:, 0:HEAD_PAD] * ctab + kr[:, HEAD_PAD:2 * HEAD_PAD] * stab
    for hd in range(N_HEADS):
        sl = slice(hd * HEAD_PAD, (hd + 1) * HEAD_PAD)
        k_ref[:, sl] = (k_pad[:, sl] + kr_rot).astype(BF16)

    cu = _dot(h, win_ref[:, C_CONV:C_CONV + 2 * CONV_CH])
    z_ref[...] = cu[:, 0:CONV_CH] * _sigmoid(cu[:, CONV_CH:2 * CONV_CH])

    gates_ref[...] = _sigmoid(_dot(h, win_ref[:, C_GATE:C_GATE + 2 * D]))


def _inproj(x, mod3, g_mix, w_in_ext, g_q, wq2, g_kv, wkv, ctab, stab):
    B, S, D = x.shape
    ts = TS_IN
    nq = N_HEADS * HEAD_PAD
    tok = lambda w: pl.BlockSpec((None, ts, w), lambda b, i: (b, i, 0))
    return pl.pallas_call(
        _inproj_kernel,
        out_shape=(
            jax.ShapeDtypeStruct((B, S, nq), BF16),
            jax.ShapeDtypeStruct((B, S, nq), BF16),
            jax.ShapeDtypeStruct((B, S, N_HEADS * V_HEAD), BF16),
            jax.ShapeDtypeStruct((B, S, CONV_CH), F32),
            jax.ShapeDtypeStruct((B, S, 2 * D), F32),
        ),
        grid=(B, S // ts),
        in_specs=[
            tok(D),
            pl.BlockSpec((None, 1, N_MOD * D), lambda b, i: (b, 0, 0)),
            _const_spec(g_mix.shape),
            _const_spec(w_in_ext.shape),
            _const_spec(g_q.shape),
            _const_spec(wq2.shape),
            _const_spec(g_kv.shape),
            _const_spec(wkv.shape),
            tok(HEAD_PAD),
            tok(HEAD_PAD),
        ],
        out_specs=(tok(nq), tok(nq), tok(N_HEADS * V_HEAD), tok(CONV_CH), tok(2 * D)),
        compiler_params=pltpu.CompilerParams(
            dimension_semantics=("parallel", "parallel"), vmem_limit_bytes=VMEM_LIMIT),
        name="input_projection",
    )(x, mod3, g_mix, w_in_ext, g_q, wq2, g_kv, wkv, ctab, stab)


def _attn_kernel(q_ref, k_ref, v_ref, o_ref):
    v = v_ref[...]
    outs = []
    for j in range(2):
        sl = slice(j * HEAD_PAD, (j + 1) * HEAD_PAD)
        s = lax.dot_general(q_ref[:, sl], k_ref[:, sl], (((1,), (1,)), ((), ())),
                            preferred_element_type=F32)
        m = jnp.max(s, axis=-1, keepdims=True)
        p = jnp.exp(s - m)
        l = jnp.sum(p, axis=-1, keepdims=True)
        outs.append(_dot(p.astype(BF16), v) / l)
    lane = lax.broadcasted_iota(jnp.int32, outs[0].shape, 1)
    o_ref[...] = jnp.where(lane < V_HEAD, outs[0], outs[1]).astype(BF16)


def _attention(q, k, v):
    B, S, _ = q.shape
    pair = 2 * HEAD_PAD
    return pl.pallas_call(
        _attn_kernel,
        out_shape=jax.ShapeDtypeStruct((B, S, N_HEADS * V_HEAD), BF16),
        grid=(B, N_HEADS // 2, S // TQ),
        in_specs=[
            pl.BlockSpec((None, TQ, pair), lambda b, hp, i: (b, i, hp)),
            pl.BlockSpec((None, S, pair), lambda b, hp, i: (b, 0, hp)),
            pl.BlockSpec((None, S, 2 * V_HEAD), lambda b, hp, i: (b, 0, hp)),
        ],
        out_specs=pl.BlockSpec((None, TQ, 2 * V_HEAD), lambda b, hp, i: (b, i, hp)),
        compiler_params=pltpu.CompilerParams(
            dimension_semantics=("parallel", "parallel", "parallel"),
            vmem_limit_bytes=VMEM_LIMIT),
        name="mla_attention",
    )(q, k, v)


def _mix_kernel(x_ref, mod_ref, z_ref, zprev_ref, znext_ref, o_ref, gates_ref,
                wdw_ref, bdw_ref, gln_ref, bln_ref, wco_ref, wao_ref, wout_ref,
                out_ref, zp_ref, conv_ref):
    D = D_MODEL
    ts = TS_MIX
    i = pl.program_id(1)
    n_i = pl.num_programs(1)

    zp_ref[0:HALO, :] = jnp.where(i > 0, zprev_ref[...], 0.0)
    zp_ref[HALO:HALO + ts, :] = z_ref[...]
    zp_ref[HALO + ts:2 * HALO + ts, :] = jnp.where(i < n_i - 1, znext_ref[...], 0.0)

    row_chunk = 64
    base = HALO - CONV_K // 2
    for cb in range(CONV_CH // LANES):
        cs = slice(cb * LANES, (cb + 1) * LANES)
        for rb in range(ts // row_chunk):
            r0 = rb * row_chunk
            acc = jnp.broadcast_to(bdw_ref[:, cs], (row_chunk, LANES))
            for kk in range(CONV_K):
                acc = acc + wdw_ref[kk:kk + 1, cs] * zp_ref[r0 + base + kk:r0 + base + kk + row_chunk, cs]
            conv_ref[r0:r0 + row_chunk, cs] = acc

    zc = conv_ref[...]
    mu = jnp.mean(zc, axis=-1, keepdims=True)
    zd = zc - mu
    var = jnp.mean(zd * zd, axis=-1, keepdims=True)
    zn = zd * lax.rsqrt(var + EPS_LN) * gln_ref[...] + bln_ref[...]
    zs = (zn * _sigmoid(zn)).astype(BF16)
    y_b = _dot(zs, wco_ref[...])
    y_a = _dot(o_ref[...], wao_ref[...])
    merged = (gates_ref[:, 0:D] * y_a + gates_ref[:, D:2 * D] * y_b).astype(BF16)
    gate_m = mod_ref[:, 2 * D:3 * D]
    out_ref[...] = x_ref[...] + gate_m * _dot(merged, wout_ref[...])


def _mix(x, mod3, z, o, gates, w_dw, b_dw, g_ln, b_ln, w_co, w_ao, w_out):
    B, S, D = x.shape
    ts = TS_MIX
    hb = ts // HALO
    n_halo = S // HALO
    tok = lambda w: pl.BlockSpec((None, ts, w), lambda b, i: (b, i, 0))
    return pl.pallas_call(
        _mix_kernel,
        out_shape=jax.ShapeDtypeStruct((B, S, D), F32),
        grid=(B, S // ts),
        in_specs=[
            tok(D),
            pl.BlockSpec((None, 1, N_MOD * D), lambda b, i: (b, 0, 0)),
            tok(CONV_CH),
            pl.BlockSpec((None, HALO, CONV_CH), lambda b, i: (b, jnp.maximum(i * hb - 1, 0), 0)),
            pl.BlockSpec((None, HALO, CONV_CH),
                         lambda b, i: (b, jnp.minimum((i + 1) * hb, n_halo - 1), 0)),
            tok(N_HEADS * V_HEAD),
            tok(2 * D),
            _const_spec(w_dw.shape),
            _const_spec(b_dw.shape),
            _const_spec(g_ln.shape),
            _const_spec(b_ln.shape),
            _const_spec(w_co.shape),
            _const_spec(w_ao.shape),
            _const_spec(w_out.shape),
        ],
        out_specs=tok(D),
        scratch_shapes=[
            pltpu.VMEM((ts + 2 * HALO, CONV_CH), F32),
            pltpu.VMEM((ts, CONV_CH), F32),
        ],
        compiler_params=pltpu.CompilerParams(
            dimension_semantics=("parallel", "parallel"), vmem_limit_bytes=VMEM_LIMIT),
        name="conv_merge_out",
    )(x, mod3, z, z, z, o, gates, w_dw, b_dw, g_ln, b_ln, w_co, w_ao, w_out)


def _ffn_kernel(x_ref, mod_ref, gffn_ref, wg_ref, wu_ref, wd_ref, gfin_ref, out_ref, *, final_norm):
    D = D_MODEL
    x = x_ref[...]
    shift = mod_ref[:, 3 * D:4 * D]
    scale = mod_ref[:, 4 * D:5 * D]
    gate = mod_ref[:, 5 * D:6 * D]
    h = (_rms(x, gffn_ref[...]) * (1.0 + scale) + shift).astype(BF16)
    g = _dot(h, wg_ref[...])
    u = _dot(h, wu_ref[...])
    a = (g * _sigmoid(g) * u).astype(BF16)
    x2 = x + gate * _dot(a, wd_ref[...])
    out_ref[...] = _rms(x2, gfin_ref[...]) if final_norm else x2


def _ffn(x, mod3, g_ffn, w_gate, w_up, w_down, g_final, final_norm):
    B, S, D = x.shape
    tm = TM_FFN
    tok = pl.BlockSpec((None, tm, D), lambda b, i: (b, i, 0))
    return pl.pallas_call(
        functools.partial(_ffn_kernel, final_norm=final_norm),
        out_shape=jax.ShapeDtypeStruct((B, S, D), F32),
        grid=(B, S // tm),
        in_specs=[
            tok,
            pl.BlockSpec((None, 1, N_MOD * D), lambda b, i: (b, 0, 0)),
            _const_spec(g_ffn.shape),
            _const_spec(w_gate.shape),
            _const_spec(w_up.shape),
            _const_spec(w_down.shape),
            _const_spec(g_final.shape),
        ],
        out_specs=tok,
        compiler_params=pltpu.CompilerParams(
            dimension_semantics=("parallel", "parallel"), vmem_limit_bytes=VMEM_LIMIT),
        name="swiglu_final_norm",
    )(x, mod3, g_ffn, w_gate, w_up, w_down, g_final)


def _prep_w_in(w_in):
    D = w_in.shape[0]
    o_kr = Q_LORA + KV_LORA
    kr1 = w_in[:, o_kr:o_kr + HALF_ROPE]
    kr2 = w_in[:, o_kr + HALF_ROPE:o_kr + QK_ROPE]
    zn = jnp.zeros((D, QK_NOPE), w_in.dtype)
    zp = jnp.zeros((D, HEAD_PAD - QK_HEAD), w_in.dtype)
    return jnp.concatenate(
        [w_in[:, :o_kr], zn, kr1, kr2, zp, zn, -kr2, kr1, zp, w_in[:, o_kr + QK_ROPE:]],
        axis=1).astype(BF16)


def _prep_w_q(w_q_up):
    r = w_q_up.shape[0]
    w = w_q_up.reshape(r, N_HEADS, QK_HEAD)
    nope = w[..., :QK_NOPE]
    r1 = w[..., QK_NOPE:QK_NOPE + HALF_ROPE]
    r2 = w[..., QK_NOPE + HALF_ROPE:]
    zp = jnp.zeros((r, N_HEADS, HEAD_PAD - QK_HEAD), w.dtype)
    main = jnp.concatenate([nope, r1, r2, zp], axis=-1).reshape(r, N_HEADS * HEAD_PAD)
    swap = jnp.concatenate([jnp.zeros_like(nope), -r2, r1, zp], axis=-1).reshape(r, N_HEADS * HEAD_PAD)
    return jnp.concatenate([main, swap], axis=1).astype(BF16)


def _prep_w_kv(w_kv_up):
    r = w_kv_up.shape[0]
    w = w_kv_up.reshape(r, N_HEADS, QK_NOPE + V_HEAD)
    k_nope = w[..., :QK_NOPE]
    v = w[..., QK_NOPE:]
    k_pad = jnp.concatenate(
        [k_nope, jnp.zeros((r, N_HEADS, HEAD_PAD - QK_NOPE), w.dtype)], axis=-1)
    return jnp.concatenate(
        [k_pad.reshape(r, N_HEADS * HEAD_PAD), v.reshape(r, N_HEADS * V_HEAD)], axis=1).astype(BF16)


def kernel(x, c, positions, w_ada, b_ada, g_norm_mix, w_in, g_q_a, w_q_up, g_kv_a, w_kv_up,
           w_attn_o, w_dw, b_dw, g_conv_ln, b_conv_ln, w_conv_out, w_out, g_norm_ffn,
           w_ffn_gate, w_ffn_up, w_ffn_down, g_final):
    B, S, D = x.shape
    depth = w_ada.shape[0]
    ctab, stab = _rope_tables(positions)
    for l in range(depth):
        mod3 = _modulation(c, w_ada[l], b_ada[l][None, :]).reshape(B, 1, N_MOD * D)
        q, k, v, z, gates = _inproj(
            x, mod3, g_norm_mix[l][None, :], _prep_w_in(w_in[l]), g_q_a[l][None, :],
            _prep_w_q(w_q_up[l]), g_kv_a[l][None, :], _prep_w_kv(w_kv_up[l]), ctab, stab)
        o = _attention(q, k, v)
        x = _mix(x, mod3, z, o, gates, w_dw[l], b_dw[l][None, :], g_conv_ln[l][None, :],
                 b_conv_ln[l][None, :], w_conv_out[l].astype(BF16), w_attn_o[l].astype(BF16),
                 w_out[l].astype(BF16))
        x = _ffn(x, mod3, g_norm_ffn[l][None, :], w_ffn_gate[l].astype(BF16),
                 w_ffn_up[l].astype(BF16), w_ffn_down[l].astype(BF16), g_final[None, :],
                 final_norm=(l == depth - 1))
    return x
```

```python
import functools
import math

import jax
import jax.numpy as jnp
from jax import lax
from jax.experimental import pallas as pl
from jax.experimental.pallas import tpu as pltpu

F32 = jnp.float32
BF16 = jnp.bfloat16

D_MODEL = 1024
N_HEADS = 8
Q_LORA = 256
KV_LORA = 128
QK_NOPE = 64
QK_ROPE = 32
HALF_ROPE = QK_ROPE // 2
V_HEAD = 64
QK_HEAD = QK_NOPE + QK_ROPE
ATTN_SCALE = 1.0 / math.sqrt(QK_HEAD)
Q_SCALE = ATTN_SCALE * math.log2(math.e)
ROPE_THETA = 10000.0
CONV_CH = 512
CONV_K = 31
N_MOD = 6
EPS_RMS = 1e-6
EPS_LN = 1e-5

LANES = 128
HEAD_PAD = LANES
VMEM_LIMIT = 56 * 1024 * 1024

C_QA = 0
C_KVA = C_QA + Q_LORA
C_KR = C_KVA + KV_LORA
C_CONV = C_KR + 2 * HEAD_PAD
C_GATE = C_CONV + 2 * CONV_CH
D_IN_EXT = C_GATE + 2 * D_MODEL

TS_IN = 256
TQ = 512
TS_MIX = 256
HALO = 16
TM_FFN = 512


def _sigmoid(x):
    return 1.0 / (1.0 + jnp.exp(-x))


def _rms(x, g):
    return x * lax.rsqrt(jnp.mean(x * x, axis=-1, keepdims=True) + EPS_RMS) * g


def _dot(a, b):
    return jnp.dot(a, b, preferred_element_type=F32)


def _const_spec(shape):
    nd = len(shape)
    return pl.BlockSpec(shape, lambda *_: (0,) * nd, pipeline_mode=pl.Buffered(1))


def _rope_kernel(pos_ref, freq_ref, cos_ref, sin_ref):
    ang = pos_ref[...] * freq_ref[...]
    cos_ref[...] = jnp.cos(ang)
    sin_ref[...] = jnp.sin(ang)


def _rope_tables(positions):
    B, S = positions.shape
    inv_freq = ROPE_THETA ** (-jnp.arange(0, QK_ROPE, 2, dtype=F32) / QK_ROPE)
    rows = B * S * HALF_ROPE // LANES
    pos_rep = jnp.repeat(positions.reshape(-1).astype(F32), HALF_ROPE).reshape(rows, LANES)
    freq = jnp.tile(inv_freq, LANES // HALF_ROPE)[None, :]
    cos, sin = pl.pallas_call(
        _rope_kernel,
        out_shape=(jax.ShapeDtypeStruct((rows, LANES), F32),) * 2,
        name="rope_tables",
    )(pos_rep, freq)
    cos = cos.reshape(B, S, HALF_ROPE)
    sin = sin.reshape(B, S, HALF_ROPE)
    ones = jnp.ones((B, S, QK_NOPE), F32)
    zeros_n = jnp.zeros((B, S, QK_NOPE), F32)
    zeros_p = jnp.zeros((B, S, HEAD_PAD - QK_HEAD), F32)
    ctab = jnp.concatenate([ones, cos, cos, zeros_p], axis=-1)
    stab = jnp.concatenate([zeros_n, sin, sin, zeros_p], axis=-1)
    return ctab, stab


def _mod_kernel(c_ref, w_ref, b_ref, o_ref):
    c = c_ref[...]
    c_act = (c * _sigmoid(c)).astype(BF16)
    o_ref[...] = _dot(c_act, w_ref[...].astype(BF16)) + b_ref[...]


def _modulation(c, w_ada, b_ada):
    B, D = c.shape
    n = w_ada.shape[1]
    bn = 1536
    return pl.pallas_call(
        _mod_kernel,
        out_shape=jax.ShapeDtypeStruct((B, n), F32),
        grid=(n // bn,),
        in_specs=[
            pl.BlockSpec((B, D), lambda j: (0, 0)),
            pl.BlockSpec((D, bn), lambda j: (0, j)),
            pl.BlockSpec((1, bn), lambda j: (0, j)),
        ],
        out_specs=pl.BlockSpec((B, bn), lambda j: (0, j)),
        compiler_params=pltpu.CompilerParams(
            dimension_semantics=("parallel",), vmem_limit_bytes=VMEM_LIMIT),
        name="adaln_modulation",
    )(c, w_ada, b_ada)


def _inproj_kernel(x_ref, mod_ref, gmix_ref, win_ref, gq_ref, wq_ref, gkv_ref, wk_ref, wvt_ref,
                   ctab_ref, stab_ref,
                   q_ref, k_ref, vt_ref, z_ref, gates_ref):
    D = D_MODEL
    shift = mod_ref[:, 0:D]
    scale = mod_ref[:, D:2 * D]
    h = (_rms(x_ref[...], gmix_ref[...]) * (1.0 + scale) + shift).astype(BF16)
    ctab = ctab_ref[...]
    stab = stab_ref[...]

    qa = _dot(h, win_ref[:, C_QA:C_QA + Q_LORA])
    qn = _rms(qa, gq_ref[...]).astype(BF16)
    nq = N_HEADS * HEAD_PAD
    q_main = _dot(qn, wq_ref[:, 0:nq])
    q_swap = _dot(qn, wq_ref[:, nq:2 * nq])
    for hd in range(N_HEADS):
        sl = slice(hd * HEAD_PAD, (hd + 1) * HEAD_PAD)
        q_ref[:, sl] = ((q_main[:, sl] * ctab + q_swap[:, sl] * stab) * Q_SCALE).astype(BF16)

    kva = _dot(h, win_ref[:, C_KVA:C_KVA + KV_LORA])
    kvn = _rms(kva, gkv_ref[...]).astype(BF16)
    k_pad = _dot(kvn, wk_ref[...])
    vt_ref[...] = lax.dot_general(wvt_ref[...], kvn, (((1,), (1,)), ((), ())),
                                  preferred_element_type=F32).astype(BF16)
    kr = _dot(h, win_ref[:, C_KR:C_KR + 2 * HEAD_PAD])
    kr_rot = kr[:, 0:HEAD_PAD] * ctab + kr[:, HEAD_PAD:2 * HEAD_PAD] * stab
    for hd in range(N_HEADS):
        sl = slice(hd * HEAD_PAD, (hd + 1) * HEAD_PAD)
        k_ref[:, sl] = (k_pad[:, sl] + kr_rot).astype(BF16)

    cu = _dot(h, win_ref[:, C_CONV:C_CONV + 2 * CONV_CH])
    z_ref[...] = cu[:, 0:CONV_CH] * _sigmoid(cu[:, CONV_CH:2 * CONV_CH])

    gates_ref[...] = _sigmoid(_dot(h, win_ref[:, C_GATE:C_GATE + 2 * D]))


def _inproj(x, mod3, g_mix, w_in_ext, g_q, wq2, g_kv, wk, wvt, ctab, stab):
    B, S, D = x.shape
    ts = TS_IN
    nq = N_HEADS * HEAD_PAD
    nv = N_HEADS * V_HEAD
    tok = lambda w: pl.BlockSpec((None, ts, w), lambda b, i: (b, i, 0))
    return pl.pallas_call(
        _inproj_kernel,
        out_shape=(
            jax.ShapeDtypeStruct((B, S, nq), BF16),
            jax.ShapeDtypeStruct((B, S, nq), BF16),
            jax.ShapeDtypeStruct((B, nv, S), BF16),
            jax.ShapeDtypeStruct((B, S, CONV_CH), F32),
            jax.ShapeDtypeStruct((B, S, 2 * D), F32),
        ),
        grid=(B, S // ts),
        in_specs=[
            tok(D),
            pl.BlockSpec((None, 1, N_MOD * D), lambda b, i: (b, 0, 0)),
            _const_spec(g_mix.shape),
            _const_spec(w_in_ext.shape),
            _const_spec(g_q.shape),
            _const_spec(wq2.shape),
            _const_spec(g_kv.shape),
            _const_spec(wk.shape),
            _const_spec(wvt.shape),
            tok(HEAD_PAD),
            tok(HEAD_PAD),
        ],
        out_specs=(tok(nq), tok(nq), pl.BlockSpec((None, nv, ts), lambda b, i: (b, 0, i)),
                   tok(CONV_CH), tok(2 * D)),
        compiler_params=pltpu.CompilerParams(
            dimension_semantics=("parallel", "parallel"), vmem_limit_bytes=VMEM_LIMIT),
        name="input_projection",
    )(x, mod3, g_mix, w_in_ext, g_q, wq2, g_kv, wk, wvt, ctab, stab)


def _attn_kernel(q_ref, k_ref, vt_ref, o_ref):
    def scores_t(hd):
        sl = slice(hd * HEAD_PAD, (hd + 1) * HEAD_PAD)
        return lax.dot_general(k_ref[:, sl], q_ref[:, sl], (((1,), (1,)), ((), ())),
                               preferred_element_type=F32)

    st_next = scores_t(0)
    for hp in range(N_HEADS // 2):
        outs = []
        for j in range(2):
            hd = 2 * hp + j
            st = st_next
            if hd + 1 < N_HEADS:
                st_next = scores_t(hd + 1)
            m = jnp.max(st, axis=0, keepdims=True)
            p = jnp.exp2(st - m)
            l = jnp.sum(p, axis=0, keepdims=True)
            ot = _dot(vt_ref[hd * V_HEAD:(hd + 1) * V_HEAD, :], p.astype(BF16))
            outs.append(ot / l)
        o_pair = jnp.concatenate(outs, axis=0).T
        o_ref[:, hp * 2 * V_HEAD:(hp + 1) * 2 * V_HEAD] = o_pair.astype(BF16)


def _attention(q, k, vt):
    B, S, nq = q.shape
    nv = N_HEADS * V_HEAD
    return pl.pallas_call(
        _attn_kernel,
        out_shape=jax.ShapeDtypeStruct((B, S, nv), BF16),
        grid=(B, S // TQ),
        in_specs=[
            pl.BlockSpec((None, TQ, nq), lambda b, i: (b, i, 0)),
            pl.BlockSpec((None, S, nq), lambda b, i: (b, 0, 0)),
            pl.BlockSpec((None, nv, S), lambda b, i: (b, 0, 0)),
        ],
        out_specs=pl.BlockSpec((None, TQ, nv), lambda b, i: (b, i, 0)),
        compiler_params=pltpu.CompilerParams(
            dimension_semantics=("parallel", "parallel"), vmem_limit_bytes=VMEM_LIMIT),
        name="mla_attention",
    )(q, k, vt)


def _mix_kernel(x_ref, mod_ref, z_ref, zprev_ref, znext_ref, o_ref, gates_ref,
                wdw_ref, bdw_ref, gln_ref, bln_ref, wco_ref, wao_ref, wout_ref,
                out_ref, zp_ref, conv_ref):
    D = D_MODEL
    ts = TS_MIX
    i = pl.program_id(1)
    n_i = pl.num_programs(1)

    zp_ref[0:HALO, :] = jnp.where(i > 0, zprev_ref[...], 0.0)
    zp_ref[HALO:HALO + ts, :] = z_ref[...]
    zp_ref[HALO + ts:2 * HALO + ts, :] = jnp.where(i < n_i - 1, znext_ref[...], 0.0)

    row_chunk = 64
    base = HALO - CONV_K // 2
    for cb in range(CONV_CH // LANES):
        cs = slice(cb * LANES, (cb + 1) * LANES)
        for rb in range(ts // row_chunk):
            r0 = rb * row_chunk
            acc = jnp.broadcast_to(bdw_ref[:, cs], (row_chunk, LANES))
            for kk in range(CONV_K):
                acc = acc + wdw_ref[kk:kk + 1, cs] * zp_ref[r0 + base + kk:r0 + base + kk + row_chunk, cs]
            conv_ref[r0:r0 + row_chunk, cs] = acc

    zc = conv_ref[...]
    mu = jnp.mean(zc, axis=-1, keepdims=True)
    zd = zc - mu
    var = jnp.mean(zd * zd, axis=-1, keepdims=True)
    zn = zd * lax.rsqrt(var + EPS_LN) * gln_ref[...] + bln_ref[...]
    zs = (zn * _sigmoid(zn)).astype(BF16)
    y_b = _dot(zs, wco_ref[...])
    y_a = _dot(o_ref[...], wao_ref[...])
    merged = (gates_ref[:, 0:D] * y_a + gates_ref[:, D:2 * D] * y_b).astype(BF16)
    gate_m = mod_ref[:, 2 * D:3 * D]
    out_ref[...] = x_ref[...] + gate_m * _dot(merged, wout_ref[...])


def _mix(x, mod3, z, o, gates, w_dw, b_dw, g_ln, b_ln, w_co, w_ao, w_out):
    B, S, D = x.shape
    ts = TS_MIX
    hb = ts // HALO
    n_halo = S // HALO
    tok = lambda w: pl.BlockSpec((None, ts, w), lambda b, i: (b, i, 0))
    return pl.pallas_call(
        _mix_kernel,
        out_shape=jax.ShapeDtypeStruct((B, S, D), F32),
        grid=(B, S // ts),
        in_specs=[
            tok(D),
            pl.BlockSpec((None, 1, N_MOD * D), lambda b, i: (b, 0, 0)),
            tok(CONV_CH),
            pl.BlockSpec((None, HALO, CONV_CH), lambda b, i: (b, jnp.maximum(i * hb - 1, 0), 0)),
            pl.BlockSpec((None, HALO, CONV_CH),
                         lambda b, i: (b, jnp.minimum((i + 1) * hb, n_halo - 1), 0)),
            tok(N_HEADS * V_HEAD),
            tok(2 * D),
            _const_spec(w_dw.shape),
            _const_spec(b_dw.shape),
            _const_spec(g_ln.shape),
            _const_spec(b_ln.shape),
            _const_spec(w_co.shape),
            _const_spec(w_ao.shape),
            _const_spec(w_out.shape),
        ],
        out_specs=tok(D),
        scratch_shapes=[
            pltpu.VMEM((ts + 2 * HALO, CONV_CH), F32),
            pltpu.VMEM((ts, CONV_CH), F32),
        ],
        compiler_params=pltpu.CompilerParams(
            dimension_semantics=("parallel", "parallel"), vmem_limit_bytes=VMEM_LIMIT),
        name="conv_merge_out",
    )(x, mod3, z, z, z, o, gates, w_dw, b_dw, g_ln, b_ln, w_co, w_ao, w_out)


def _ffn_kernel(x_ref, mod_ref, gffn_ref, wg_ref, wu_ref, wd_ref, gfin_ref, out_ref, *, final_norm):
    D = D_MODEL
    x = x_ref[...]
    shift = mod_ref[:, 3 * D:4 * D]
    scale = mod_ref[:, 4 * D:5 * D]
    gate = mod_ref[:, 5 * D:6 * D]
    h = (_rms(x, gffn_ref[...]) * (1.0 + scale) + shift).astype(BF16)
    g = _dot(h, wg_ref[...])
    u = _dot(h, wu_ref[...])
    a = (g * _sigmoid(g) * u).astype(BF16)
    x2 = x + gate * _dot(a, wd_ref[...])
    out_ref[...] = _rms(x2, gfin_ref[...]) if final_norm else x2


def _ffn(x, mod3, g_ffn, w_gate, w_up, w_down, g_final, final_norm):
    B, S, D = x.shape
    tm = TM_FFN
    tok = pl.BlockSpec((None, tm, D), lambda b, i: (b, i, 0))
    return pl.pallas_call(
        functools.partial(_ffn_kernel, final_norm=final_norm),
        out_shape=jax.ShapeDtypeStruct((B, S, D), F32),
        grid=(B, S // tm),
        in_specs=[
            tok,
            pl.BlockSpec((None, 1, N_MOD * D), lambda b, i: (b, 0, 0)),
            _const_spec(g_ffn.shape),
            _const_spec(w_gate.shape),
            _const_spec(w_up.shape),
            _const_spec(w_down.shape),
            _const_spec(g_final.shape),
        ],
        out_specs=tok,
        compiler_params=pltpu.CompilerParams(
            dimension_semantics=("parallel", "parallel"), vmem_limit_bytes=VMEM_LIMIT),
        name="swiglu_final_norm",
    )(x, mod3, g_ffn, w_gate, w_up, w_down, g_final)


def _prep_w_in(w_in):
    D = w_in.shape[0]
    o_kr = Q_LORA + KV_LORA
    kr1 = w_in[:, o_kr:o_kr + HALF_ROPE]
    kr2 = w_in[:, o_kr + HALF_ROPE:o_kr + QK_ROPE]
    zn = jnp.zeros((D, QK_NOPE), w_in.dtype)
    zp = jnp.zeros((D, HEAD_PAD - QK_HEAD), w_in.dtype)
    return jnp.concatenate(
        [w_in[:, :o_kr], zn, kr1, kr2, zp, zn, -kr2, kr1, zp, w_in[:, o_kr + QK_ROPE:]],
        axis=1).astype(BF16)


def _prep_w_q(w_q_up):
    r = w_q_up.shape[0]
    w = w_q_up.reshape(r, N_HEADS, QK_HEAD)
    nope = w[..., :QK_NOPE]
    r1 = w[..., QK_NOPE:QK_NOPE + HALF_ROPE]
    r2 = w[..., QK_NOPE + HALF_ROPE:]
    zp = jnp.zeros((r, N_HEADS, HEAD_PAD - QK_HEAD), w.dtype)
    main = jnp.concatenate([nope, r1, r2, zp], axis=-1).reshape(r, N_HEADS * HEAD_PAD)
    swap = jnp.concatenate([jnp.zeros_like(nope), -r2, r1, zp], axis=-1).reshape(r, N_HEADS * HEAD_PAD)
    return jnp.concatenate([main, swap], axis=1).astype(BF16)


def _prep_w_kv(w_kv_up):
    r = w_kv_up.shape[0]
    w = w_kv_up.reshape(r, N_HEADS, QK_NOPE + V_HEAD)
    k_nope = w[..., :QK_NOPE]
    v = w[..., QK_NOPE:]
    k_pad = jnp.concatenate(
        [k_nope, jnp.zeros((r, N_HEADS, HEAD_PAD - QK_NOPE), w.dtype)], axis=-1)
    wk = k_pad.reshape(r, N_HEADS * HEAD_PAD).astype(BF16)
    wvt = v.reshape(r, N_HEADS * V_HEAD).T.astype(BF16)
    return wk, wvt


def kernel(x, c, positions, w_ada, b_ada, g_norm_mix, w_in, g_q_a, w_q_up, g_kv_a, w_kv_up,
           w_attn_o, w_dw, b_dw, g_conv_ln, b_conv_ln, w_conv_out, w_out, g_norm_ffn,
           w_ffn_gate, w_ffn_up, w_ffn_down, g_final):
    B, S, D = x.shape
    depth = w_ada.shape[0]
    ctab, stab = _rope_tables(positions)
    for l in range(depth):
        mod3 = _modulation(c, w_ada[l], b_ada[l][None, :]).reshape(B, 1, N_MOD * D)
        wk, wvt = _prep_w_kv(w_kv_up[l])
        q, k, vt, z, gates = _inproj(
            x, mod3, g_norm_mix[l][None, :], _prep_w_in(w_in[l]), g_q_a[l][None, :],
            _prep_w_q(w_q_up[l]), g_kv_a[l][None, :], wk, wvt, ctab, stab)
        o = _attention(q, k, vt)
        x = _mix(x, mod3, z, o, gates, w_dw[l], b_dw[l][None, :], g_conv_ln[l][None, :],
                 b_conv_ln[l][None, :], w_conv_out[l].astype(BF16), w_attn_o[l].astype(BF16),
                 w_out[l].astype(BF16))
        x = _ffn(x, mod3, g_norm_ffn[l][None, :], w_ffn_gate[l].astype(BF16),
                 w_ffn_up[l].astype(BF16), w_ffn_down[l].astype(BF16), g_final[None, :],
                 final_norm=(l == depth - 1))
    return x
```

```python
import functools
import math

import jax
import jax.numpy as jnp
from jax import lax
from jax.experimental import pallas as pl
from jax.experimental.pallas import tpu as pltpu

F32 = jnp.float32
BF16 = jnp.bfloat16

D_MODEL = 1024
N_HEADS = 8
Q_LORA = 256
KV_LORA = 128
QK_NOPE = 64
QK_ROPE = 32
HALF_ROPE = QK_ROPE // 2
V_HEAD = 64
QK_HEAD = QK_NOPE + QK_ROPE
ATTN_SCALE = 1.0 / math.sqrt(QK_HEAD)
Q_SCALE = ATTN_SCALE * math.log2(math.e)
ROPE_THETA = 10000.0
CONV_CH = 512
CONV_K = 31
N_MOD = 6
EPS_RMS = 1e-6
EPS_LN = 1e-5

LANES = 128
SUBLANES = 8
HEAD_PAD = LANES
VMEM_LIMIT = 56 * 1024 * 1024

C_QA = 0
C_KVA = C_QA + Q_LORA
C_KR = C_KVA + KV_LORA
C_CONV = C_KR + 2 * HEAD_PAD
C_GATE = C_CONV + 2 * CONV_CH
D_IN_EXT = C_GATE + 2 * D_MODEL

TS_IN = 512
TQ = 512
TS_MIX = 512
HALO = 16
TM_FFN = 512


def _sigmoid(x):
    return 1.0 / (1.0 + jnp.exp(-x))


def _rms(x, g):
    return x * lax.rsqrt(jnp.mean(x * x, axis=-1, keepdims=True) + EPS_RMS) * g


def _dot(a, b):
    return jnp.dot(a, b, preferred_element_type=F32)


def _const_spec(shape):
    nd = len(shape)
    return pl.BlockSpec(shape, lambda *_: (0,) * nd, pipeline_mode=pl.Buffered(1))


def _rope_kernel(pos_ref, freq_ref, cos_ref, sin_ref):
    ang = pos_ref[...] * freq_ref[...]
    cos_ref[...] = jnp.cos(ang)
    sin_ref[...] = jnp.sin(ang)


def _rope_tables(positions):
    B, S = positions.shape
    inv_freq = ROPE_THETA ** (-jnp.arange(0, QK_ROPE, 2, dtype=F32) / QK_ROPE)
    rows = B * S * HALF_ROPE // LANES
    pos_rep = jnp.repeat(positions.reshape(-1).astype(F32), HALF_ROPE).reshape(rows, LANES)
    freq = jnp.tile(inv_freq, LANES // HALF_ROPE)[None, :]
    cos, sin = pl.pallas_call(
        _rope_kernel,
        out_shape=(jax.ShapeDtypeStruct((rows, LANES), F32),) * 2,
        name="rope_tables",
    )(pos_rep, freq)
    cos = cos.reshape(B, S, HALF_ROPE)
    sin = sin.reshape(B, S, HALF_ROPE)
    ones = jnp.ones((B, S, QK_NOPE), F32)
    zeros_n = jnp.zeros((B, S, QK_NOPE), F32)
    zeros_p = jnp.zeros((B, S, HEAD_PAD - QK_HEAD), F32)
    ctab = jnp.concatenate([ones, cos, cos, zeros_p], axis=-1)
    stab = jnp.concatenate([zeros_n, sin, sin, zeros_p], axis=-1)
    return ctab, stab


def _mod_kernel(c_ref, w_ref, b_ref, o_ref):
    c = c_ref[...]
    c_act = (c * _sigmoid(c)).astype(BF16)
    o_ref[...] = _dot(c_act, w_ref[...].astype(BF16)) + b_ref[...]


def _modulation(c, w_ada, b_ada):
    B, D = c.shape
    n = w_ada.shape[1]
    bn = 1536
    return pl.pallas_call(
        _mod_kernel,
        out_shape=jax.ShapeDtypeStruct((B, n), F32),
        grid=(n // bn,),
        in_specs=[
            pl.BlockSpec((B, D), lambda j: (0, 0)),
            pl.BlockSpec((D, bn), lambda j: (0, j)),
            pl.BlockSpec((1, bn), lambda j: (0, j)),
        ],
        out_specs=pl.BlockSpec((B, bn), lambda j: (0, j)),
        compiler_params=pltpu.CompilerParams(
            dimension_semantics=("parallel",), vmem_limit_bytes=VMEM_LIMIT),
        name="adaln_modulation",
    )(c, w_ada, b_ada)


def _mixer_input(x_ref, mod_ref, gmix_ref):
    D = D_MODEL
    shift = mod_ref[:, 0:D]
    scale = mod_ref[:, D:2 * D]
    return (_rms(x_ref[...], gmix_ref[...]) * (1.0 + scale) + shift).astype(BF16)


def _inproj_kernel(x_ref, mod_ref, gmix_ref, win_ref, gq_ref, wq_ref, gkv_ref, wk_ref, wvt_ref,
                   ctab_ref, stab_ref,
                   q_ref, k_ref, vt_ref, z_ref, h_ref):
    h = _mixer_input(x_ref, mod_ref, gmix_ref)
    h_ref[...] = h
    ctab = ctab_ref[...]
    stab = stab_ref[...]

    qa = _dot(h, win_ref[:, C_QA:C_QA + Q_LORA])
    qn = _rms(qa, gq_ref[...]).astype(BF16)
    nq = N_HEADS * HEAD_PAD
    q_main = _dot(qn, wq_ref[:, 0:nq])
    q_swap = _dot(qn, wq_ref[:, nq:2 * nq])
    for hd in range(N_HEADS):
        sl = slice(hd * HEAD_PAD, (hd + 1) * HEAD_PAD)
        q_ref[:, sl] = ((q_main[:, sl] * ctab + q_swap[:, sl] * stab) * Q_SCALE).astype(BF16)

    kva = _dot(h, win_ref[:, C_KVA:C_KVA + KV_LORA])
    kvn = _rms(kva, gkv_ref[...]).astype(BF16)
    k_pad = _dot(kvn, wk_ref[...])
    vt_ref[...] = lax.dot_general(wvt_ref[...], kvn, (((1,), (1,)), ((), ())),
                                  preferred_element_type=F32).astype(BF16)
    kr = _dot(h, win_ref[:, C_KR:C_KR + 2 * HEAD_PAD])
    kr_rot = kr[:, 0:HEAD_PAD] * ctab + kr[:, HEAD_PAD:2 * HEAD_PAD] * stab
    for hd in range(N_HEADS):
        sl = slice(hd * HEAD_PAD, (hd + 1) * HEAD_PAD)
        k_ref[:, sl] = (k_pad[:, sl] + kr_rot).astype(BF16)

    cu = _dot(h, win_ref[:, C_CONV:C_CONV + 2 * CONV_CH])
    z_ref[...] = cu[:, 0:CONV_CH] * _sigmoid(cu[:, CONV_CH:2 * CONV_CH])


def _inproj(x, mod3, g_mix, w_in_ext, g_q, wq2, g_kv, wk, wvt, ctab, stab):
    B, S, D = x.shape
    ts = TS_IN
    nq = N_HEADS * HEAD_PAD
    nv = N_HEADS * V_HEAD
    tok = lambda w: pl.BlockSpec((None, ts, w), lambda b, i: (b, i, 0))
    return pl.pallas_call(
        _inproj_kernel,
        out_shape=(
            jax.ShapeDtypeStruct((B, S, nq), BF16),
            jax.ShapeDtypeStruct((B, S, nq), BF16),
            jax.ShapeDtypeStruct((B, nv, S), BF16),
            jax.ShapeDtypeStruct((B, S, CONV_CH), F32),
            jax.ShapeDtypeStruct((B, S, D), BF16),
        ),
        grid=(B, S // ts),
        in_specs=[
            tok(D),
            pl.BlockSpec((None, 1, N_MOD * D), lambda b, i: (b, 0, 0)),
            _const_spec(g_mix.shape),
            _const_spec(w_in_ext.shape),
            _const_spec(g_q.shape),
            _const_spec(wq2.shape),
            _const_spec(g_kv.shape),
            _const_spec(wk.shape),
            _const_spec(wvt.shape),
            tok(HEAD_PAD),
            tok(HEAD_PAD),
        ],
        out_specs=(tok(nq), tok(nq), pl.BlockSpec((None, nv, ts), lambda b, i: (b, 0, i)),
                   tok(CONV_CH), tok(D)),
        compiler_params=pltpu.CompilerParams(
            dimension_semantics=("parallel", "parallel"), vmem_limit_bytes=VMEM_LIMIT),
        name="input_projection",
    )(x, mod3, g_mix, w_in_ext, g_q, wq2, g_kv, wk, wvt, ctab, stab)


def _attn_kernel(q_ref, k_ref, vt_ref, o_ref):
    def scores_t(hd):
        sl = slice(hd * HEAD_PAD, (hd + 1) * HEAD_PAD)
        return lax.dot_general(k_ref[:, sl], q_ref[:, sl], (((1,), (1,)), ((), ())),
                               preferred_element_type=F32)

    st_next = scores_t(0)
    for hp in range(N_HEADS // 2):
        outs = []
        for j in range(2):
            hd = 2 * hp + j
            st = st_next
            if hd + 1 < N_HEADS:
                st_next = scores_t(hd + 1)
            m = jnp.max(st, axis=0, keepdims=True)
            p = jnp.exp2(st - m)
            l = jnp.sum(p, axis=0, keepdims=True)
            ot = _dot(vt_ref[hd * V_HEAD:(hd + 1) * V_HEAD, :], p.astype(BF16))
            outs.append(ot / l)
        o_pair = jnp.concatenate(outs, axis=0).T
        o_ref[:, hp * 2 * V_HEAD:(hp + 1) * 2 * V_HEAD] = o_pair.astype(BF16)


def _attention(q, k, vt):
    B, S, nq = q.shape
    nv = N_HEADS * V_HEAD
    return pl.pallas_call(
        _attn_kernel,
        out_shape=jax.ShapeDtypeStruct((B, S, nv), BF16),
        grid=(B, S // TQ),
        in_specs=[
            pl.BlockSpec((None, TQ, nq), lambda b, i: (b, i, 0)),
            pl.BlockSpec((None, S, nq), lambda b, i: (b, 0, 0)),
            pl.BlockSpec((None, nv, S), lambda b, i: (b, 0, 0)),
        ],
        out_specs=pl.BlockSpec((None, TQ, nv), lambda b, i: (b, i, 0)),
        compiler_params=pltpu.CompilerParams(
            dimension_semantics=("parallel", "parallel"), vmem_limit_bytes=VMEM_LIMIT),
        name="mla_attention",
    )(q, k, vt)


def _mix_kernel(x_ref, mod_ref, z_ref, zprev_ref, znext_ref, o_ref, h_ref, wgate_ref,
                wdw_ref, bdw_ref, gln_ref, bln_ref, wco_ref, wao_ref, wout_ref,
                out_ref, zp_ref, zs_ref, conv_ref):
    D = D_MODEL
    ts = TS_MIX
    i = pl.program_id(1)
    n_i = pl.num_programs(1)

    zp_ref[0:HALO, :] = jnp.where(i > 0, zprev_ref[...], 0.0)
    zp_ref[HALO:HALO + ts, :] = z_ref[...]
    zp_ref[HALO + ts:2 * HALO + ts, :] = jnp.where(i < n_i - 1, znext_ref[...], 0.0)

    n_shift = zs_ref.shape[1]
    for s in range(1, SUBLANES):
        zs_ref[s - 1] = zp_ref[s:s + n_shift, :]

    gate_logits = _dot(h_ref[...], wgate_ref[...])
    y_a = _dot(o_ref[...], wao_ref[...])

    row_chunk = 64
    base = HALO - CONV_K // 2
    for cb in range(CONV_CH // LANES):
        cs = slice(cb * LANES, (cb + 1) * LANES)
        for rb in range(ts // row_chunk):
            r0 = rb * row_chunk
            acc = jnp.broadcast_to(bdw_ref[:, cs], (row_chunk, LANES))
            for kk in range(CONV_K):
                s = (base + kk) % SUBLANES
                a = r0 + base + kk - s
                src = zp_ref if s == 0 else zs_ref.at[s - 1]
                acc = acc + wdw_ref[kk:kk + 1, cs] * src[a:a + row_chunk, cs]
            conv_ref[r0:r0 + row_chunk, cs] = acc

    zc = conv_ref[...]
    mu = jnp.mean(zc, axis=-1, keepdims=True)
    zd = zc - mu
    var = jnp.mean(zd * zd, axis=-1, keepdims=True)
    zn = zd * lax.rsqrt(var + EPS_LN) * gln_ref[...] + bln_ref[...]
    zs = (zn * _sigmoid(zn)).astype(BF16)
    y_b = _dot(zs, wco_ref[...])
    gates = _sigmoid(gate_logits)
    merged = (gates[:, 0:D] * y_a + gates[:, D:2 * D] * y_b).astype(BF16)
    gate_m = mod_ref[:, 2 * D:3 * D]
    out_ref[...] = x_ref[...] + gate_m * _dot(merged, wout_ref[...])


def _mix(x, mod3, z, o, h, w_gate, w_dw, b_dw, g_ln, b_ln, w_co, w_ao, w_out):
    B, S, D = x.shape
    ts = TS_MIX
    hb = ts // HALO
    n_halo = S // HALO
    tok = lambda w: pl.BlockSpec((None, ts, w), lambda b, i: (b, i, 0))
    return pl.pallas_call(
        _mix_kernel,
        out_shape=jax.ShapeDtypeStruct((B, S, D), F32),
        grid=(B, S // ts),
        in_specs=[
            tok(D),
            pl.BlockSpec((None, 1, N_MOD * D), lambda b, i: (b, 0, 0)),
            tok(CONV_CH),
            pl.BlockSpec((None, HALO, CONV_CH), lambda b, i: (b, jnp.maximum(i * hb - 1, 0), 0)),
            pl.BlockSpec((None, HALO, CONV_CH),
                         lambda b, i: (b, jnp.minimum((i + 1) * hb, n_halo - 1), 0)),
            tok(N_HEADS * V_HEAD),
            tok(D),
            _const_spec(w_gate.shape),
            _const_spec(w_dw.shape),
            _const_spec(b_dw.shape),
            _const_spec(g_ln.shape),
            _const_spec(b_ln.shape),
            _const_spec(w_co.shape),
            _const_spec(w_ao.shape),
            _const_spec(w_out.shape),
        ],
        out_specs=tok(D),
        scratch_shapes=[
            pltpu.VMEM((ts + 2 * HALO, CONV_CH), F32),
            pltpu.VMEM((SUBLANES - 1, ts + 2 * HALO - SUBLANES, CONV_CH), F32),
            pltpu.VMEM((ts, CONV_CH), F32),
        ],
        compiler_params=pltpu.CompilerParams(
            dimension_semantics=("parallel", "parallel"), vmem_limit_bytes=VMEM_LIMIT),
        name="conv_merge_out",
    )(x, mod3, z, z, z, o, h, w_gate, w_dw, b_dw, g_ln, b_ln, w_co, w_ao, w_out)


def _ffn_kernel(x_ref, mod_ref, gffn_ref, wg_ref, wu_ref, wd_ref, gfin_ref, out_ref, *, final_norm):
    D = D_MODEL
    x = x_ref[...]
    shift = mod_ref[:, 3 * D:4 * D]
    scale = mod_ref[:, 4 * D:5 * D]
    gate = mod_ref[:, 5 * D:6 * D]
    h = (_rms(x, gffn_ref[...]) * (1.0 + scale) + shift).astype(BF16)
    g = _dot(h, wg_ref[...])
    u = _dot(h, wu_ref[...])
    a = (g * _sigmoid(g) * u).astype(BF16)
    x2 = x + gate * _dot(a, wd_ref[...])
    out_ref[...] = _rms(x2, gfin_ref[...]) if final_norm else x2


def _ffn(x, mod3, g_ffn, w_gate, w_up, w_down, g_final, final_norm):
    B, S, D = x.shape
    tm = TM_FFN
    tok = pl.BlockSpec((None, tm, D), lambda b, i: (b, i, 0))
    return pl.pallas_call(
        functools.partial(_ffn_kernel, final_norm=final_norm),
        out_shape=jax.ShapeDtypeStruct((B, S, D), F32),
        grid=(B, S // tm),
        in_specs=[
            tok,
            pl.BlockSpec((None, 1, N_MOD * D), lambda b, i: (b, 0, 0)),
            _const_spec(g_ffn.shape),
            _const_spec(w_gate.shape),
            _const_spec(w_up.shape),
            _const_spec(w_down.shape),
            _const_spec(g_final.shape),
        ],
        out_specs=tok,
        compiler_params=pltpu.CompilerParams(
            dimension_semantics=("parallel", "parallel"), vmem_limit_bytes=VMEM_LIMIT),
        name="swiglu_final_norm",
    )(x, mod3, g_ffn, w_gate, w_up, w_down, g_final)


def _prep_w_in(w_in):
    D = w_in.shape[0]
    o_kr = Q_LORA + KV_LORA
    o_conv = o_kr + QK_ROPE
    o_gate = o_conv + 2 * CONV_CH
    kr1 = w_in[:, o_kr:o_kr + HALF_ROPE]
    kr2 = w_in[:, o_kr + HALF_ROPE:o_kr + QK_ROPE]
    zn = jnp.zeros((D, QK_NOPE), w_in.dtype)
    zp = jnp.zeros((D, HEAD_PAD - QK_HEAD), w_in.dtype)
    w_proj = jnp.concatenate(
        [w_in[:, :o_kr], zn, kr1, kr2, zp, zn, -kr2, kr1, zp, w_in[:, o_conv:o_gate]],
        axis=1).astype(BF16)
    return w_proj, w_in[:, o_gate:].astype(BF16)


def _prep_w_q(w_q_up):
    r = w_q_up.shape[0]
    w = w_q_up.reshape(r, N_HEADS, QK_HEAD)
    nope = w[..., :QK_NOPE]
    r1 = w[..., QK_NOPE:QK_NOPE + HALF_ROPE]
    r2 = w[..., QK_NOPE + HALF_ROPE:]
    zp = jnp.zeros((r, N_HEADS, HEAD_PAD - QK_HEAD), w.dtype)
    main = jnp.concatenate([nope, r1, r2, zp], axis=-1).reshape(r, N_HEADS * HEAD_PAD)
    swap = jnp.concatenate([jnp.zeros_like(nope), -r2, r1, zp], axis=-1).reshape(r, N_HEADS * HEAD_PAD)
    return jnp.concatenate([main, swap], axis=1).astype(BF16)


def _prep_w_kv(w_kv_up):
    r = w_kv_up.shape[0]
    w = w_kv_up.reshape(r, N_HEADS, QK_NOPE + V_HEAD)
    k_nope = w[..., :QK_NOPE]
    v = w[..., QK_NOPE:]
    k_pad = jnp.concatenate(
        [k_nope, jnp.zeros((r, N_HEADS, HEAD_PAD - QK_NOPE), w.dtype)], axis=-1)
    wk = k_pad.reshape(r, N_HEADS * HEAD_PAD).astype(BF16)
    wvt = v.reshape(r, N_HEADS * V_HEAD).T.astype(BF16)
    return wk, wvt


def kernel(x, c, positions, w_ada, b_ada, g_norm_mix, w_in, g_q_a, w_q_up, g_kv_a, w_kv_up,
           w_attn_o, w_dw, b_dw, g_conv_ln, b_conv_ln, w_conv_out, w_out, g_norm_ffn,
           w_ffn_gate, w_ffn_up, w_ffn_down, g_final):
    B, S, D = x.shape
    depth = w_ada.shape[0]
    ctab, stab = _rope_tables(positions)
    for l in range(depth):
        mod3 = _modulation(c, w_ada[l], b_ada[l][None, :]).reshape(B, 1, N_MOD * D)
        wk, wvt = _prep_w_kv(w_kv_up[l])
        w_proj, w_gate = _prep_w_in(w_in[l])
        q, k, vt, z, h = _inproj(
            x, mod3, g_norm_mix[l][None, :], w_proj, g_q_a[l][None, :],
            _prep_w_q(w_q_up[l]), g_kv_a[l][None, :], wk, wvt, ctab, stab)
        o = _attention(q, k, vt)
        x = _mix(x, mod3, z, o, h, w_gate, w_dw[l], b_dw[l][None, :], g_conv_ln[l][None, :],
                 b_conv_ln[l][None, :], w_conv_out[l].astype(BF16), w_attn_o[l].astype(BF16),
                 w_out[l].astype(BF16))
        x = _ffn(x, mod3, g_norm_ffn[l][None, :], w_ffn_gate[l].astype(BF16),
                 w_ffn_up[l].astype(BF16), w_ffn_down[l].astype(BF16), g_final[None, :],
                 final_norm=(l == depth - 1))
    return x
```

```python
import functools
import math

import jax
import jax.numpy as jnp
from jax import lax
from jax.experimental import pallas as pl
from jax.experimental.pallas import tpu as pltpu

F32 = jnp.float32
BF16 = jnp.bfloat16

D_MODEL = 1024
N_HEADS = 8
Q_LORA = 256
KV_LORA = 128
QK_NOPE = 64
QK_ROPE = 32
HALF_ROPE = QK_ROPE // 2
V_HEAD = 64
QK_HEAD = QK_NOPE + QK_ROPE
ATTN_SCALE = 1.0 / math.sqrt(QK_HEAD)
Q_SCALE = ATTN_SCALE * math.log2(math.e)
ROPE_THETA = 10000.0
CONV_CH = 512
CONV_K = 31
N_MOD = 6
EPS_RMS = 1e-6
EPS_LN = 1e-5

LANES = 128
SUBLANES = 8
HEAD_PAD = LANES
VMEM_LIMIT = 56 * 1024 * 1024

C_QA = 0
C_KVA = C_QA + Q_LORA
C_KR = C_KVA + KV_LORA
C_CONV = C_KR + 2 * HEAD_PAD

TS_IN = 512
TQ = 512
KEY_CHUNK = 1024
TS_MIX = 512
HALO = 16
TM_FFN = 512


def _sigmoid(x):
    return 1.0 / (1.0 + jnp.exp(-x))


def _rms(x, g):
    return x * lax.rsqrt(jnp.mean(x * x, axis=-1, keepdims=True) + EPS_RMS) * g


def _dot(a, b):
    return jnp.dot(a, b, preferred_element_type=F32)


def _const_spec(shape):
    nd = len(shape)
    return pl.BlockSpec(shape, lambda *_: (0,) * nd, pipeline_mode=pl.Buffered(1))


def _rope_kernel(pos_ref, freq_ref, cos_ref, sin_ref):
    ang = pos_ref[...] * freq_ref[...]
    cos_ref[...] = jnp.cos(ang)
    sin_ref[...] = jnp.sin(ang)


def _rope_tables(positions):
    B, S = positions.shape
    inv_freq = ROPE_THETA ** (-jnp.arange(0, QK_ROPE, 2, dtype=F32) / QK_ROPE)
    rows = B * S * HALF_ROPE // LANES
    pos_rep = jnp.repeat(positions.reshape(-1).astype(F32), HALF_ROPE).reshape(rows, LANES)
    freq = jnp.tile(inv_freq, LANES // HALF_ROPE)[None, :]
    cos, sin = pl.pallas_call(
        _rope_kernel,
        out_shape=(jax.ShapeDtypeStruct((rows, LANES), F32),) * 2,
        name="rope_tables",
    )(pos_rep, freq)
    return cos, sin


def _expand_rope_table(compact_ref, out_ref, nope_value):
    pos_per_row = LANES // HALF_ROPE
    n = compact_ref.shape[0]
    comp = compact_ref[...]
    lane = lax.broadcasted_iota(jnp.int32, (n, LANES), 1)
    for j in range(pos_per_row):
        shift = (QK_NOPE - HALF_ROPE * j) % LANES
        first = pltpu.roll(comp, shift, 1) if shift else comp
        second = pltpu.roll(first, HALF_ROPE, 1)
        out_ref[pl.ds(j, n, stride=pos_per_row), :] = jnp.where(
            lane < QK_NOPE, nope_value,
            jnp.where(lane < QK_NOPE + HALF_ROPE, first,
                      jnp.where(lane < QK_HEAD, second, 0.0)))


def _mod_kernel(c_ref, w_ref, b_ref, o_ref):
    c = c_ref[...]
    c_act = (c * _sigmoid(c)).astype(BF16)
    o_ref[...] = _dot(c_act, w_ref[...].astype(BF16)) + b_ref[...]


def _modulation(c, w_ada, b_ada):
    B, D = c.shape
    n = w_ada.shape[1]
    bn = 1536
    return pl.pallas_call(
        _mod_kernel,
        out_shape=jax.ShapeDtypeStruct((B, n), F32),
        grid=(n // bn,),
        in_specs=[
            pl.BlockSpec((B, D), lambda j: (0, 0)),
            pl.BlockSpec((D, bn), lambda j: (0, j)),
            pl.BlockSpec((1, bn), lambda j: (0, j)),
        ],
        out_specs=pl.BlockSpec((B, bn), lambda j: (0, j)),
        compiler_params=pltpu.CompilerParams(
            dimension_semantics=("parallel",), vmem_limit_bytes=VMEM_LIMIT),
        name="adaln_modulation",
    )(c, w_ada, b_ada)


def _mixer_input(x_ref, mod_ref, gmix_ref):
    D = D_MODEL
    shift = mod_ref[:, 0:D]
    scale = mod_ref[:, D:2 * D]
    return (_rms(x_ref[...], gmix_ref[...]) * (1.0 + scale) + shift).astype(BF16)


def _inproj_kernel(x_ref, mod_ref, gmix_ref, win_ref, gq_ref, wq_ref, gkv_ref, wk_ref, wvt_ref,
                   cosc_ref, sinc_ref,
                   q_ref, k_ref, vt_ref, z_ref, h_ref, ctab_ref, stab_ref):
    h = _mixer_input(x_ref, mod_ref, gmix_ref)
    h_ref[...] = h
    _expand_rope_table(cosc_ref, ctab_ref, 1.0)
    _expand_rope_table(sinc_ref, stab_ref, 0.0)
    ctab = ctab_ref[...]
    stab = stab_ref[...]

    qa = _dot(h, win_ref[:, C_QA:C_QA + Q_LORA])
    qn = _rms(qa, gq_ref[...]).astype(BF16)
    nq = N_HEADS * HEAD_PAD
    q_main = _dot(qn, wq_ref[:, 0:nq])
    q_swap = _dot(qn, wq_ref[:, nq:2 * nq])
    for hd in range(N_HEADS):
        sl = slice(hd * HEAD_PAD, (hd + 1) * HEAD_PAD)
        q_ref[:, sl] = ((q_main[:, sl] * ctab + q_swap[:, sl] * stab) * Q_SCALE).astype(BF16)

    kva = _dot(h, win_ref[:, C_KVA:C_KVA + KV_LORA])
    kvn = _rms(kva, gkv_ref[...]).astype(BF16)
    k_pad = _dot(kvn, wk_ref[...])
    vt_ref[...] = lax.dot_general(wvt_ref[...], kvn, (((1,), (1,)), ((), ())),
                                  preferred_element_type=F32).astype(BF16)
    kr = _dot(h, win_ref[:, C_KR:C_KR + 2 * HEAD_PAD])
    kr_rot = kr[:, 0:HEAD_PAD] * ctab + kr[:, HEAD_PAD:2 * HEAD_PAD] * stab
    for hd in range(N_HEADS):
        sl = slice(hd * HEAD_PAD, (hd + 1) * HEAD_PAD)
        k_ref[:, sl] = (k_pad[:, sl] + kr_rot).astype(BF16)

    cu = _dot(h, win_ref[:, C_CONV:C_CONV + 2 * CONV_CH])
    z_ref[...] = cu[:, 0:CONV_CH] * _sigmoid(cu[:, CONV_CH:2 * CONV_CH])


def _inproj(x, mod3, g_mix, w_in_ext, g_q, wq2, g_kv, wk, wvt, cos_c, sin_c):
    B, S, D = x.shape
    ts = TS_IN
    nq = N_HEADS * HEAD_PAD
    nv = N_HEADS * V_HEAD
    tok = lambda w: pl.BlockSpec((None, ts, w), lambda b, i: (b, i, 0))
    rows = ts * HALF_ROPE // LANES
    n_i = S // ts
    compact = pl.BlockSpec((rows, LANES), lambda b, i: (b * n_i + i, 0))
    return pl.pallas_call(
        _inproj_kernel,
        out_shape=(
            jax.ShapeDtypeStruct((B, S, nq), BF16),
            jax.ShapeDtypeStruct((B, S, nq), BF16),
            jax.ShapeDtypeStruct((B, nv, S), BF16),
            jax.ShapeDtypeStruct((B, S, CONV_CH), F32),
            jax.ShapeDtypeStruct((B, S, D), BF16),
        ),
        grid=(B, S // ts),
        in_specs=[
            tok(D),
            pl.BlockSpec((None, 1, N_MOD * D), lambda b, i: (b, 0, 0)),
            _const_spec(g_mix.shape),
            _const_spec(w_in_ext.shape),
            _const_spec(g_q.shape),
            _const_spec(wq2.shape),
            _const_spec(g_kv.shape),
            _const_spec(wk.shape),
            _const_spec(wvt.shape),
            compact,
            compact,
        ],
        out_specs=(tok(nq), tok(nq), pl.BlockSpec((None, nv, ts), lambda b, i: (b, 0, i)),
                   tok(CONV_CH), tok(D)),
        scratch_shapes=[pltpu.VMEM((ts, HEAD_PAD), F32), pltpu.VMEM((ts, HEAD_PAD), F32)],
        compiler_params=pltpu.CompilerParams(
            dimension_semantics=("parallel", "parallel"), vmem_limit_bytes=VMEM_LIMIT),
        name="input_projection",
    )(x, mod3, g_mix, w_in_ext, g_q, wq2, g_kv, wk, wvt, cos_c, sin_c)


def _attn_kernel(q_ref, k_ref, vt_ref, o_ref):
    n_kc = k_ref.shape[0] // KEY_CHUNK

    def score_chunk(hd, c):
        sl = slice(hd * HEAD_PAD, (hd + 1) * HEAD_PAD)
        ks = slice(c * KEY_CHUNK, (c + 1) * KEY_CHUNK)
        return lax.dot_general(k_ref[ks, sl], q_ref[:, sl], (((1,), (1,)), ((), ())),
                               preferred_element_type=F32)

    def col_max(chunks):
        m = jnp.max(chunks[0], axis=0, keepdims=True)
        for st in chunks[1:]:
            m = jnp.maximum(m, jnp.max(st, axis=0, keepdims=True))
        return m

    st_next = [score_chunk(0, c) for c in range(n_kc)]
    outs = []
    for hd in range(N_HEADS):
        st_cur, st_next = st_next, []
        m = col_max(st_cur)
        l = acc = None
        for c in range(n_kc):
            if hd + 1 < N_HEADS:
                st_next.append(score_chunk(hd + 1, c))
            ks = slice(c * KEY_CHUNK, (c + 1) * KEY_CHUNK)
            p = jnp.exp2(st_cur[c] - m)
            ls = jnp.sum(p, axis=0, keepdims=True)
            pv = _dot(vt_ref[hd * V_HEAD:(hd + 1) * V_HEAD, ks], p.astype(BF16))
            l, acc = (ls, pv) if c == 0 else (l + ls, acc + pv)
        outs.append(acc / l)
        if hd % 2 == 1:
            o_pair = jnp.concatenate(outs, axis=0).T
            o_ref[:, (hd // 2) * 2 * V_HEAD:(hd // 2 + 1) * 2 * V_HEAD] = o_pair.astype(BF16)
            outs = []


def _attention(q, k, vt):
    B, S, nq = q.shape
    nv = N_HEADS * V_HEAD
    return pl.pallas_call(
        _attn_kernel,
        out_shape=jax.ShapeDtypeStruct((B, S, nv), BF16),
        grid=(B, S // TQ),
        in_specs=[
            pl.BlockSpec((None, TQ, nq), lambda b, i: (b, i, 0)),
            pl.BlockSpec((None, S, nq), lambda b, i: (b, 0, 0)),
            pl.BlockSpec((None, nv, S), lambda b, i: (b, 0, 0)),
        ],
        out_specs=pl.BlockSpec((None, TQ, nv), lambda b, i: (b, i, 0)),
        compiler_params=pltpu.CompilerParams(
            dimension_semantics=("parallel", "parallel"), vmem_limit_bytes=VMEM_LIMIT),
        name="mla_attention",
    )(q, k, vt)


def _mix_kernel(x_ref, mod_ref, z_ref, zprev_ref, znext_ref, o_ref, h_ref, wgate_ref,
                wdw_ref, bdw_ref, gln_ref, bln_ref, wco_ref, wao_ref, wout_ref,
                out_ref, zp_ref, zs_ref, conv_ref):
    D = D_MODEL
    ts = TS_MIX
    i = pl.program_id(1)
    n_i = pl.num_programs(1)

    zp_ref[0:HALO, :] = jnp.where(i > 0, zprev_ref[...], 0.0)
    zp_ref[HALO:HALO + ts, :] = z_ref[...]
    zp_ref[HALO + ts:2 * HALO + ts, :] = jnp.where(i < n_i - 1, znext_ref[...], 0.0)

    n_shift = zs_ref.shape[1]
    for s in range(1, SUBLANES):
        zs_ref[s - 1] = zp_ref[s:s + n_shift, :]

    gate_logits = _dot(h_ref[...], wgate_ref[...])
    y_a = _dot(o_ref[...], wao_ref[...])

    row_chunk = 64
    base = HALO - CONV_K // 2
    for cb in range(CONV_CH // LANES):
        cs = slice(cb * LANES, (cb + 1) * LANES)
        for rb in range(ts // row_chunk):
            r0 = rb * row_chunk
            acc = jnp.broadcast_to(bdw_ref[:, cs], (row_chunk, LANES))
            for kk in range(CONV_K):
                s = (base + kk) % SUBLANES
                a = r0 + base + kk - s
                src = zp_ref if s == 0 else zs_ref.at[s - 1]
                acc = acc + wdw_ref[kk:kk + 1, cs] * src[a:a + row_chunk, cs]
            conv_ref[r0:r0 + row_chunk, cs] = acc

    zc = conv_ref[...]
    mu = jnp.mean(zc, axis=-1, keepdims=True)
    zd = zc - mu
    var = jnp.mean(zd * zd, axis=-1, keepdims=True)
    zn = zd * lax.rsqrt(var + EPS_LN) * gln_ref[...] + bln_ref[...]
    zs = (zn * _sigmoid(zn)).astype(BF16)
    y_b = _dot(zs, wco_ref[...])
    gates = _sigmoid(gate_logits)
    merged = (gates[:, 0:D] * y_a + gates[:, D:2 * D] * y_b).astype(BF16)
    gate_m = mod_ref[:, 2 * D:3 * D]
    out_ref[...] = x_ref[...] + gate_m * _dot(merged, wout_ref[...])


def _mix(x, mod3, z, o, h, w_gate, w_dw, b_dw, g_ln, b_ln, w_co, w_ao, w_out):
    B, S, D = x.shape
    ts = TS_MIX
    hb = ts // HALO
    n_halo = S // HALO
    tok = lambda w: pl.BlockSpec((None, ts, w), lambda b, i: (b, i, 0))
    return pl.pallas_call(
        _mix_kernel,
        out_shape=jax.ShapeDtypeStruct((B, S, D), F32),
        grid=(B, S // ts),
        in_specs=[
            tok(D),
            pl.BlockSpec((None, 1, N_MOD * D), lambda b, i: (b, 0, 0)),
            tok(CONV_CH),
            pl.BlockSpec((None, HALO, CONV_CH), lambda b, i: (b, jnp.maximum(i * hb - 1, 0), 0)),
            pl.BlockSpec((None, HALO, CONV_CH),
                         lambda b, i: (b, jnp.minimum((i + 1) * hb, n_halo - 1), 0)),
            tok(N_HEADS * V_HEAD),
            tok(D),
            _const_spec(w_gate.shape),
            _const_spec(w_dw.shape),
            _const_spec(b_dw.shape),
            _const_spec(g_ln.shape),
            _const_spec(b_ln.shape),
            _const_spec(w_co.shape),
            _const_spec(w_ao.shape),
            _const_spec(w_out.shape),
        ],
        out_specs=tok(D),
        scratch_shapes=[
            pltpu.VMEM((ts + 2 * HALO, CONV_CH), F32),
            pltpu.VMEM((SUBLANES - 1, ts + 2 * HALO - SUBLANES, CONV_CH), F32),
            pltpu.VMEM((ts, CONV_CH), F32),
        ],
        compiler_params=pltpu.CompilerParams(
            dimension_semantics=("parallel", "parallel"), vmem_limit_bytes=VMEM_LIMIT),
        name="conv_merge_out",
    )(x, mod3, z, z, z, o, h, w_gate, w_dw, b_dw, g_ln, b_ln, w_co, w_ao, w_out)


def _ffn_kernel(x_ref, mod_ref, gffn_ref, wg_ref, wu_ref, wd_ref, gfin_ref, out_ref, *, final_norm):
    D = D_MODEL
    x = x_ref[...]
    shift = mod_ref[:, 3 * D:4 * D]
    scale = mod_ref[:, 4 * D:5 * D]
    gate = mod_ref[:, 5 * D:6 * D]
    h = (_rms(x, gffn_ref[...]) * (1.0 + scale) + shift).astype(BF16)
    g = _dot(h, wg_ref[...])
    u = _dot(h, wu_ref[...])
    a = (g * _sigmoid(g) * u).astype(BF16)
    x2 = x + gate * _dot(a, wd_ref[...])
    out_ref[...] = _rms(x2, gfin_ref[...]) if final_norm else x2


def _ffn(x, mod3, g_ffn, w_gate, w_up, w_down, g_final, final_norm):
    B, S, D = x.shape
    tm = TM_FFN
    tok = pl.BlockSpec((None, tm, D), lambda b, i: (b, i, 0))
    return pl.pallas_call(
        functools.partial(_ffn_kernel, final_norm=final_norm),
        out_shape=jax.ShapeDtypeStruct((B, S, D), F32),
        grid=(B, S // tm),
        in_specs=[
            tok,
            pl.BlockSpec((None, 1, N_MOD * D), lambda b, i: (b, 0, 0)),
            _const_spec(g_ffn.shape),
            _const_spec(w_gate.shape),
            _const_spec(w_up.shape),
            _const_spec(w_down.shape),
            _const_spec(g_final.shape),
        ],
        out_specs=tok,
        compiler_params=pltpu.CompilerParams(
            dimension_semantics=("parallel", "parallel"), vmem_limit_bytes=VMEM_LIMIT),
        name="swiglu_final_norm",
    )(x, mod3, g_ffn, w_gate, w_up, w_down, g_final)


def _prep_w_in(w_in):
    D = w_in.shape[0]
    o_kr = Q_LORA + KV_LORA
    o_conv = o_kr + QK_ROPE
    o_gate = o_conv + 2 * CONV_CH
    kr1 = w_in[:, o_kr:o_kr + HALF_ROPE]
    kr2 = w_in[:, o_kr + HALF_ROPE:o_kr + QK_ROPE]
    zn = jnp.zeros((D, QK_NOPE), w_in.dtype)
    zp = jnp.zeros((D, HEAD_PAD - QK_HEAD), w_in.dtype)
    w_proj = jnp.concatenate(
        [w_in[:, :o_kr], zn, kr1, kr2, zp, zn, -kr2, kr1, zp, w_in[:, o_conv:o_gate]],
        axis=1).astype(BF16)
    return w_proj, w_in[:, o_gate:].astype(BF16)


def _prep_w_q(w_q_up):
    r = w_q_up.shape[0]
    w = w_q_up.reshape(r, N_HEADS, QK_HEAD)
    nope = w[..., :QK_NOPE]
    r1 = w[..., QK_NOPE:QK_NOPE + HALF_ROPE]
    r2 = w[..., QK_NOPE + HALF_ROPE:]
    zp = jnp.zeros((r, N_HEADS, HEAD_PAD - QK_HEAD), w.dtype)
    main = jnp.concatenate([nope, r1, r2, zp], axis=-1).reshape(r, N_HEADS * HEAD_PAD)
    swap = jnp.concatenate([jnp.zeros_like(nope), -r2, r1, zp], axis=-1).reshape(r, N_HEADS * HEAD_PAD)
    return jnp.concatenate([main, swap], axis=1).astype(BF16)


def _prep_w_kv(w_kv_up):
    r = w_kv_up.shape[0]
    w = w_kv_up.reshape(r, N_HEADS, QK_NOPE + V_HEAD)
    k_nope = w[..., :QK_NOPE]
    v = w[..., QK_NOPE:]
    k_pad = jnp.concatenate(
        [k_nope, jnp.zeros((r, N_HEADS, HEAD_PAD - QK_NOPE), w.dtype)], axis=-1)
    wk = k_pad.reshape(r, N_HEADS * HEAD_PAD).astype(BF16)
    wvt = v.reshape(r, N_HEADS * V_HEAD).T.astype(BF16)
    return wk, wvt


def kernel(x, c, positions, w_ada, b_ada, g_norm_mix, w_in, g_q_a, w_q_up, g_kv_a, w_kv_up,
           w_attn_o, w_dw, b_dw, g_conv_ln, b_conv_ln, w_conv_out, w_out, g_norm_ffn,
           w_ffn_gate, w_ffn_up, w_ffn_down, g_final):
    B, S, D = x.shape
    depth = w_ada.shape[0]
    cos_c, sin_c = _rope_tables(positions)
    for l in range(depth):
        mod3 = _modulation(c, w_ada[l], b_ada[l][None, :]).reshape(B, 1, N_MOD * D)
        wk, wvt = _prep_w_kv(w_kv_up[l])
        w_proj, w_gate = _prep_w_in(w_in[l])
        q, k, vt, z, h = _inproj(
            x, mod3, g_norm_mix[l][None, :], w_proj, g_q_a[l][None, :],
            _prep_w_q(w_q_up[l]), g_kv_a[l][None, :], wk, wvt, cos_c, sin_c)
        o = _attention(q, k, vt)
        x = _mix(x, mod3, z, o, h, w_gate, w_dw[l], b_dw[l][None, :], g_conv_ln[l][None, :],
                 b_conv_ln[l][None, :], w_conv_out[l].astype(BF16), w_attn_o[l].astype(BF16),
                 w_out[l].astype(BF16))
        x = _ffn(x, mod3, g_norm_ffn[l][None, :], w_ffn_gate[l].astype(BF16),
                 w_ffn_up[l].astype(BF16), w_ffn_down[l].astype(BF16), g_final[None, :],
                 final_norm=(l == depth - 1))
    return x
```

```python
import functools
import math

import jax
import jax.numpy as jnp
from jax import lax
from jax.experimental import pallas as pl
from jax.experimental.pallas import tpu as pltpu

F32 = jnp.float32
BF16 = jnp.bfloat16

D_MODEL = 1024
N_HEADS = 8
Q_LORA = 256
KV_LORA = 128
QK_NOPE = 64
QK_ROPE = 32
HALF_ROPE = QK_ROPE // 2
V_HEAD = 64
QK_HEAD = QK_NOPE + QK_ROPE
ATTN_SCALE = 1.0 / math.sqrt(QK_HEAD)
Q_SCALE = ATTN_SCALE * math.log2(math.e)
ROPE_THETA = 10000.0
CONV_CH = 512
CONV_K = 31
N_MOD = 6
EPS_RMS = 1e-6
EPS_LN = 1e-5

LANES = 128
SUBLANES = 8
HEAD_PAD = LANES
VMEM_LIMIT = 56 * 1024 * 1024

C_QA = 0
C_KVA = C_QA + Q_LORA
C_KR = C_KVA + KV_LORA
C_CONV = C_KR + HEAD_PAD

TS_IN = 512
TQ = 512
KEY_CHUNK = 1024
TS_MIX = 512
HALO = 16
TM_FFN = 512


def _sigmoid(x):
    return 1.0 / (1.0 + jnp.exp(-x))


def _rms(x, g):
    return x * lax.rsqrt(jnp.mean(x * x, axis=-1, keepdims=True) + EPS_RMS) * g


def _dot(a, b):
    return jnp.dot(a, b, preferred_element_type=F32)


def _const_spec(shape):
    nd = len(shape)
    return pl.BlockSpec(shape, lambda *_: (0,) * nd, pipeline_mode=pl.Buffered(1))


def _rope_kernel(pos_ref, freq_ref, cos_ref, sin_ref):
    ang = pos_ref[...] * freq_ref[...]
    cos_ref[...] = jnp.cos(ang)
    sin_ref[...] = jnp.sin(ang)


def _rope_tables(positions):
    B, S = positions.shape
    inv_freq = ROPE_THETA ** (-jnp.arange(0, QK_ROPE, 2, dtype=F32) / QK_ROPE)
    rows = B * S * HALF_ROPE // LANES
    pos_rep = jnp.repeat(positions.reshape(-1).astype(F32), HALF_ROPE).reshape(rows, LANES)
    freq = jnp.tile(inv_freq, LANES // HALF_ROPE)[None, :]
    cos, sin = pl.pallas_call(
        _rope_kernel,
        out_shape=(jax.ShapeDtypeStruct((rows, LANES), F32),) * 2,
        name="rope_tables",
    )(pos_rep, freq)
    return cos, sin


def _expand_rope_table(compact_ref, out_ref, nope_value):
    pos_per_row = LANES // HALF_ROPE
    n = compact_ref.shape[0]
    comp = compact_ref[...]
    lane = lax.broadcasted_iota(jnp.int32, (n, LANES), 1)
    for j in range(pos_per_row):
        shift = (QK_NOPE - HALF_ROPE * j) % LANES
        first = pltpu.roll(comp, shift, 1) if shift else comp
        second = pltpu.roll(first, HALF_ROPE, 1)
        out_ref[pl.ds(j, n, stride=pos_per_row), :] = jnp.where(
            lane < QK_NOPE, nope_value,
            jnp.where(lane < QK_NOPE + HALF_ROPE, first,
                      jnp.where(lane < QK_HEAD, second, 0.0)))


def _rope_tile(t, ctab, stab):
    swapped = pltpu.roll(t, HEAD_PAD - QK_ROPE, 1)
    return t * ctab + swapped * stab


def _mod_kernel(c_ref, w_ref, b_ref, o_ref):
    c = c_ref[...]
    c_act = (c * _sigmoid(c)).astype(BF16)
    o_ref[...] = _dot(c_act, w_ref[...].astype(BF16)) + b_ref[...]


def _modulation(c, w_ada, b_ada):
    B, D = c.shape
    n = w_ada.shape[1]
    bn = 1536
    return pl.pallas_call(
        _mod_kernel,
        out_shape=jax.ShapeDtypeStruct((B, n), F32),
        grid=(n // bn,),
        in_specs=[
            pl.BlockSpec((B, D), lambda j: (0, 0)),
            pl.BlockSpec((D, bn), lambda j: (0, j)),
            pl.BlockSpec((1, bn), lambda j: (0, j)),
        ],
        out_specs=pl.BlockSpec((B, bn), lambda j: (0, j)),
        compiler_params=pltpu.CompilerParams(
            dimension_semantics=("parallel",), vmem_limit_bytes=VMEM_LIMIT),
        name="adaln_modulation",
    )(c, w_ada, b_ada)


def _mixer_input(x_ref, mod_ref, gmix_ref):
    D = D_MODEL
    shift = mod_ref[:, 0:D]
    scale = mod_ref[:, D:2 * D]
    return (_rms(x_ref[...], gmix_ref[...]) * (1.0 + scale) + shift).astype(BF16)


def _inproj_kernel(x_ref, mod_ref, gmix_ref, win_ref, gq_ref, wq_ref, gkv_ref, wk_ref, wvt_ref,
                   cosc_ref, sinc_ref,
                   q_ref, k_ref, vt_ref, z_ref, h_ref, ctab_ref, stab_ref):
    h = _mixer_input(x_ref, mod_ref, gmix_ref)
    h_ref[...] = h
    _expand_rope_table(cosc_ref, ctab_ref, 1.0)
    _expand_rope_table(sinc_ref, stab_ref, 0.0)
    ctab = ctab_ref[...]
    stab = stab_ref[...]

    qa = _dot(h, win_ref[:, C_QA:C_QA + Q_LORA])
    qn = _rms(qa, gq_ref[...]).astype(BF16)
    q_all = _dot(qn, wq_ref[...])
    for hd in range(N_HEADS):
        sl = slice(hd * HEAD_PAD, (hd + 1) * HEAD_PAD)
        q_ref[:, sl] = (_rope_tile(q_all[:, sl], ctab, stab) * Q_SCALE).astype(BF16)

    kva = _dot(h, win_ref[:, C_KVA:C_KVA + KV_LORA])
    kvn = _rms(kva, gkv_ref[...]).astype(BF16)
    k_pad = _dot(kvn, wk_ref[...])
    vt_ref[...] = lax.dot_general(wvt_ref[...], kvn, (((1,), (1,)), ((), ())),
                                  preferred_element_type=F32).astype(BF16)
    kr_rot = _rope_tile(_dot(h, win_ref[:, C_KR:C_KR + HEAD_PAD]), ctab, stab)
    for hd in range(N_HEADS):
        sl = slice(hd * HEAD_PAD, (hd + 1) * HEAD_PAD)
        k_ref[:, sl] = (k_pad[:, sl] + kr_rot).astype(BF16)

    cu = _dot(h, win_ref[:, C_CONV:C_CONV + 2 * CONV_CH])
    z_ref[...] = cu[:, 0:CONV_CH] * _sigmoid(cu[:, CONV_CH:2 * CONV_CH])


def _inproj(x, mod3, g_mix, w_in_ext, g_q, wq2, g_kv, wk, wvt, cos_c, sin_c):
    B, S, D = x.shape
    ts = TS_IN
    nq = N_HEADS * HEAD_PAD
    nv = N_HEADS * V_HEAD
    tok = lambda w: pl.BlockSpec((None, ts, w), lambda b, i: (b, i, 0))
    rows = ts * HALF_ROPE // LANES
    n_i = S // ts
    compact = pl.BlockSpec((rows, LANES), lambda b, i: (b * n_i + i, 0))
    return pl.pallas_call(
        _inproj_kernel,
        out_shape=(
            jax.ShapeDtypeStruct((B, S, nq), BF16),
            jax.ShapeDtypeStruct((B, S, nq), BF16),
            jax.ShapeDtypeStruct((B, nv, S), BF16),
            jax.ShapeDtypeStruct((B, S, CONV_CH), F32),
            jax.ShapeDtypeStruct((B, S, D), BF16),
        ),
        grid=(B, S // ts),
        in_specs=[
            tok(D),
            pl.BlockSpec((None, 1, N_MOD * D), lambda b, i: (b, 0, 0)),
            _const_spec(g_mix.shape),
            _const_spec(w_in_ext.shape),
            _const_spec(g_q.shape),
            _const_spec(wq2.shape),
            _const_spec(g_kv.shape),
            _const_spec(wk.shape),
            _const_spec(wvt.shape),
            compact,
            compact,
        ],
        out_specs=(tok(nq), tok(nq), pl.BlockSpec((None, nv, ts), lambda b, i: (b, 0, i)),
                   tok(CONV_CH), tok(D)),
        scratch_shapes=[pltpu.VMEM((ts, HEAD_PAD), F32), pltpu.VMEM((ts, HEAD_PAD), F32)],
        compiler_params=pltpu.CompilerParams(
            dimension_semantics=("parallel", "parallel"), vmem_limit_bytes=VMEM_LIMIT),
        name="input_projection",
    )(x, mod3, g_mix, w_in_ext, g_q, wq2, g_kv, wk, wvt, cos_c, sin_c)


def _attn_kernel(q_ref, k_ref, vt_ref, o_ref):
    n_kc = k_ref.shape[0] // KEY_CHUNK

    def score_chunk(hd, c):
        sl = slice(hd * HEAD_PAD, (hd + 1) * HEAD_PAD)
        ks = slice(c * KEY_CHUNK, (c + 1) * KEY_CHUNK)
        return lax.dot_general(k_ref[ks, sl], q_ref[:, sl], (((1,), (1,)), ((), ())),
                               preferred_element_type=F32)

    def col_max(chunks):
        m = jnp.max(chunks[0], axis=0, keepdims=True)
        for st in chunks[1:]:
            m = jnp.maximum(m, jnp.max(st, axis=0, keepdims=True))
        return m

    st_next = [score_chunk(0, c) for c in range(n_kc)]
    outs = []
    for hd in range(N_HEADS):
        st_cur, st_next = st_next, []
        m = col_max(st_cur)
        l = acc = None
        for c in range(n_kc):
            if hd + 1 < N_HEADS:
                st_next.append(score_chunk(hd + 1, c))
            ks = slice(c * KEY_CHUNK, (c + 1) * KEY_CHUNK)
            p = jnp.exp2(st_cur[c] - m)
            ls = jnp.sum(p, axis=0, keepdims=True)
            pv = _dot(vt_ref[hd * V_HEAD:(hd + 1) * V_HEAD, ks], p.astype(BF16))
            l, acc = (ls, pv) if c == 0 else (l + ls, acc + pv)
        outs.append(acc / l)
        if hd % 2 == 1:
            o_pair = jnp.concatenate(outs, axis=0).T
            o_ref[:, (hd // 2) * 2 * V_HEAD:(hd // 2 + 1) * 2 * V_HEAD] = o_pair.astype(BF16)
            outs = []


def _attention(q, k, vt):
    B, S, nq = q.shape
    nv = N_HEADS * V_HEAD
    return pl.pallas_call(
        _attn_kernel,
        out_shape=jax.ShapeDtypeStruct((B, S, nv), BF16),
        grid=(B, S // TQ),
        in_specs=[
            pl.BlockSpec((None, TQ, nq), lambda b, i: (b, i, 0)),
            pl.BlockSpec((None, S, nq), lambda b, i: (b, 0, 0)),
            pl.BlockSpec((None, nv, S), lambda b, i: (b, 0, 0)),
        ],
        out_specs=pl.BlockSpec((None, TQ, nv), lambda b, i: (b, i, 0)),
        compiler_params=pltpu.CompilerParams(
            dimension_semantics=("parallel", "parallel"), vmem_limit_bytes=VMEM_LIMIT),
        name="mla_attention",
    )(q, k, vt)


def _mix_kernel(x_ref, mod_ref, z_ref, zprev_ref, znext_ref, o_ref, h_ref, wgate_ref,
                wdw_ref, bdw_ref, gln_ref, bln_ref, wco_ref, wao_ref, wout_ref,
                out_ref, zp_ref, zs_ref, conv_ref):
    D = D_MODEL
    ts = TS_MIX
    i = pl.program_id(1)
    n_i = pl.num_programs(1)

    zp_ref[0:HALO, :] = jnp.where(i > 0, zprev_ref[...], 0.0)
    zp_ref[HALO:HALO + ts, :] = z_ref[...]
    zp_ref[HALO + ts:2 * HALO + ts, :] = jnp.where(i < n_i - 1, znext_ref[...], 0.0)

    n_shift = zs_ref.shape[1]
    for s in range(1, SUBLANES):
        zs_ref[s - 1] = zp_ref[s:s + n_shift, :]

    gate_logits = _dot(h_ref[...], wgate_ref[...])
    y_a = _dot(o_ref[...], wao_ref[...])

    row_chunk = 64
    base = HALO - CONV_K // 2
    for cb in range(CONV_CH // LANES):
        cs = slice(cb * LANES, (cb + 1) * LANES)
        for rb in range(ts // row_chunk):
            r0 = rb * row_chunk
            acc = jnp.broadcast_to(bdw_ref[:, cs], (row_chunk, LANES))
            for kk in range(CONV_K):
                s = (base + kk) % SUBLANES
                a = r0 + base + kk - s
                src = zp_ref if s == 0 else zs_ref.at[s - 1]
                acc = acc + wdw_ref[kk:kk + 1, cs] * src[a:a + row_chunk, cs]
            conv_ref[r0:r0 + row_chunk, cs] = acc

    zc = conv_ref[...]
    mu = jnp.mean(zc, axis=-1, keepdims=True)
    zd = zc - mu
    var = jnp.mean(zd * zd, axis=-1, keepdims=True)
    zn = zd * lax.rsqrt(var + EPS_LN) * gln_ref[...] + bln_ref[...]
    zs = (zn * _sigmoid(zn)).astype(BF16)
    y_b = _dot(zs, wco_ref[...])
    gates = _sigmoid(gate_logits)
    merged = (gates[:, 0:D] * y_a + gates[:, D:2 * D] * y_b).astype(BF16)
    gate_m = mod_ref[:, 2 * D:3 * D]
    out_ref[...] = x_ref[...] + gate_m * _dot(merged, wout_ref[...])


def _mix(x, mod3, z, o, h, w_gate, w_dw, b_dw, g_ln, b_ln, w_co, w_ao, w_out):
    B, S, D = x.shape
    ts = TS_MIX
    hb = ts // HALO
    n_halo = S // HALO
    tok = lambda w: pl.BlockSpec((None, ts, w), lambda b, i: (b, i, 0))
    return pl.pallas_call(
        _mix_kernel,
        out_shape=jax.ShapeDtypeStruct((B, S, D), F32),
        grid=(B, S // ts),
        in_specs=[
            tok(D),
            pl.BlockSpec((None, 1, N_MOD * D), lambda b, i: (b, 0, 0)),
            tok(CONV_CH),
            pl.BlockSpec((None, HALO, CONV_CH), lambda b, i: (b, jnp.maximum(i * hb - 1, 0), 0)),
            pl.BlockSpec((None, HALO, CONV_CH),
                         lambda b, i: (b, jnp.minimum((i + 1) * hb, n_halo - 1), 0)),
            tok(N_HEADS * V_HEAD),
            tok(D),
            _const_spec(w_gate.shape),
            _const_spec(w_dw.shape),
            _const_spec(b_dw.shape),
            _const_spec(g_ln.shape),
            _const_spec(b_ln.shape),
            _const_spec(w_co.shape),
            _const_spec(w_ao.shape),
            _const_spec(w_out.shape),
        ],
        out_specs=tok(D),
        scratch_shapes=[
            pltpu.VMEM((ts + 2 * HALO, CONV_CH), F32),
            pltpu.VMEM((SUBLANES - 1, ts + 2 * HALO - SUBLANES, CONV_CH), F32),
            pltpu.VMEM((ts, CONV_CH), F32),
        ],
        compiler_params=pltpu.CompilerParams(
            dimension_semantics=("parallel", "parallel"), vmem_limit_bytes=VMEM_LIMIT),
        name="conv_merge_out",
    )(x, mod3, z, z, z, o, h, w_gate, w_dw, b_dw, g_ln, b_ln, w_co, w_ao, w_out)


def _ffn_kernel(x_ref, mod_ref, gffn_ref, wg_ref, wu_ref, wd_ref, gfin_ref, out_ref, *, final_norm):
    D = D_MODEL
    x = x_ref[...]
    shift = mod_ref[:, 3 * D:4 * D]
    scale = mod_ref[:, 4 * D:5 * D]
    gate = mod_ref[:, 5 * D:6 * D]
    h = (_rms(x, gffn_ref[...]) * (1.0 + scale) + shift).astype(BF16)
    g = _dot(h, wg_ref[...])
    u = _dot(h, wu_ref[...])
    a = (g * _sigmoid(g) * u).astype(BF16)
    x2 = x + gate * _dot(a, wd_ref[...])
    out_ref[...] = _rms(x2, gfin_ref[...]) if final_norm else x2


def _ffn(x, mod3, g_ffn, w_gate, w_up, w_down, g_final, final_norm):
    B, S, D = x.shape
    tm = TM_FFN
    tok = pl.BlockSpec((None, tm, D), lambda b, i: (b, i, 0))
    return pl.pallas_call(
        functools.partial(_ffn_kernel, final_norm=final_norm),
        out_shape=jax.ShapeDtypeStruct((B, S, D), F32),
        grid=(B, S // tm),
        in_specs=[
            tok,
            pl.BlockSpec((None, 1, N_MOD * D), lambda b, i: (b, 0, 0)),
            _const_spec(g_ffn.shape),
            _const_spec(w_gate.shape),
            _const_spec(w_up.shape),
            _const_spec(w_down.shape),
            _const_spec(g_final.shape),
        ],
        out_specs=tok,
        compiler_params=pltpu.CompilerParams(
            dimension_semantics=("parallel", "parallel"), vmem_limit_bytes=VMEM_LIMIT),
        name="swiglu_final_norm",
    )(x, mod3, g_ffn, w_gate, w_up, w_down, g_final)


def _prep_w_in(w_in):
    D = w_in.shape[0]
    o_kr = Q_LORA + KV_LORA
    o_conv = o_kr + QK_ROPE
    o_gate = o_conv + 2 * CONV_CH
    kr1 = w_in[:, o_kr:o_kr + HALF_ROPE]
    kr2 = w_in[:, o_kr + HALF_ROPE:o_kr + QK_ROPE]
    zn = jnp.zeros((D, QK_NOPE), w_in.dtype)
    w_proj = jnp.concatenate(
        [w_in[:, :o_kr], zn, kr1, kr2, -kr2, kr1, w_in[:, o_conv:o_gate]], axis=1).astype(BF16)
    return w_proj, w_in[:, o_gate:].astype(BF16)


def _prep_w_q(w_q_up):
    r = w_q_up.shape[0]
    w = w_q_up.reshape(r, N_HEADS, QK_HEAD)
    nope = w[..., :QK_NOPE]
    r1 = w[..., QK_NOPE:QK_NOPE + HALF_ROPE]
    r2 = w[..., QK_NOPE + HALF_ROPE:]
    return jnp.concatenate([nope, r1, r2, -r2, r1], axis=-1).reshape(r, N_HEADS * HEAD_PAD).astype(BF16)


def _prep_w_kv(w_kv_up):
    r = w_kv_up.shape[0]
    w = w_kv_up.reshape(r, N_HEADS, QK_NOPE + V_HEAD)
    k_nope = w[..., :QK_NOPE]
    v = w[..., QK_NOPE:]
    k_pad = jnp.concatenate(
        [k_nope, jnp.zeros((r, N_HEADS, HEAD_PAD - QK_NOPE), w.dtype)], axis=-1)
    wk = k_pad.reshape(r, N_HEADS * HEAD_PAD).astype(BF16)
    wvt = v.reshape(r, N_HEADS * V_HEAD).T.astype(BF16)
    return wk, wvt


def kernel(x, c, positions, w_ada, b_ada, g_norm_mix, w_in, g_q_a, w_q_up, g_kv_a, w_kv_up,
           w_attn_o, w_dw, b_dw, g_conv_ln, b_conv_ln, w_conv_out, w_out, g_norm_ffn,
           w_ffn_gate, w_ffn_up, w_ffn_down, g_final):
    B, S, D = x.shape
    depth = w_ada.shape[0]
    cos_c, sin_c = _rope_tables(positions)
    for l in range(depth):
        mod3 = _modulation(c, w_ada[l], b_ada[l][None, :]).reshape(B, 1, N_MOD * D)
        wk, wvt = _prep_w_kv(w_kv_up[l])
        w_proj, w_gate = _prep_w_in(w_in[l])
        q, k, vt, z, h = _inproj(
            x, mod3, g_norm_mix[l][None, :], w_proj, g_q_a[l][None, :],
            _prep_w_q(w_q_up[l]), g_kv_a[l][None, :], wk, wvt, cos_c, sin_c)
        o = _attention(q, k, vt)
        x = _mix(x, mod3, z, o, h, w_gate, w_dw[l], b_dw[l][None, :], g_conv_ln[l][None, :],
                 b_conv_ln[l][None, :], w_conv_out[l].astype(BF16), w_attn_o[l].astype(BF16),
                 w_out[l].astype(BF16))
        x = _ffn(x, mod3, g_norm_ffn[l][None, :], w_ffn_gate[l].astype(BF16),
                 w_ffn_up[l].astype(BF16), w_ffn_down[l].astype(BF16), g_final[None, :],
                 final_norm=(l == depth - 1))
    return x
```

```python
import functools
import math

import jax
import jax.numpy as jnp
from jax import lax
from jax.experimental import pallas as pl
from jax.experimental.pallas import tpu as pltpu

F32 = jnp.float32
BF16 = jnp.bfloat16

D_MODEL = 1024
N_HEADS = 8
Q_LORA = 256
KV_LORA = 128
QK_NOPE = 64
QK_ROPE = 32
HALF_ROPE = QK_ROPE // 2
V_HEAD = 64
QK_HEAD = QK_NOPE + QK_ROPE
ATTN_SCALE = 1.0 / math.sqrt(QK_HEAD)
Q_SCALE = ATTN_SCALE * math.log2(math.e)
ROPE_THETA = 10000.0
CONV_CH = 512
CONV_K = 31
N_MOD = 6
EPS_RMS = 1e-6
EPS_LN = 1e-5

LANES = 128
SUBLANES = 8
HEAD_PAD = LANES
VMEM_LIMIT = 56 * 1024 * 1024

C_QA = 0
C_KVA = C_QA + Q_LORA
C_KR = C_KVA + KV_LORA
C_CONV = C_KR + HEAD_PAD

TS_IN = 512
TQ = 512
KEY_CHUNK = 1024
TS_MIX = 512
HALO = 16
TM_FFN = 512


def _sigmoid(x):
    return 1.0 / (1.0 + jnp.exp(-x))


def _rms(x, g):
    return x * lax.rsqrt(jnp.mean(x * x, axis=-1, keepdims=True) + EPS_RMS) * g


def _dot(a, b):
    return jnp.dot(a, b, preferred_element_type=F32)


def _const_spec(shape):
    nd = len(shape)
    return pl.BlockSpec(shape, lambda *_: (0,) * nd, pipeline_mode=pl.Buffered(1))


def _rope_kernel(pos_ref, freq_ref, cos_ref, sin_ref):
    ang = pos_ref[...] * freq_ref[...]
    cos_ref[...] = jnp.cos(ang)
    sin_ref[...] = jnp.sin(ang)


def _rope_tables(positions):
    B, S = positions.shape
    inv_freq = ROPE_THETA ** (-jnp.arange(0, QK_ROPE, 2, dtype=F32) / QK_ROPE)
    rows = B * S * HALF_ROPE // LANES
    pos_rep = jnp.repeat(positions.reshape(-1).astype(F32), HALF_ROPE).reshape(rows, LANES)
    freq = jnp.tile(inv_freq, LANES // HALF_ROPE)[None, :]
    cos, sin = pl.pallas_call(
        _rope_kernel,
        out_shape=(jax.ShapeDtypeStruct((rows, LANES), F32),) * 2,
        name="rope_tables",
    )(pos_rep, freq)
    return cos, sin


def _expand_rope_table(compact_ref, out_ref, nope_value):
    pos_per_row = LANES // HALF_ROPE
    n = compact_ref.shape[0]
    comp = compact_ref[...]
    lane = lax.broadcasted_iota(jnp.int32, (n, LANES), 1)
    for j in range(pos_per_row):
        shift = (QK_NOPE - HALF_ROPE * j) % LANES
        first = pltpu.roll(comp, shift, 1) if shift else comp
        second = pltpu.roll(first, HALF_ROPE, 1)
        out_ref[pl.ds(j, n, stride=pos_per_row), :] = jnp.where(
            lane < QK_NOPE, nope_value,
            jnp.where(lane < QK_NOPE + HALF_ROPE, first,
                      jnp.where(lane < QK_HEAD, second, 0.0)))


def _rope_tile(t, ctab, stab):
    swapped = pltpu.roll(t, HEAD_PAD - QK_ROPE, 1)
    return t * ctab + swapped * stab


def _mod_kernel(c_ref, w_ref, b_ref, o_ref):
    c = c_ref[...]
    c_act = (c * _sigmoid(c)).astype(BF16)
    o_ref[...] = _dot(c_act, w_ref[...].astype(BF16)) + b_ref[...]


def _modulation(c, w_ada, b_ada):
    B, D = c.shape
    n = w_ada.shape[1]
    bn = 1536
    return pl.pallas_call(
        _mod_kernel,
        out_shape=jax.ShapeDtypeStruct((B, n), F32),
        grid=(n // bn,),
        in_specs=[
            pl.BlockSpec((B, D), lambda j: (0, 0)),
            pl.BlockSpec((D, bn), lambda j: (0, j)),
            pl.BlockSpec((1, bn), lambda j: (0, j)),
        ],
        out_specs=pl.BlockSpec((B, bn), lambda j: (0, j)),
        compiler_params=pltpu.CompilerParams(
            dimension_semantics=("parallel",), vmem_limit_bytes=VMEM_LIMIT),
        name="adaln_modulation",
    )(c, w_ada, b_ada)


def _mixer_input(x_ref, mod_ref, gmix_ref):
    D = D_MODEL
    shift = mod_ref[:, 0:D]
    scale = mod_ref[:, D:2 * D]
    return (_rms(x_ref[...], gmix_ref[...]) * (1.0 + scale) + shift).astype(BF16)


W_IN_ROWS = 128


def _first_grid_step():
    return jnp.logical_and(pl.program_id(0) == 0, pl.program_id(1) == 0)


def _relayout_w_proj(w_ref, out_ref):
    o_kr = Q_LORA + KV_LORA
    o_conv = o_kr + QK_ROPE
    sup_end = o_kr + HEAD_PAD + 2 * CONV_CH
    lane = lax.broadcasted_iota(jnp.int32, (W_IN_ROWS, LANES), 1)

    def body(r, carry):
        rs = pl.ds(pl.multiple_of(r * W_IN_ROWS, W_IN_ROWS), W_IN_ROWS)
        out_ref[rs, 0:o_kr] = w_ref[rs, 0:o_kr].astype(BF16)
        sup = w_ref[rs, o_kr:sup_end]
        t = sup[:, 0:LANES]
        tile = jnp.where(
            lane < QK_NOPE, 0.0,
            jnp.where(lane < QK_HEAD, pltpu.roll(t, QK_NOPE, 1),
                      jnp.where(lane < QK_HEAD + HALF_ROPE, -pltpu.roll(t, QK_NOPE + HALF_ROPE, 1),
                                pltpu.roll(t, QK_HEAD + HALF_ROPE, 1))))
        out_ref[rs, C_KR:C_KR + HEAD_PAD] = tile.astype(BF16)
        out_ref[rs, C_CONV:C_CONV + 2 * CONV_CH] = sup[:, QK_ROPE:QK_ROPE + 2 * CONV_CH].astype(BF16)
        return carry

    lax.fori_loop(0, w_ref.shape[0] // W_IN_ROWS, body, 0)


def _inproj_kernel(x_ref, mod_ref, gmix_ref, w_ref, gq_ref, wq_ref, gkv_ref, wk_ref, wvt_ref,
                   cosc_ref, sinc_ref,
                   q_ref, k_ref, vt_ref, z_ref, h_ref, ctab_ref, stab_ref, win_ref):
    pl.when(_first_grid_step())(lambda: _relayout_w_proj(w_ref, win_ref))
    h = _mixer_input(x_ref, mod_ref, gmix_ref)
    h_ref[...] = h
    _expand_rope_table(cosc_ref, ctab_ref, 1.0)
    _expand_rope_table(sinc_ref, stab_ref, 0.0)
    ctab = ctab_ref[...]
    stab = stab_ref[...]

    qa = _dot(h, win_ref[:, C_QA:C_QA + Q_LORA])
    qn = _rms(qa, gq_ref[...]).astype(BF16)
    q_all = _dot(qn, wq_ref[...])
    for hd in range(N_HEADS):
        sl = slice(hd * HEAD_PAD, (hd + 1) * HEAD_PAD)
        q_ref[:, sl] = (_rope_tile(q_all[:, sl], ctab, stab) * Q_SCALE).astype(BF16)

    kva = _dot(h, win_ref[:, C_KVA:C_KVA + KV_LORA])
    kvn = _rms(kva, gkv_ref[...]).astype(BF16)
    k_pad = _dot(kvn, wk_ref[...])
    vt_ref[...] = lax.dot_general(wvt_ref[...], kvn, (((1,), (1,)), ((), ())),
                                  preferred_element_type=F32).astype(BF16)
    kr_rot = _rope_tile(_dot(h, win_ref[:, C_KR:C_KR + HEAD_PAD]), ctab, stab)
    for hd in range(N_HEADS):
        sl = slice(hd * HEAD_PAD, (hd + 1) * HEAD_PAD)
        k_ref[:, sl] = (k_pad[:, sl] + kr_rot).astype(BF16)

    cu = _dot(h, win_ref[:, C_CONV:C_CONV + 2 * CONV_CH])
    z_ref[...] = cu[:, 0:CONV_CH] * _sigmoid(cu[:, CONV_CH:2 * CONV_CH])


def _inproj(x, mod3, g_mix, w_in, g_q, wq2, g_kv, wk, wvt, cos_c, sin_c):
    B, S, D = x.shape
    ts = TS_IN
    nq = N_HEADS * HEAD_PAD
    nv = N_HEADS * V_HEAD
    tok = lambda w: pl.BlockSpec((None, ts, w), lambda b, i: (b, i, 0))
    rows = ts * HALF_ROPE // LANES
    n_i = S // ts
    compact = pl.BlockSpec((rows, LANES), lambda b, i: (b * n_i + i, 0))
    return pl.pallas_call(
        _inproj_kernel,
        out_shape=(
            jax.ShapeDtypeStruct((B, S, nq), BF16),
            jax.ShapeDtypeStruct((B, S, nq), BF16),
            jax.ShapeDtypeStruct((B, nv, S), BF16),
            jax.ShapeDtypeStruct((B, S, CONV_CH), F32),
            jax.ShapeDtypeStruct((B, S, D), BF16),
        ),
        grid=(B, S // ts),
        in_specs=[
            tok(D),
            pl.BlockSpec((None, 1, N_MOD * D), lambda b, i: (b, 0, 0)),
            _const_spec(g_mix.shape),
            _const_spec(w_in.shape),
            _const_spec(g_q.shape),
            _const_spec(wq2.shape),
            _const_spec(g_kv.shape),
            _const_spec(wk.shape),
            _const_spec(wvt.shape),
            compact,
            compact,
        ],
        out_specs=(tok(nq), tok(nq), pl.BlockSpec((None, nv, ts), lambda b, i: (b, 0, i)),
                   tok(CONV_CH), tok(D)),
        scratch_shapes=[pltpu.VMEM((ts, HEAD_PAD), F32), pltpu.VMEM((ts, HEAD_PAD), F32),
                        pltpu.VMEM((D, C_CONV + 2 * CONV_CH), BF16)],
        compiler_params=pltpu.CompilerParams(
            dimension_semantics=("arbitrary", "arbitrary"), vmem_limit_bytes=VMEM_LIMIT),
        name="input_projection",
    )(x, mod3, g_mix, w_in, g_q, wq2, g_kv, wk, wvt, cos_c, sin_c)


def _attn_kernel(q_ref, k_ref, vt_ref, o_ref):
    n_kc = k_ref.shape[0] // KEY_CHUNK

    def score_chunk(hd, c):
        sl = slice(hd * HEAD_PAD, (hd + 1) * HEAD_PAD)
        ks = slice(c * KEY_CHUNK, (c + 1) * KEY_CHUNK)
        return lax.dot_general(k_ref[ks, sl], q_ref[:, sl], (((1,), (1,)), ((), ())),
                               preferred_element_type=F32)

    def col_max(chunks):
        m = jnp.max(chunks[0], axis=0, keepdims=True)
        for st in chunks[1:]:
            m = jnp.maximum(m, jnp.max(st, axis=0, keepdims=True))
        return m

    st_next = [score_chunk(0, c) for c in range(n_kc)]
    outs = []
    for hd in range(N_HEADS):
        st_cur, st_next = st_next, []
        m = col_max(st_cur)
        l = acc = None
        for c in range(n_kc):
            if hd + 1 < N_HEADS:
                st_next.append(score_chunk(hd + 1, c))
            ks = slice(c * KEY_CHUNK, (c + 1) * KEY_CHUNK)
            p = jnp.exp2(st_cur[c] - m)
            ls = jnp.sum(p, axis=0, keepdims=True)
            pv = _dot(vt_ref[hd * V_HEAD:(hd + 1) * V_HEAD, ks], p.astype(BF16))
            l, acc = (ls, pv) if c == 0 else (l + ls, acc + pv)
        outs.append(acc / l)
        if hd % 2 == 1:
            o_pair = jnp.concatenate(outs, axis=0).T
            o_ref[:, (hd // 2) * 2 * V_HEAD:(hd // 2 + 1) * 2 * V_HEAD] = o_pair.astype(BF16)
            outs = []


def _attention(q, k, vt):
    B, S, nq = q.shape
    nv = N_HEADS * V_HEAD
    return pl.pallas_call(
        _attn_kernel,
        out_shape=jax.ShapeDtypeStruct((B, S, nv), BF16),
        grid=(B, S // TQ),
        in_specs=[
            pl.BlockSpec((None, TQ, nq), lambda b, i: (b, i, 0)),
            pl.BlockSpec((None, S, nq), lambda b, i: (b, 0, 0)),
            pl.BlockSpec((None, nv, S), lambda b, i: (b, 0, 0)),
        ],
        out_specs=pl.BlockSpec((None, TQ, nv), lambda b, i: (b, i, 0)),
        compiler_params=pltpu.CompilerParams(
            dimension_semantics=("parallel", "parallel"), vmem_limit_bytes=VMEM_LIMIT),
        name="mla_attention",
    )(q, k, vt)


def _relayout_w_gate(w_ref, out_ref):
    o_gate = Q_LORA + KV_LORA + QK_ROPE + 2 * CONV_CH
    sup_start = o_gate // LANES * LANES
    off = o_gate - sup_start

    def body(r, carry):
        rs = pl.ds(pl.multiple_of(r * W_IN_ROWS, W_IN_ROWS), W_IN_ROWS)
        sup = w_ref[rs, sup_start:w_ref.shape[1]]
        out_ref[rs, :] = sup[:, off:off + out_ref.shape[1]].astype(BF16)
        return carry

    lax.fori_loop(0, w_ref.shape[0] // W_IN_ROWS, body, 0)


def _mix_kernel(x_ref, mod_ref, z_ref, zprev_ref, znext_ref, o_ref, h_ref, w_ref,
                wdw_ref, bdw_ref, gln_ref, bln_ref, wco_ref, wao_ref, wout_ref,
                out_ref, zp_ref, zs_ref, conv_ref, wgate_ref):
    D = D_MODEL
    ts = TS_MIX
    i = pl.program_id(1)
    n_i = pl.num_programs(1)
    pl.when(_first_grid_step())(lambda: _relayout_w_gate(w_ref, wgate_ref))

    zp_ref[0:HALO, :] = jnp.where(i > 0, zprev_ref[...], 0.0)
    zp_ref[HALO:HALO + ts, :] = z_ref[...]
    zp_ref[HALO + ts:2 * HALO + ts, :] = jnp.where(i < n_i - 1, znext_ref[...], 0.0)

    n_shift = zs_ref.shape[1]
    for s in range(1, SUBLANES):
        zs_ref[s - 1] = zp_ref[s:s + n_shift, :]

    gate_logits = _dot(h_ref[...], wgate_ref[...])
    y_a = _dot(o_ref[...], wao_ref[...])

    row_chunk = 64
    base = HALO - CONV_K // 2
    for cb in range(CONV_CH // LANES):
        cs = slice(cb * LANES, (cb + 1) * LANES)
        for rb in range(ts // row_chunk):
            r0 = rb * row_chunk
            acc = jnp.broadcast_to(bdw_ref[:, cs], (row_chunk, LANES))
            for kk in range(CONV_K):
                s = (base + kk) % SUBLANES
                a = r0 + base + kk - s
                src = zp_ref if s == 0 else zs_ref.at[s - 1]
                acc = acc + wdw_ref[kk:kk + 1, cs] * src[a:a + row_chunk, cs]
            conv_ref[r0:r0 + row_chunk, cs] = acc

    zc = conv_ref[...]
    mu = jnp.mean(zc, axis=-1, keepdims=True)
    zd = zc - mu
    var = jnp.mean(zd * zd, axis=-1, keepdims=True)
    zn = zd * lax.rsqrt(var + EPS_LN) * gln_ref[...] + bln_ref[...]
    zs = (zn * _sigmoid(zn)).astype(BF16)
    y_b = _dot(zs, wco_ref[...])
    gates = _sigmoid(gate_logits)
    merged = (gates[:, 0:D] * y_a + gates[:, D:2 * D] * y_b).astype(BF16)
    gate_m = mod_ref[:, 2 * D:3 * D]
    out_ref[...] = x_ref[...] + gate_m * _dot(merged, wout_ref[...])


def _mix(x, mod3, z, o, h, w_in, w_dw, b_dw, g_ln, b_ln, w_co, w_ao, w_out):
    B, S, D = x.shape
    ts = TS_MIX
    hb = ts // HALO
    n_halo = S // HALO
    tok = lambda w: pl.BlockSpec((None, ts, w), lambda b, i: (b, i, 0))
    return pl.pallas_call(
        _mix_kernel,
        out_shape=jax.ShapeDtypeStruct((B, S, D), F32),
        grid=(B, S // ts),
        in_specs=[
            tok(D),
            pl.BlockSpec((None, 1, N_MOD * D), lambda b, i: (b, 0, 0)),
            tok(CONV_CH),
            pl.BlockSpec((None, HALO, CONV_CH), lambda b, i: (b, jnp.maximum(i * hb - 1, 0), 0)),
            pl.BlockSpec((None, HALO, CONV_CH),
                         lambda b, i: (b, jnp.minimum((i + 1) * hb, n_halo - 1), 0)),
            tok(N_HEADS * V_HEAD),
            tok(D),
            _const_spec(w_in.shape),
            _const_spec(w_dw.shape),
            _const_spec(b_dw.shape),
            _const_spec(g_ln.shape),
            _const_spec(b_ln.shape),
            _const_spec(w_co.shape),
            _const_spec(w_ao.shape),
            _const_spec(w_out.shape),
        ],
        out_specs=tok(D),
        scratch_shapes=[
            pltpu.VMEM((ts + 2 * HALO, CONV_CH), F32),
            pltpu.VMEM((SUBLANES - 1, ts + 2 * HALO - SUBLANES, CONV_CH), F32),
            pltpu.VMEM((ts, CONV_CH), F32),
            pltpu.VMEM((D, 2 * D), BF16),
        ],
        compiler_params=pltpu.CompilerParams(
            dimension_semantics=("arbitrary", "arbitrary"), vmem_limit_bytes=VMEM_LIMIT),
        name="conv_merge_out",
    )(x, mod3, z, z, z, o, h, w_in, w_dw, b_dw, g_ln, b_ln, w_co, w_ao, w_out)


def _ffn_kernel(x_ref, mod_ref, gffn_ref, wg_ref, wu_ref, wd_ref, gfin_ref, out_ref, *, final_norm):
    D = D_MODEL
    x = x_ref[...]
    shift = mod_ref[:, 3 * D:4 * D]
    scale = mod_ref[:, 4 * D:5 * D]
    gate = mod_ref[:, 5 * D:6 * D]
    h = (_rms(x, gffn_ref[...]) * (1.0 + scale) + shift).astype(BF16)
    g = _dot(h, wg_ref[...])
    u = _dot(h, wu_ref[...])
    a = (g * _sigmoid(g) * u).astype(BF16)
    x2 = x + gate * _dot(a, wd_ref[...])
    out_ref[...] = _rms(x2, gfin_ref[...]) if final_norm else x2


def _ffn(x, mod3, g_ffn, w_gate, w_up, w_down, g_final, final_norm):
    B, S, D = x.shape
    tm = TM_FFN
    tok = pl.BlockSpec((None, tm, D), lambda b, i: (b, i, 0))
    return pl.pallas_call(
        functools.partial(_ffn_kernel, final_norm=final_norm),
        out_shape=jax.ShapeDtypeStruct((B, S, D), F32),
        grid=(B, S // tm),
        in_specs=[
            tok,
            pl.BlockSpec((None, 1, N_MOD * D), lambda b, i: (b, 0, 0)),
            _const_spec(g_ffn.shape),
            _const_spec(w_gate.shape),
            _const_spec(w_up.shape),
            _const_spec(w_down.shape),
            _const_spec(g_final.shape),
        ],
        out_specs=tok,
        compiler_params=pltpu.CompilerParams(
            dimension_semantics=("parallel", "parallel"), vmem_limit_bytes=VMEM_LIMIT),
        name="swiglu_final_norm",
    )(x, mod3, g_ffn, w_gate, w_up, w_down, g_final)


def _prep_w_q(w_q_up):
    r = w_q_up.shape[0]
    w = w_q_up.reshape(r, N_HEADS, QK_HEAD)
    nope = w[..., :QK_NOPE]
    r1 = w[..., QK_NOPE:QK_NOPE + HALF_ROPE]
    r2 = w[..., QK_NOPE + HALF_ROPE:]
    return jnp.concatenate([nope, r1, r2, -r2, r1], axis=-1).reshape(r, N_HEADS * HEAD_PAD).astype(BF16)


def _prep_w_kv(w_kv_up):
    r = w_kv_up.shape[0]
    w = w_kv_up.reshape(r, N_HEADS, QK_NOPE + V_HEAD)
    k_nope = w[..., :QK_NOPE]
    v = w[..., QK_NOPE:]
    k_pad = jnp.concatenate(
        [k_nope, jnp.zeros((r, N_HEADS, HEAD_PAD - QK_NOPE), w.dtype)], axis=-1)
    wk = k_pad.reshape(r, N_HEADS * HEAD_PAD).astype(BF16)
    wvt = v.reshape(r, N_HEADS * V_HEAD).T.astype(BF16)
    return wk, wvt


def kernel(x, c, positions, w_ada, b_ada, g_norm_mix, w_in, g_q_a, w_q_up, g_kv_a, w_kv_up,
           w_attn_o, w_dw, b_dw, g_conv_ln, b_conv_ln, w_conv_out, w_out, g_norm_ffn,
           w_ffn_gate, w_ffn_up, w_ffn_down, g_final):
    B, S, D = x.shape
    depth = w_ada.shape[0]
    cos_c, sin_c = _rope_tables(positions)
    for l in range(depth):
        mod3 = _modulation(c, w_ada[l], b_ada[l][None, :]).reshape(B, 1, N_MOD * D)
        wk, wvt = _prep_w_kv(w_kv_up[l])
        q, k, vt, z, h = _inproj(
            x, mod3, g_norm_mix[l][None, :], w_in[l], g_q_a[l][None, :],
            _prep_w_q(w_q_up[l]), g_kv_a[l][None, :], wk, wvt, cos_c, sin_c)
        o = _attention(q, k, vt)
        x = _mix(x, mod3, z, o, h, w_in[l], w_dw[l], b_dw[l][None, :], g_conv_ln[l][None, :],
                 b_conv_ln[l][None, :], w_conv_out[l].astype(BF16), w_attn_o[l].astype(BF16),
                 w_out[l].astype(BF16))
        x = _ffn(x, mod3, g_norm_ffn[l][None, :], w_ffn_gate[l].astype(BF16),
                 w_ffn_up[l].astype(BF16), w_ffn_down[l].astype(BF16), g_final[None, :],
                 final_norm=(l == depth - 1))
    return x
```

```python
import functools
import math

import jax
import jax.numpy as jnp
from jax import lax
from jax.experimental import pallas as pl
from jax.experimental.pallas import tpu as pltpu

F32 = jnp.float32
BF16 = jnp.bfloat16

D_MODEL = 1024
N_HEADS = 8
Q_LORA = 256
KV_LORA = 128
QK_NOPE = 64
QK_ROPE = 32
HALF_ROPE = QK_ROPE // 2
V_HEAD = 64
QK_HEAD = QK_NOPE + QK_ROPE
ATTN_SCALE = 1.0 / math.sqrt(QK_HEAD)
Q_SCALE = ATTN_SCALE * math.log2(math.e)
ROPE_THETA = 10000.0
CONV_CH = 512
CONV_K = 31
N_MOD = 6
EPS_RMS = 1e-6
EPS_LN = 1e-5

LANES = 128
SUBLANES = 8
HEAD_PAD = LANES
VMEM_LIMIT = 56 * 1024 * 1024

C_QA = 0
C_KVA = C_QA + Q_LORA
C_KR = C_KVA + KV_LORA
C_CONV = C_KR + HEAD_PAD

TS_IN = 512
TQ = 512
KEY_CHUNK = 1024
TS_MIX = 512
HALO = 16
TM_FFN = 512


def _sigmoid(x):
    return 1.0 / (1.0 + jnp.exp(-x))


def _rms(x, g):
    return x * lax.rsqrt(jnp.mean(x * x, axis=-1, keepdims=True) + EPS_RMS) * g


def _dot(a, b):
    return jnp.dot(a, b, preferred_element_type=F32)


def _const_spec(shape):
    nd = len(shape)
    return pl.BlockSpec(shape, lambda *_: (0,) * nd, pipeline_mode=pl.Buffered(1))


def _rope_kernel(pos_ref, freq_ref, cos_ref, sin_ref):
    ang = pos_ref[...] * freq_ref[...]
    cos_ref[...] = jnp.cos(ang)
    sin_ref[...] = jnp.sin(ang)


def _rope_tables(positions):
    B, S = positions.shape
    inv_freq = ROPE_THETA ** (-jnp.arange(0, QK_ROPE, 2, dtype=F32) / QK_ROPE)
    rows = B * S * HALF_ROPE // LANES
    pos_rep = jnp.repeat(positions.reshape(-1).astype(F32), HALF_ROPE).reshape(rows, LANES)
    freq = jnp.tile(inv_freq, LANES // HALF_ROPE)[None, :]
    cos, sin = pl.pallas_call(
        _rope_kernel,
        out_shape=(jax.ShapeDtypeStruct((rows, LANES), F32),) * 2,
        name="rope_tables",
    )(pos_rep, freq)
    return cos, sin


def _expand_rope_table(compact_ref, out_ref, nope_value):
    pos_per_row = LANES // HALF_ROPE
    n = compact_ref.shape[0]
    comp = compact_ref[...]
    lane = lax.broadcasted_iota(jnp.int32, (n, LANES), 1)
    for j in range(pos_per_row):
        shift = (QK_NOPE - HALF_ROPE * j) % LANES
        first = pltpu.roll(comp, shift, 1) if shift else comp
        second = pltpu.roll(first, HALF_ROPE, 1)
        out_ref[pl.ds(j, n, stride=pos_per_row), :] = jnp.where(
            lane < QK_NOPE, nope_value,
            jnp.where(lane < QK_NOPE + HALF_ROPE, first,
                      jnp.where(lane < QK_HEAD, second, 0.0)))


def _rope_tile(t, ctab, stab):
    swapped = pltpu.roll(t, HEAD_PAD - QK_ROPE, 1)
    return t * ctab + swapped * stab


def _mod_kernel(c_ref, w_ref, b_ref, o_ref):
    c = c_ref[...]
    c_act = (c * _sigmoid(c)).astype(BF16)
    o_ref[...] = _dot(c_act, w_ref[...].astype(BF16)) + b_ref[...]


def _modulation(c, w_ada, b_ada):
    B, D = c.shape
    n = w_ada.shape[1]
    bn = 1536
    return pl.pallas_call(
        _mod_kernel,
        out_shape=jax.ShapeDtypeStruct((B, n), F32),
        grid=(n // bn,),
        in_specs=[
            pl.BlockSpec((B, D), lambda j: (0, 0)),
            pl.BlockSpec((D, bn), lambda j: (0, j)),
            pl.BlockSpec((1, bn), lambda j: (0, j)),
        ],
        out_specs=pl.BlockSpec((B, bn), lambda j: (0, j)),
        compiler_params=pltpu.CompilerParams(
            dimension_semantics=("parallel",), vmem_limit_bytes=VMEM_LIMIT),
        name="adaln_modulation",
    )(c, w_ada, b_ada)


def _mixer_input(x_ref, mod_ref, gmix_ref):
    D = D_MODEL
    shift = mod_ref[:, 0:D]
    scale = mod_ref[:, D:2 * D]
    return (_rms(x_ref[...], gmix_ref[...]) * (1.0 + scale) + shift).astype(BF16)


def _first_grid_step():
    return jnp.logical_and(pl.program_id(0) == 0, pl.program_id(1) == 0)


def _transpose_rows(wt_ref, row0, n_rows, out_ref, col0):
    for j in range(n_rows // LANES):
        rows = wt_ref[row0 + j * LANES:row0 + (j + 1) * LANES, :]
        out_ref[:, col0 + j * LANES:col0 + (j + 1) * LANES] = rows.T.astype(BF16)


def _relayout_w_proj(wt_ref, out_ref):
    o_kr = Q_LORA + KV_LORA
    o_conv = o_kr + QK_ROPE
    _transpose_rows(wt_ref, 0, o_kr, out_ref, C_QA)
    k1 = wt_ref[o_kr:o_kr + HALF_ROPE, :]
    k2 = wt_ref[o_kr + HALF_ROPE:o_conv, :]
    key_tile_t = jnp.concatenate([jnp.zeros((QK_NOPE, k1.shape[1]), F32), k1, k2, -k2, k1], axis=0)
    out_ref[:, C_KR:C_KR + HEAD_PAD] = key_tile_t.T.astype(BF16)
    _transpose_rows(wt_ref, o_conv, 2 * CONV_CH, out_ref, C_CONV)


def _inproj_kernel(x_ref, mod_ref, gmix_ref, w_ref, gq_ref, wq_ref, gkv_ref, wk_ref, wvt_ref,
                   cosc_ref, sinc_ref,
                   q_ref, k_ref, vt_ref, z_ref, h_ref, ctab_ref, stab_ref, win_ref):
    pl.when(_first_grid_step())(lambda: _relayout_w_proj(w_ref, win_ref))
    h = _mixer_input(x_ref, mod_ref, gmix_ref)
    h_ref[...] = h
    _expand_rope_table(cosc_ref, ctab_ref, 1.0)
    _expand_rope_table(sinc_ref, stab_ref, 0.0)
    ctab = ctab_ref[...]
    stab = stab_ref[...]

    qa = _dot(h, win_ref[:, C_QA:C_QA + Q_LORA])
    qn = _rms(qa, gq_ref[...]).astype(BF16)
    q_all = _dot(qn, wq_ref[...])
    for hd in range(N_HEADS):
        sl = slice(hd * HEAD_PAD, (hd + 1) * HEAD_PAD)
        q_ref[:, sl] = (_rope_tile(q_all[:, sl], ctab, stab) * Q_SCALE).astype(BF16)

    kva = _dot(h, win_ref[:, C_KVA:C_KVA + KV_LORA])
    kvn = _rms(kva, gkv_ref[...]).astype(BF16)
    k_pad = _dot(kvn, wk_ref[...])
    vt_ref[...] = lax.dot_general(wvt_ref[...], kvn, (((1,), (1,)), ((), ())),
                                  preferred_element_type=F32).astype(BF16)
    kr_rot = _rope_tile(_dot(h, win_ref[:, C_KR:C_KR + HEAD_PAD]), ctab, stab)
    for hd in range(N_HEADS):
        sl = slice(hd * HEAD_PAD, (hd + 1) * HEAD_PAD)
        k_ref[:, sl] = (k_pad[:, sl] + kr_rot).astype(BF16)

    cu = _dot(h, win_ref[:, C_CONV:C_CONV + 2 * CONV_CH])
    z_ref[...] = cu[:, 0:CONV_CH] * _sigmoid(cu[:, CONV_CH:2 * CONV_CH])


def _inproj(x, mod3, g_mix, w_in, g_q, wq2, g_kv, wk, wvt, cos_c, sin_c):
    B, S, D = x.shape
    ts = TS_IN
    nq = N_HEADS * HEAD_PAD
    nv = N_HEADS * V_HEAD
    tok = lambda w: pl.BlockSpec((None, ts, w), lambda b, i: (b, i, 0))
    rows = ts * HALF_ROPE // LANES
    n_i = S // ts
    compact = pl.BlockSpec((rows, LANES), lambda b, i: (b * n_i + i, 0))
    return pl.pallas_call(
        _inproj_kernel,
        out_shape=(
            jax.ShapeDtypeStruct((B, S, nq), BF16),
            jax.ShapeDtypeStruct((B, S, nq), BF16),
            jax.ShapeDtypeStruct((B, nv, S), BF16),
            jax.ShapeDtypeStruct((B, S, CONV_CH), F32),
            jax.ShapeDtypeStruct((B, S, D), BF16),
        ),
        grid=(B, S // ts),
        in_specs=[
            tok(D),
            pl.BlockSpec((None, 1, N_MOD * D), lambda b, i: (b, 0, 0)),
            _const_spec(g_mix.shape),
            _const_spec(w_in.shape),
            _const_spec(g_q.shape),
            _const_spec(wq2.shape),
            _const_spec(g_kv.shape),
            _const_spec(wk.shape),
            _const_spec(wvt.shape),
            compact,
            compact,
        ],
        out_specs=(tok(nq), tok(nq), pl.BlockSpec((None, nv, ts), lambda b, i: (b, 0, i)),
                   tok(CONV_CH), tok(D)),
        scratch_shapes=[pltpu.VMEM((ts, HEAD_PAD), F32), pltpu.VMEM((ts, HEAD_PAD), F32),
                        pltpu.VMEM((D, C_CONV + 2 * CONV_CH), BF16)],
        compiler_params=pltpu.CompilerParams(
            dimension_semantics=("arbitrary", "arbitrary"), vmem_limit_bytes=VMEM_LIMIT),
        name="input_projection",
    )(x, mod3, g_mix, w_in, g_q, wq2, g_kv, wk, wvt, cos_c, sin_c)


def _attn_kernel(q_ref, k_ref, vt_ref, o_ref):
    n_kc = k_ref.shape[0] // KEY_CHUNK

    def score_chunk(hd, c):
        sl = slice(hd * HEAD_PAD, (hd + 1) * HEAD_PAD)
        ks = slice(c * KEY_CHUNK, (c + 1) * KEY_CHUNK)
        return lax.dot_general(k_ref[ks, sl], q_ref[:, sl], (((1,), (1,)), ((), ())),
                               preferred_element_type=F32)

    def col_max(chunks):
        m = jnp.max(chunks[0], axis=0, keepdims=True)
        for st in chunks[1:]:
            m = jnp.maximum(m, jnp.max(st, axis=0, keepdims=True))
        return m

    st_next = [score_chunk(0, c) for c in range(n_kc)]
    outs = []
    for hd in range(N_HEADS):
        st_cur, st_next = st_next, []
        m = col_max(st_cur)
        l = acc = None
        for c in range(n_kc):
            if hd + 1 < N_HEADS:
                st_next.append(score_chunk(hd + 1, c))
            ks = slice(c * KEY_CHUNK, (c + 1) * KEY_CHUNK)
            p = jnp.exp2(st_cur[c] - m)
            ls = jnp.sum(p, axis=0, keepdims=True)
            pv = _dot(vt_ref[hd * V_HEAD:(hd + 1) * V_HEAD, ks], p.astype(BF16))
            l, acc = (ls, pv) if c == 0 else (l + ls, acc + pv)
        outs.append(acc / l)
        if hd % 2 == 1:
            o_pair = jnp.concatenate(outs, axis=0).T
            o_ref[:, (hd // 2) * 2 * V_HEAD:(hd // 2 + 1) * 2 * V_HEAD] = o_pair.astype(BF16)
            outs = []


def _attention(q, k, vt):
    B, S, nq = q.shape
    nv = N_HEADS * V_HEAD
    return pl.pallas_call(
        _attn_kernel,
        out_shape=jax.ShapeDtypeStruct((B, S, nv), BF16),
        grid=(B, S // TQ),
        in_specs=[
            pl.BlockSpec((None, TQ, nq), lambda b, i: (b, i, 0)),
            pl.BlockSpec((None, S, nq), lambda b, i: (b, 0, 0)),
            pl.BlockSpec((None, nv, S), lambda b, i: (b, 0, 0)),
        ],
        out_specs=pl.BlockSpec((None, TQ, nv), lambda b, i: (b, i, 0)),
        compiler_params=pltpu.CompilerParams(
            dimension_semantics=("parallel", "parallel"), vmem_limit_bytes=VMEM_LIMIT),
        name="mla_attention",
    )(q, k, vt)


def _relayout_w_gate(wt_ref, out_ref):
    o_gate = Q_LORA + KV_LORA + QK_ROPE + 2 * CONV_CH
    _transpose_rows(wt_ref, o_gate, out_ref.shape[1], out_ref, 0)


def _mix_kernel(x_ref, mod_ref, z_ref, zprev_ref, znext_ref, o_ref, h_ref, w_ref,
                wdw_ref, bdw_ref, gln_ref, bln_ref, wco_ref, wao_ref, wout_ref,
                out_ref, zp_ref, zs_ref, conv_ref, wgate_ref):
    D = D_MODEL
    ts = TS_MIX
    i = pl.program_id(1)
    n_i = pl.num_programs(1)
    pl.when(_first_grid_step())(lambda: _relayout_w_gate(w_ref, wgate_ref))

    zp_ref[0:HALO, :] = jnp.where(i > 0, zprev_ref[...], 0.0)
    zp_ref[HALO:HALO + ts, :] = z_ref[...]
    zp_ref[HALO + ts:2 * HALO + ts, :] = jnp.where(i < n_i - 1, znext_ref[...], 0.0)

    n_shift = zs_ref.shape[1]
    for s in range(1, SUBLANES):
        zs_ref[s - 1] = zp_ref[s:s + n_shift, :]

    gate_logits = _dot(h_ref[...], wgate_ref[...])
    y_a = _dot(o_ref[...], wao_ref[...])

    row_chunk = 64
    base = HALO - CONV_K // 2
    for cb in range(CONV_CH // LANES):
        cs = slice(cb * LANES, (cb + 1) * LANES)
        for rb in range(ts // row_chunk):
            r0 = rb * row_chunk
            acc = jnp.broadcast_to(bdw_ref[:, cs], (row_chunk, LANES))
            for kk in range(CONV_K):
                s = (base + kk) % SUBLANES
                a = r0 + base + kk - s
                src = zp_ref if s == 0 else zs_ref.at[s - 1]
                acc = acc + wdw_ref[kk:kk + 1, cs] * src[a:a + row_chunk, cs]
            conv_ref[r0:r0 + row_chunk, cs] = acc

    zc = conv_ref[...]
    mu = jnp.mean(zc, axis=-1, keepdims=True)
    zd = zc - mu
    var = jnp.mean(zd * zd, axis=-1, keepdims=True)
    zn = zd * lax.rsqrt(var + EPS_LN) * gln_ref[...] + bln_ref[...]
    zs = (zn * _sigmoid(zn)).astype(BF16)
    y_b = _dot(zs, wco_ref[...])
    gates = _sigmoid(gate_logits)
    merged = (gates[:, 0:D] * y_a + gates[:, D:2 * D] * y_b).astype(BF16)
    gate_m = mod_ref[:, 2 * D:3 * D]
    out_ref[...] = x_ref[...] + gate_m * _dot(merged, wout_ref[...])


def _mix(x, mod3, z, o, h, w_in, w_dw, b_dw, g_ln, b_ln, w_co, w_ao, w_out):
    B, S, D = x.shape
    ts = TS_MIX
    hb = ts // HALO
    n_halo = S // HALO
    tok = lambda w: pl.BlockSpec((None, ts, w), lambda b, i: (b, i, 0))
    return pl.pallas_call(
        _mix_kernel,
        out_shape=jax.ShapeDtypeStruct((B, S, D), F32),
        grid=(B, S // ts),
        in_specs=[
            tok(D),
            pl.BlockSpec((None, 1, N_MOD * D), lambda b, i: (b, 0, 0)),
            tok(CONV_CH),
            pl.BlockSpec((None, HALO, CONV_CH), lambda b, i: (b, jnp.maximum(i * hb - 1, 0), 0)),
            pl.BlockSpec((None, HALO, CONV_CH),
                         lambda b, i: (b, jnp.minimum((i + 1) * hb, n_halo - 1), 0)),
            tok(N_HEADS * V_HEAD),
            tok(D),
            _const_spec(w_in.shape),
            _const_spec(w_dw.shape),
            _const_spec(b_dw.shape),
            _const_spec(g_ln.shape),
            _const_spec(b_ln.shape),
            _const_spec(w_co.shape),
            _const_spec(w_ao.shape),
            _const_spec(w_out.shape),
        ],
        out_specs=tok(D),
        scratch_shapes=[
            pltpu.VMEM((ts + 2 * HALO, CONV_CH), F32),
            pltpu.VMEM((SUBLANES - 1, ts + 2 * HALO - SUBLANES, CONV_CH), F32),
            pltpu.VMEM((ts, CONV_CH), F32),
            pltpu.VMEM((D, 2 * D), BF16),
        ],
        compiler_params=pltpu.CompilerParams(
            dimension_semantics=("arbitrary", "arbitrary"), vmem_limit_bytes=VMEM_LIMIT),
        name="conv_merge_out",
    )(x, mod3, z, z, z, o, h, w_in, w_dw, b_dw, g_ln, b_ln, w_co, w_ao, w_out)


def _ffn_kernel(x_ref, mod_ref, gffn_ref, wg_ref, wu_ref, wd_ref, gfin_ref, out_ref, *, final_norm):
    D = D_MODEL
    x = x_ref[...]
    shift = mod_ref[:, 3 * D:4 * D]
    scale = mod_ref[:, 4 * D:5 * D]
    gate = mod_ref[:, 5 * D:6 * D]
    h = (_rms(x, gffn_ref[...]) * (1.0 + scale) + shift).astype(BF16)
    g = _dot(h, wg_ref[...])
    u = _dot(h, wu_ref[...])
    a = (g * _sigmoid(g) * u).astype(BF16)
    x2 = x + gate * _dot(a, wd_ref[...])
    out_ref[...] = _rms(x2, gfin_ref[...]) if final_norm else x2


def _ffn(x, mod3, g_ffn, w_gate, w_up, w_down, g_final, final_norm):
    B, S, D = x.shape
    tm = TM_FFN
    tok = pl.BlockSpec((None, tm, D), lambda b, i: (b, i, 0))
    return pl.pallas_call(
        functools.partial(_ffn_kernel, final_norm=final_norm),
        out_shape=jax.ShapeDtypeStruct((B, S, D), F32),
        grid=(B, S // tm),
        in_specs=[
            tok,
            pl.BlockSpec((None, 1, N_MOD * D), lambda b, i: (b, 0, 0)),
            _const_spec(g_ffn.shape),
            _const_spec(w_gate.shape),
            _const_spec(w_up.shape),
            _const_spec(w_down.shape),
            _const_spec(g_final.shape),
        ],
        out_specs=tok,
        compiler_params=pltpu.CompilerParams(
            dimension_semantics=("parallel", "parallel"), vmem_limit_bytes=VMEM_LIMIT),
        name="swiglu_final_norm",
    )(x, mod3, g_ffn, w_gate, w_up, w_down, g_final)


def _prep_w_q(w_q_up):
    r = w_q_up.shape[0]
    w = w_q_up.reshape(r, N_HEADS, QK_HEAD)
    nope = w[..., :QK_NOPE]
    r1 = w[..., QK_NOPE:QK_NOPE + HALF_ROPE]
    r2 = w[..., QK_NOPE + HALF_ROPE:]
    return jnp.concatenate([nope, r1, r2, -r2, r1], axis=-1).reshape(r, N_HEADS * HEAD_PAD).astype(BF16)


def _prep_w_kv(w_kv_up):
    r = w_kv_up.shape[0]
    w = w_kv_up.reshape(r, N_HEADS, QK_NOPE + V_HEAD)
    k_nope = w[..., :QK_NOPE]
    v = w[..., QK_NOPE:]
    k_pad = jnp.concatenate(
        [k_nope, jnp.zeros((r, N_HEADS, HEAD_PAD - QK_NOPE), w.dtype)], axis=-1)
    wk = k_pad.reshape(r, N_HEADS * HEAD_PAD).astype(BF16)
    wvt = v.reshape(r, N_HEADS * V_HEAD).T.astype(BF16)
    return wk, wvt


def kernel(x, c, positions, w_ada, b_ada, g_norm_mix, w_in, g_q_a, w_q_up, g_kv_a, w_kv_up,
           w_attn_o, w_dw, b_dw, g_conv_ln, b_conv_ln, w_conv_out, w_out, g_norm_ffn,
           w_ffn_gate, w_ffn_up, w_ffn_down, g_final):
    B, S, D = x.shape
    depth = w_ada.shape[0]
    cos_c, sin_c = _rope_tables(positions)
    for l in range(depth):
        mod3 = _modulation(c, w_ada[l], b_ada[l][None, :]).reshape(B, 1, N_MOD * D)
        wk, wvt = _prep_w_kv(w_kv_up[l])
        w_in_t = w_in[l].T
        q, k, vt, z, h = _inproj(
            x, mod3, g_norm_mix[l][None, :], w_in_t, g_q_a[l][None, :],
            _prep_w_q(w_q_up[l]), g_kv_a[l][None, :], wk, wvt, cos_c, sin_c)
        o = _attention(q, k, vt)
        x = _mix(x, mod3, z, o, h, w_in_t, w_dw[l], b_dw[l][None, :], g_conv_ln[l][None, :],
                 b_conv_ln[l][None, :], w_conv_out[l].astype(BF16), w_attn_o[l].astype(BF16),
                 w_out[l].astype(BF16))
        x = _ffn(x, mod3, g_norm_ffn[l][None, :], w_ffn_gate[l].astype(BF16),
                 w_ffn_up[l].astype(BF16), w_ffn_down[l].astype(BF16), g_final[None, :],
                 final_norm=(l == depth - 1))
    return x
```

```python
import functools
import math

import jax
import jax.numpy as jnp
from jax import lax
from jax.experimental import pallas as pl
from jax.experimental.pallas import tpu as pltpu

F32 = jnp.float32
BF16 = jnp.bfloat16

D_MODEL = 1024
N_HEADS = 8
Q_LORA = 256
KV_LORA = 128
QK_NOPE = 64
QK_ROPE = 32
HALF_ROPE = QK_ROPE // 2
V_HEAD = 64
QK_HEAD = QK_NOPE + QK_ROPE
ATTN_SCALE = 1.0 / math.sqrt(QK_HEAD)
Q_SCALE = ATTN_SCALE * math.log2(math.e)
ROPE_THETA = 10000.0
CONV_CH = 512
CONV_K = 31
N_MOD = 6
EPS_RMS = 1e-6
EPS_LN = 1e-5

LANES = 128
SUBLANES = 8
HEAD_PAD = LANES
VMEM_LIMIT = 56 * 1024 * 1024

C_QA = 0
C_KVA = C_QA + Q_LORA
C_KR = C_KVA + KV_LORA
C_CONV = C_KR + HEAD_PAD

TS_IN = 512
TQ = 512
KEY_CHUNK = 1024
TS_MIX = 512
HALO = 16
TM_FFN = 512
ROW_SUBTILES = 2


def _sigmoid(x):
    return 1.0 / (1.0 + jnp.exp(-x))


def _rms(x, g):
    return x * lax.rsqrt(jnp.mean(x * x, axis=-1, keepdims=True) + EPS_RMS) * g


def _dot(a, b):
    return jnp.dot(a, b, preferred_element_type=F32)


def _const_spec(shape):
    nd = len(shape)
    return pl.BlockSpec(shape, lambda *_: (0,) * nd, pipeline_mode=pl.Buffered(1))


def _rope_kernel(pos_ref, freq_ref, cos_ref, sin_ref):
    ang = pos_ref[...] * freq_ref[...]
    cos_ref[...] = jnp.cos(ang)
    sin_ref[...] = jnp.sin(ang)


def _rope_tables(positions):
    B, S = positions.shape
    inv_freq = ROPE_THETA ** (-jnp.arange(0, QK_ROPE, 2, dtype=F32) / QK_ROPE)
    rows = B * S * HALF_ROPE // LANES
    pos_rep = jnp.repeat(positions.reshape(-1).astype(F32), HALF_ROPE).reshape(rows, LANES)
    freq = jnp.tile(inv_freq, LANES // HALF_ROPE)[None, :]
    cos, sin = pl.pallas_call(
        _rope_kernel,
        out_shape=(jax.ShapeDtypeStruct((rows, LANES), F32),) * 2,
        name="rope_tables",
    )(pos_rep, freq)
    return cos, sin


def _expand_rope_table(compact_ref, out_ref, nope_value):
    pos_per_row = LANES // HALF_ROPE
    n = compact_ref.shape[0]
    comp = compact_ref[...]
    lane = lax.broadcasted_iota(jnp.int32, (n, LANES), 1)
    for j in range(pos_per_row):
        shift = (QK_NOPE - HALF_ROPE * j) % LANES
        first = pltpu.roll(comp, shift, 1) if shift else comp
        second = pltpu.roll(first, HALF_ROPE, 1)
        out_ref[pl.ds(j, n, stride=pos_per_row), :] = jnp.where(
            lane < QK_NOPE, nope_value,
            jnp.where(lane < QK_NOPE + HALF_ROPE, first,
                      jnp.where(lane < QK_HEAD, second, 0.0)))


def _rope_tile(t, ctab, stab):
    swapped = pltpu.roll(t, HEAD_PAD - QK_ROPE, 1)
    return t * ctab + swapped * stab


def _mod_kernel(c_ref, w_ref, b_ref, o_ref):
    c = c_ref[...]
    c_act = (c * _sigmoid(c)).astype(BF16)
    o_ref[...] = _dot(c_act, w_ref[...].astype(BF16)) + b_ref[...]


def _modulation(c, w_ada, b_ada):
    B, D = c.shape
    n = w_ada.shape[1]
    bn = 1536
    return pl.pallas_call(
        _mod_kernel,
        out_shape=jax.ShapeDtypeStruct((B, n), F32),
        grid=(n // bn,),
        in_specs=[
            pl.BlockSpec((B, D), lambda j: (0, 0)),
            pl.BlockSpec((D, bn), lambda j: (0, j)),
            pl.BlockSpec((1, bn), lambda j: (0, j)),
        ],
        out_specs=pl.BlockSpec((B, bn), lambda j: (0, j)),
        compiler_params=pltpu.CompilerParams(
            dimension_semantics=("parallel",), vmem_limit_bytes=VMEM_LIMIT),
        name="adaln_modulation",
    )(c, w_ada, b_ada)


def _mixer_input(x, mod_ref, gmix_ref):
    D = D_MODEL
    shift = mod_ref[:, 0:D]
    scale = mod_ref[:, D:2 * D]
    return (_rms(x, gmix_ref[...]) * (1.0 + scale) + shift).astype(BF16)


def _first_grid_step():
    return jnp.logical_and(pl.program_id(0) == 0, pl.program_id(1) == 0)


def _transpose_rows(wt_ref, row0, n_rows, out_ref, col0):
    for j in range(n_rows // LANES):
        rows = wt_ref[row0 + j * LANES:row0 + (j + 1) * LANES, :]
        out_ref[:, col0 + j * LANES:col0 + (j + 1) * LANES] = rows.T.astype(BF16)


def _relayout_w_proj(wt_ref, out_ref):
    o_kr = Q_LORA + KV_LORA
    o_conv = o_kr + QK_ROPE
    _transpose_rows(wt_ref, 0, o_kr, out_ref, C_QA)
    k1 = wt_ref[o_kr:o_kr + HALF_ROPE, :]
    k2 = wt_ref[o_kr + HALF_ROPE:o_conv, :]
    key_tile_t = jnp.concatenate([jnp.zeros((QK_NOPE, k1.shape[1]), F32), k1, k2, -k2, k1], axis=0)
    out_ref[:, C_KR:C_KR + HEAD_PAD] = key_tile_t.T.astype(BF16)
    _transpose_rows(wt_ref, o_conv, 2 * CONV_CH, out_ref, C_CONV)


def _inproj_kernel(x_ref, mod_ref, gmix_ref, w_ref, gq_ref, wq_ref, gkv_ref, wk_ref, wvt_ref,
                   cosc_ref, sinc_ref,
                   q_ref, k_ref, vt_ref, z_ref, h_ref, ctab_ref, stab_ref, win_ref):
    pl.when(_first_grid_step())(lambda: _relayout_w_proj(w_ref, win_ref))
    h = _mixer_input(x_ref[...], mod_ref, gmix_ref)
    h_ref[...] = h
    _expand_rope_table(cosc_ref, ctab_ref, 1.0)
    _expand_rope_table(sinc_ref, stab_ref, 0.0)
    ctab = ctab_ref[...]
    stab = stab_ref[...]

    qa = _dot(h, win_ref[:, C_QA:C_QA + Q_LORA])
    qn = _rms(qa, gq_ref[...]).astype(BF16)
    q_all = _dot(qn, wq_ref[...])
    for hd in range(N_HEADS):
        sl = slice(hd * HEAD_PAD, (hd + 1) * HEAD_PAD)
        q_ref[:, sl] = (_rope_tile(q_all[:, sl], ctab, stab) * Q_SCALE).astype(BF16)

    kva = _dot(h, win_ref[:, C_KVA:C_KVA + KV_LORA])
    kvn = _rms(kva, gkv_ref[...]).astype(BF16)
    k_pad = _dot(kvn, wk_ref[...])
    vt_ref[...] = lax.dot_general(wvt_ref[...], kvn, (((1,), (1,)), ((), ())),
                                  preferred_element_type=F32).astype(BF16)
    kr_rot = _rope_tile(_dot(h, win_ref[:, C_KR:C_KR + HEAD_PAD]), ctab, stab)
    for hd in range(N_HEADS):
        sl = slice(hd * HEAD_PAD, (hd + 1) * HEAD_PAD)
        k_ref[:, sl] = (k_pad[:, sl] + kr_rot).astype(BF16)

    cu = _dot(h, win_ref[:, C_CONV:C_CONV + 2 * CONV_CH])
    z_ref[...] = cu[:, 0:CONV_CH] * _sigmoid(cu[:, CONV_CH:2 * CONV_CH])


def _inproj(x, mod3, g_mix, w_in, g_q, wq2, g_kv, wk, wvt, cos_c, sin_c):
    B, S, D = x.shape
    ts = TS_IN
    nq = N_HEADS * HEAD_PAD
    nv = N_HEADS * V_HEAD
    tok = lambda w: pl.BlockSpec((None, ts, w), lambda b, i: (b, i, 0))
    rows = ts * HALF_ROPE // LANES
    n_i = S // ts
    compact = pl.BlockSpec((rows, LANES), lambda b, i: (b * n_i + i, 0))
    return pl.pallas_call(
        _inproj_kernel,
        out_shape=(
            jax.ShapeDtypeStruct((B, S, nq), BF16),
            jax.ShapeDtypeStruct((B, S, nq), BF16),
            jax.ShapeDtypeStruct((B, nv, S), BF16),
            jax.ShapeDtypeStruct((B, S, CONV_CH), F32),
            jax.ShapeDtypeStruct((B, S, D), BF16),
        ),
        grid=(B, S // ts),
        in_specs=[
            tok(D),
            pl.BlockSpec((None, 1, N_MOD * D), lambda b, i: (b, 0, 0)),
            _const_spec(g_mix.shape),
            _const_spec(w_in.shape),
            _const_spec(g_q.shape),
            _const_spec(wq2.shape),
            _const_spec(g_kv.shape),
            _const_spec(wk.shape),
            _const_spec(wvt.shape),
            compact,
            compact,
        ],
        out_specs=(tok(nq), tok(nq), pl.BlockSpec((None, nv, ts), lambda b, i: (b, 0, i)),
                   tok(CONV_CH), tok(D)),
        scratch_shapes=[pltpu.VMEM((ts, HEAD_PAD), F32), pltpu.VMEM((ts, HEAD_PAD), F32),
                        pltpu.VMEM((D, C_CONV + 2 * CONV_CH), BF16)],
        compiler_params=pltpu.CompilerParams(
            dimension_semantics=("arbitrary", "arbitrary"), vmem_limit_bytes=VMEM_LIMIT),
        name="input_projection",
    )(x, mod3, g_mix, w_in, g_q, wq2, g_kv, wk, wvt, cos_c, sin_c)


def _attn_kernel(q_ref, k_ref, vt_ref, o_ref):
    n_kc = k_ref.shape[0] // KEY_CHUNK

    def score_chunk(hd, c):
        sl = slice(hd * HEAD_PAD, (hd + 1) * HEAD_PAD)
        ks = slice(c * KEY_CHUNK, (c + 1) * KEY_CHUNK)
        return lax.dot_general(k_ref[ks, sl], q_ref[:, sl], (((1,), (1,)), ((), ())),
                               preferred_element_type=F32)

    def col_max(chunks):
        m = jnp.max(chunks[0], axis=0, keepdims=True)
        for st in chunks[1:]:
            m = jnp.maximum(m, jnp.max(st, axis=0, keepdims=True))
        return m

    st_next = [score_chunk(0, c) for c in range(n_kc)]
    outs = []
    for hd in range(N_HEADS):
        st_cur, st_next = st_next, []
        m = col_max(st_cur)
        l = acc = None
        for c in range(n_kc):
            if hd + 1 < N_HEADS:
                st_next.append(score_chunk(hd + 1, c))
            ks = slice(c * KEY_CHUNK, (c + 1) * KEY_CHUNK)
            p = jnp.exp2(st_cur[c] - m)
            ls = jnp.sum(p, axis=0, keepdims=True)
            pv = _dot(vt_ref[hd * V_HEAD:(hd + 1) * V_HEAD, ks], p.astype(BF16))
            l, acc = (ls, pv) if c == 0 else (l + ls, acc + pv)
        outs.append(acc / l)
        if hd % 2 == 1:
            o_pair = jnp.concatenate(outs, axis=0).T
            o_ref[:, (hd // 2) * 2 * V_HEAD:(hd // 2 + 1) * 2 * V_HEAD] = o_pair.astype(BF16)
            outs = []


def _attention(q, k, vt):
    B, S, nq = q.shape
    nv = N_HEADS * V_HEAD
    return pl.pallas_call(
        _attn_kernel,
        out_shape=jax.ShapeDtypeStruct((B, S, nv), BF16),
        grid=(B, S // TQ),
        in_specs=[
            pl.BlockSpec((None, TQ, nq), lambda b, i: (b, i, 0)),
            pl.BlockSpec((None, S, nq), lambda b, i: (b, 0, 0)),
            pl.BlockSpec((None, nv, S), lambda b, i: (b, 0, 0)),
        ],
        out_specs=pl.BlockSpec((None, TQ, nv), lambda b, i: (b, i, 0)),
        compiler_params=pltpu.CompilerParams(
            dimension_semantics=("parallel", "parallel"), vmem_limit_bytes=VMEM_LIMIT),
        name="mla_attention",
    )(q, k, vt)


def _relayout_w_gate(wt_ref, out_ref):
    o_gate = Q_LORA + KV_LORA + QK_ROPE + 2 * CONV_CH
    _transpose_rows(wt_ref, o_gate, out_ref.shape[1], out_ref, 0)


def _mix_kernel(x_ref, mod_ref, z_ref, zprev_ref, znext_ref, o_ref, h_ref, w_ref,
                wdw_ref, bdw_ref, gln_ref, bln_ref, wco_ref, wao_ref, wout_ref,
                out_ref, zp_ref, zs_ref, conv_ref, wgate_ref):
    D = D_MODEL
    ts = TS_MIX
    i = pl.program_id(1)
    n_i = pl.num_programs(1)
    pl.when(_first_grid_step())(lambda: _relayout_w_gate(w_ref, wgate_ref))

    zp_ref[0:HALO, :] = jnp.where(i > 0, zprev_ref[...], 0.0)
    zp_ref[HALO:HALO + ts, :] = z_ref[...]
    zp_ref[HALO + ts:2 * HALO + ts, :] = jnp.where(i < n_i - 1, znext_ref[...], 0.0)

    n_shift = zs_ref.shape[1]
    for s in range(1, SUBLANES):
        zs_ref[s - 1] = zp_ref[s:s + n_shift, :]

    gate_logits = _dot(h_ref[...], wgate_ref[...])
    y_a = _dot(o_ref[...], wao_ref[...])

    row_chunk = 64
    base = HALO - CONV_K // 2
    for cb in range(CONV_CH // LANES):
        cs = slice(cb * LANES, (cb + 1) * LANES)
        for rb in range(ts // row_chunk):
            r0 = rb * row_chunk
            acc = jnp.broadcast_to(bdw_ref[:, cs], (row_chunk, LANES))
            for kk in range(CONV_K):
                s = (base + kk) % SUBLANES
                a = r0 + base + kk - s
                src = zp_ref if s == 0 else zs_ref.at[s - 1]
                acc = acc + wdw_ref[kk:kk + 1, cs] * src[a:a + row_chunk, cs]
            conv_ref[r0:r0 + row_chunk, cs] = acc

    zc = conv_ref[...]
    mu = jnp.mean(zc, axis=-1, keepdims=True)
    zd = zc - mu
    var = jnp.mean(zd * zd, axis=-1, keepdims=True)
    zn = zd * lax.rsqrt(var + EPS_LN) * gln_ref[...] + bln_ref[...]
    zs = (zn * _sigmoid(zn)).astype(BF16)
    y_b = _dot(zs, wco_ref[...])
    gates = _sigmoid(gate_logits)
    merged = (gates[:, 0:D] * y_a + gates[:, D:2 * D] * y_b).astype(BF16)
    gate_m = mod_ref[:, 2 * D:3 * D]
    out_ref[...] = x_ref[...] + gate_m * _dot(merged, wout_ref[...])


def _mix(x, mod3, z, o, h, w_in, w_dw, b_dw, g_ln, b_ln, w_co, w_ao, w_out):
    B, S, D = x.shape
    ts = TS_MIX
    hb = ts // HALO
    n_halo = S // HALO
    tok = lambda w: pl.BlockSpec((None, ts, w), lambda b, i: (b, i, 0))
    return pl.pallas_call(
        _mix_kernel,
        out_shape=jax.ShapeDtypeStruct((B, S, D), F32),
        grid=(B, S // ts),
        in_specs=[
            tok(D),
            pl.BlockSpec((None, 1, N_MOD * D), lambda b, i: (b, 0, 0)),
            tok(CONV_CH),
            pl.BlockSpec((None, HALO, CONV_CH), lambda b, i: (b, jnp.maximum(i * hb - 1, 0), 0)),
            pl.BlockSpec((None, HALO, CONV_CH),
                         lambda b, i: (b, jnp.minimum((i + 1) * hb, n_halo - 1), 0)),
            tok(N_HEADS * V_HEAD),
            tok(D),
            _const_spec(w_in.shape),
            _const_spec(w_dw.shape),
            _const_spec(b_dw.shape),
            _const_spec(g_ln.shape),
            _const_spec(b_ln.shape),
            _const_spec(w_co.shape),
            _const_spec(w_ao.shape),
            _const_spec(w_out.shape),
        ],
        out_specs=tok(D),
        scratch_shapes=[
            pltpu.VMEM((ts + 2 * HALO, CONV_CH), F32),
            pltpu.VMEM((SUBLANES - 1, ts + 2 * HALO - SUBLANES, CONV_CH), F32),
            pltpu.VMEM((ts, CONV_CH), F32),
            pltpu.VMEM((D, 2 * D), BF16),
        ],
        compiler_params=pltpu.CompilerParams(
            dimension_semantics=("arbitrary", "arbitrary"), vmem_limit_bytes=VMEM_LIMIT),
        name="conv_merge_out",
    )(x, mod3, z, z, z, o, h, w_in, w_dw, b_dw, g_ln, b_ln, w_co, w_ao, w_out)


def _ffn_kernel(x_ref, mod_ref, gffn_ref, wg_ref, wu_ref, wd_ref, gfin_ref, out_ref, *, final_norm):
    D = D_MODEL
    shift = mod_ref[:, 3 * D:4 * D]
    scale = mod_ref[:, 4 * D:5 * D]
    gate = mod_ref[:, 5 * D:6 * D]
    rows = x_ref.shape[0] // ROW_SUBTILES
    for j in range(ROW_SUBTILES):
        rs = slice(j * rows, (j + 1) * rows)
        x = x_ref[rs, :]
        h = (_rms(x, gffn_ref[...]) * (1.0 + scale) + shift).astype(BF16)
        g = _dot(h, wg_ref[...])
        u = _dot(h, wu_ref[...])
        a = (g * _sigmoid(g) * u).astype(BF16)
        x2 = x + gate * _dot(a, wd_ref[...])
        out_ref[rs, :] = _rms(x2, gfin_ref[...]) if final_norm else x2


def _ffn(x, mod3, g_ffn, w_gate, w_up, w_down, g_final, final_norm):
    B, S, D = x.shape
    tm = TM_FFN
    tok = pl.BlockSpec((None, tm, D), lambda b, i: (b, i, 0))
    return pl.pallas_call(
        functools.partial(_ffn_kernel, final_norm=final_norm),
        out_shape=jax.ShapeDtypeStruct((B, S, D), F32),
        grid=(B, S // tm),
        in_specs=[
            tok,
            pl.BlockSpec((None, 1, N_MOD * D), lambda b, i: (b, 0, 0)),
            _const_spec(g_ffn.shape),
            _const_spec(w_gate.shape),
            _const_spec(w_up.shape),
            _const_spec(w_down.shape),
            _const_spec(g_final.shape),
        ],
        out_specs=tok,
        compiler_params=pltpu.CompilerParams(
            dimension_semantics=("parallel", "parallel"), vmem_limit_bytes=VMEM_LIMIT),
        name="swiglu_final_norm",
    )(x, mod3, g_ffn, w_gate, w_up, w_down, g_final)


def _prep_w_q(w_q_up):
    r = w_q_up.shape[0]
    w = w_q_up.reshape(r, N_HEADS, QK_HEAD)
    nope = w[..., :QK_NOPE]
    r1 = w[..., QK_NOPE:QK_NOPE + HALF_ROPE]
    r2 = w[..., QK_NOPE + HALF_ROPE:]
    return jnp.concatenate([nope, r1, r2, -r2, r1], axis=-1).reshape(r, N_HEADS * HEAD_PAD).astype(BF16)


def _prep_w_kv(w_kv_up):
    r = w_kv_up.shape[0]
    w = w_kv_up.reshape(r, N_HEADS, QK_NOPE + V_HEAD)
    k_nope = w[..., :QK_NOPE]
    v = w[..., QK_NOPE:]
    k_pad = jnp.concatenate(
        [k_nope, jnp.zeros((r, N_HEADS, HEAD_PAD - QK_NOPE), w.dtype)], axis=-1)
    wk = k_pad.reshape(r, N_HEADS * HEAD_PAD).astype(BF16)
    wvt = v.reshape(r, N_HEADS * V_HEAD).T.astype(BF16)
    return wk, wvt


def kernel(x, c, positions, w_ada, b_ada, g_norm_mix, w_in, g_q_a, w_q_up, g_kv_a, w_kv_up,
           w_attn_o, w_dw, b_dw, g_conv_ln, b_conv_ln, w_conv_out, w_out, g_norm_ffn,
           w_ffn_gate, w_ffn_up, w_ffn_down, g_final):
    B, S, D = x.shape
    depth = w_ada.shape[0]
    cos_c, sin_c = _rope_tables(positions)
    for l in range(depth):
        mod3 = _modulation(c, w_ada[l], b_ada[l][None, :]).reshape(B, 1, N_MOD * D)
        wk, wvt = _prep_w_kv(w_kv_up[l])
        w_in_t = w_in[l].T
        q, k, vt, z, h = _inproj(
            x, mod3, g_norm_mix[l][None, :], w_in_t, g_q_a[l][None, :],
            _prep_w_q(w_q_up[l]), g_kv_a[l][None, :], wk, wvt, cos_c, sin_c)
        o = _attention(q, k, vt)
        x = _mix(x, mod3, z, o, h, w_in_t, w_dw[l], b_dw[l][None, :], g_conv_ln[l][None, :],
                 b_conv_ln[l][None, :], w_conv_out[l].astype(BF16), w_attn_o[l].astype(BF16),
                 w_out[l].astype(BF16))
        x = _ffn(x, mod3, g_norm_ffn[l][None, :], w_ffn_gate[l].astype(BF16),
                 w_ffn_up[l].astype(BF16), w_ffn_down[l].astype(BF16), g_final[None, :],
                 final_norm=(l == depth - 1))
    return x
```

```python
import functools
import math

import jax
import jax.numpy as jnp
from jax import lax
from jax.experimental import pallas as pl
from jax.experimental.pallas import tpu as pltpu

F32 = jnp.float32
BF16 = jnp.bfloat16

D_MODEL = 1024
N_HEADS = 8
Q_LORA = 256
KV_LORA = 128
QK_NOPE = 64
QK_ROPE = 32
HALF_ROPE = QK_ROPE // 2
V_HEAD = 64
QK_HEAD = QK_NOPE + QK_ROPE
ATTN_SCALE = 1.0 / math.sqrt(QK_HEAD)
Q_SCALE = ATTN_SCALE * math.log2(math.e)
ROPE_THETA = 10000.0
CONV_CH = 512
CONV_K = 31
N_MOD = 6
EPS_RMS = 1e-6
EPS_LN = 1e-5

LANES = 128
SUBLANES = 8
HEAD_PAD = LANES
VMEM_LIMIT = 56 * 1024 * 1024

C_QA = 0
C_KVA = C_QA + Q_LORA
C_KR = C_KVA + KV_LORA
C_CONV = C_KR + HEAD_PAD

TS_IN = 512
TQ = 512
KEY_CHUNK = 1024
TS_MIX = 512
HALO = 16
TM_FFN = 512
ROW_SUBTILES = 4


def _sigmoid(x):
    return 1.0 / (1.0 + jnp.exp(-x))


def _rms(x, g):
    return x * lax.rsqrt(jnp.mean(x * x, axis=-1, keepdims=True) + EPS_RMS) * g


def _dot(a, b):
    return jnp.dot(a, b, preferred_element_type=F32)


def _const_spec(shape):
    nd = len(shape)
    return pl.BlockSpec(shape, lambda *_: (0,) * nd, pipeline_mode=pl.Buffered(1))


def _rope_kernel(pos_ref, freq_ref, cos_ref, sin_ref):
    ang = pos_ref[...] * freq_ref[...]
    cos_ref[...] = jnp.cos(ang)
    sin_ref[...] = jnp.sin(ang)


def _rope_tables(positions):
    B, S = positions.shape
    inv_freq = ROPE_THETA ** (-jnp.arange(0, QK_ROPE, 2, dtype=F32) / QK_ROPE)
    rows = B * S * HALF_ROPE // LANES
    pos_rep = jnp.repeat(positions.reshape(-1).astype(F32), HALF_ROPE).reshape(rows, LANES)
    freq = jnp.tile(inv_freq, LANES // HALF_ROPE)[None, :]
    cos, sin = pl.pallas_call(
        _rope_kernel,
        out_shape=(jax.ShapeDtypeStruct((rows, LANES), F32),) * 2,
        name="rope_tables",
    )(pos_rep, freq)
    return cos, sin


def _expand_rope_table(compact_ref, out_ref, nope_value):
    pos_per_row = LANES // HALF_ROPE
    n = compact_ref.shape[0]
    comp = compact_ref[...]
    lane = lax.broadcasted_iota(jnp.int32, (n, LANES), 1)
    for j in range(pos_per_row):
        shift = (QK_NOPE - HALF_ROPE * j) % LANES
        first = pltpu.roll(comp, shift, 1) if shift else comp
        second = pltpu.roll(first, HALF_ROPE, 1)
        out_ref[pl.ds(j, n, stride=pos_per_row), :] = jnp.where(
            lane < QK_NOPE, nope_value,
            jnp.where(lane < QK_NOPE + HALF_ROPE, first,
                      jnp.where(lane < QK_HEAD, second, 0.0)))


def _rope_tile(t, ctab, stab):
    swapped = pltpu.roll(t, HEAD_PAD - QK_ROPE, 1)
    return t * ctab + swapped * stab


def _mod_kernel(c_ref, w_ref, b_ref, o_ref):
    c = c_ref[...]
    c_act = (c * _sigmoid(c)).astype(BF16)
    o_ref[...] = _dot(c_act, w_ref[...].astype(BF16)) + b_ref[...]


def _modulation(c, w_ada, b_ada):
    B, D = c.shape
    n = w_ada.shape[1]
    bn = 768
    return pl.pallas_call(
        _mod_kernel,
        out_shape=jax.ShapeDtypeStruct((B, n), F32),
        grid=(n // bn,),
        in_specs=[
            pl.BlockSpec((B, D), lambda j: (0, 0)),
            pl.BlockSpec((D, bn), lambda j: (0, j)),
            pl.BlockSpec((1, bn), lambda j: (0, j)),
        ],
        out_specs=pl.BlockSpec((B, bn), lambda j: (0, j)),
        compiler_params=pltpu.CompilerParams(
            dimension_semantics=("parallel",), vmem_limit_bytes=VMEM_LIMIT),
        name="adaln_modulation",
    )(c, w_ada, b_ada)


def _mixer_input(x, mod_ref, gmix_ref):
    D = D_MODEL
    shift = mod_ref[:, 0:D]
    scale = mod_ref[:, D:2 * D]
    return (_rms(x, gmix_ref[...]) * (1.0 + scale) + shift).astype(BF16)


def _first_grid_step():
    return jnp.logical_and(pl.program_id(0) == 0, pl.program_id(1) == 0)


def _transpose_rows(wt_ref, row0, n_rows, out_ref, col0):
    for j in range(n_rows // LANES):
        rows = wt_ref[row0 + j * LANES:row0 + (j + 1) * LANES, :]
        out_ref[:, col0 + j * LANES:col0 + (j + 1) * LANES] = rows.T.astype(BF16)


def _relayout_w_proj(wt_ref, out_ref):
    o_kr = Q_LORA + KV_LORA
    o_conv = o_kr + QK_ROPE
    _transpose_rows(wt_ref, 0, o_kr, out_ref, C_QA)
    k1 = wt_ref[o_kr:o_kr + HALF_ROPE, :]
    k2 = wt_ref[o_kr + HALF_ROPE:o_conv, :]
    key_tile_t = jnp.concatenate([jnp.zeros((QK_NOPE, k1.shape[1]), F32), k1, k2, -k2, k1], axis=0)
    out_ref[:, C_KR:C_KR + HEAD_PAD] = key_tile_t.T.astype(BF16)
    _transpose_rows(wt_ref, o_conv, 2 * CONV_CH, out_ref, C_CONV)


def _inproj_kernel(x_ref, mod_ref, gmix_ref, w_ref, gq_ref, wq_ref, gkv_ref, wk_ref, wvt_ref,
                   cosc_ref, sinc_ref,
                   q_ref, k_ref, vt_ref, z_ref, h_ref, ctab_ref, stab_ref, win_ref):
    pl.when(_first_grid_step())(lambda: _relayout_w_proj(w_ref, win_ref))
    h = _mixer_input(x_ref[...], mod_ref, gmix_ref)
    h_ref[...] = h
    _expand_rope_table(cosc_ref, ctab_ref, 1.0)
    _expand_rope_table(sinc_ref, stab_ref, 0.0)
    ctab = ctab_ref[...]
    stab = stab_ref[...]

    qa = _dot(h, win_ref[:, C_QA:C_QA + Q_LORA])
    qn = _rms(qa, gq_ref[...]).astype(BF16)
    q_all = _dot(qn, wq_ref[...])
    for hd in range(N_HEADS):
        sl = slice(hd * HEAD_PAD, (hd + 1) * HEAD_PAD)
        q_ref[:, sl] = (_rope_tile(q_all[:, sl], ctab, stab) * Q_SCALE).astype(BF16)

    kva = _dot(h, win_ref[:, C_KVA:C_KVA + KV_LORA])
    kvn = _rms(kva, gkv_ref[...]).astype(BF16)
    k_pad = _dot(kvn, wk_ref[...])
    vt_ref[...] = lax.dot_general(wvt_ref[...], kvn, (((1,), (1,)), ((), ())),
                                  preferred_element_type=F32).astype(BF16)
    kr_rot = _rope_tile(_dot(h, win_ref[:, C_KR:C_KR + HEAD_PAD]), ctab, stab)
    for hd in range(N_HEADS):
        sl = slice(hd * HEAD_PAD, (hd + 1) * HEAD_PAD)
        k_ref[:, sl] = (k_pad[:, sl] + kr_rot).astype(BF16)

    cu = _dot(h, win_ref[:, C_CONV:C_CONV + 2 * CONV_CH])
    z_ref[...] = cu[:, 0:CONV_CH] * _sigmoid(cu[:, CONV_CH:2 * CONV_CH])


def _inproj(x, mod3, g_mix, w_in, g_q, wq2, g_kv, wk, wvt, cos_c, sin_c):
    B, S, D = x.shape
    ts = TS_IN
    nq = N_HEADS * HEAD_PAD
    nv = N_HEADS * V_HEAD
    tok = lambda w: pl.BlockSpec((None, ts, w), lambda b, i: (b, i, 0))
    rows = ts * HALF_ROPE // LANES
    n_i = S // ts
    compact = pl.BlockSpec((rows, LANES), lambda b, i: (b * n_i + i, 0))
    return pl.pallas_call(
        _inproj_kernel,
        out_shape=(
            jax.ShapeDtypeStruct((B, S, nq), BF16),
            jax.ShapeDtypeStruct((B, S, nq), BF16),
            jax.ShapeDtypeStruct((B, nv, S), BF16),
            jax.ShapeDtypeStruct((B, S, CONV_CH), F32),
            jax.ShapeDtypeStruct((B, S, D), BF16),
        ),
        grid=(B, S // ts),
        in_specs=[
            tok(D),
            pl.BlockSpec((None, 1, N_MOD * D), lambda b, i: (b, 0, 0)),
            _const_spec(g_mix.shape),
            _const_spec(w_in.shape),
            _const_spec(g_q.shape),
            _const_spec(wq2.shape),
            _const_spec(g_kv.shape),
            _const_spec(wk.shape),
            _const_spec(wvt.shape),
            compact,
            compact,
        ],
        out_specs=(tok(nq), tok(nq), pl.BlockSpec((None, nv, ts), lambda b, i: (b, 0, i)),
                   tok(CONV_CH), tok(D)),
        scratch_shapes=[pltpu.VMEM((ts, HEAD_PAD), F32), pltpu.VMEM((ts, HEAD_PAD), F32),
                        pltpu.VMEM((D, C_CONV + 2 * CONV_CH), BF16)],
        compiler_params=pltpu.CompilerParams(
            dimension_semantics=("arbitrary", "arbitrary"), vmem_limit_bytes=VMEM_LIMIT),
        name="input_projection",
    )(x, mod3, g_mix, w_in, g_q, wq2, g_kv, wk, wvt, cos_c, sin_c)


def _attn_kernel(q_ref, k_ref, vt_ref, o_ref):
    n_kc = k_ref.shape[0] // KEY_CHUNK

    def score_chunk(hd, c):
        sl = slice(hd * HEAD_PAD, (hd + 1) * HEAD_PAD)
        ks = slice(c * KEY_CHUNK, (c + 1) * KEY_CHUNK)
        return lax.dot_general(k_ref[ks, sl], q_ref[:, sl], (((1,), (1,)), ((), ())),
                               preferred_element_type=F32)

    def col_max(chunks):
        m = jnp.max(chunks[0], axis=0, keepdims=True)
        for st in chunks[1:]:
            m = jnp.maximum(m, jnp.max(st, axis=0, keepdims=True))
        return m

    st_next = [score_chunk(0, c) for c in range(n_kc)]
    outs = []
    for hd in range(N_HEADS):
        st_cur, st_next = st_next, []
        m = col_max(st_cur)
        l = acc = None
        for c in range(n_kc):
            if hd + 1 < N_HEADS:
                st_next.append(score_chunk(hd + 1, c))
            ks = slice(c * KEY_CHUNK, (c + 1) * KEY_CHUNK)
            p = jnp.exp2(st_cur[c] - m)
            ls = jnp.sum(p, axis=0, keepdims=True)
            pv = _dot(vt_ref[hd * V_HEAD:(hd + 1) * V_HEAD, ks], p.astype(BF16))
            l, acc = (ls, pv) if c == 0 else (l + ls, acc + pv)
        outs.append(acc / l)
        if hd % 2 == 1:
            o_pair = jnp.concatenate(outs, axis=0).T
            o_ref[:, (hd // 2) * 2 * V_HEAD:(hd // 2 + 1) * 2 * V_HEAD] = o_pair.astype(BF16)
            outs = []


def _attention(q, k, vt):
    B, S, nq = q.shape
    nv = N_HEADS * V_HEAD
    return pl.pallas_call(
        _attn_kernel,
        out_shape=jax.ShapeDtypeStruct((B, S, nv), BF16),
        grid=(B, S // TQ),
        in_specs=[
            pl.BlockSpec((None, TQ, nq), lambda b, i: (b, i, 0)),
            pl.BlockSpec((None, S, nq), lambda b, i: (b, 0, 0)),
            pl.BlockSpec((None, nv, S), lambda b, i: (b, 0, 0)),
        ],
        out_specs=pl.BlockSpec((None, TQ, nv), lambda b, i: (b, i, 0)),
        compiler_params=pltpu.CompilerParams(
            dimension_semantics=("parallel", "parallel"), vmem_limit_bytes=VMEM_LIMIT),
        name="mla_attention",
    )(q, k, vt)


def _relayout_w_gate(wt_ref, out_ref):
    o_gate = Q_LORA + KV_LORA + QK_ROPE + 2 * CONV_CH
    _transpose_rows(wt_ref, o_gate, out_ref.shape[1], out_ref, 0)


def _mix_kernel(x_ref, mod_ref, z_ref, zprev_ref, znext_ref, o_ref, h_ref, w_ref,
                wdw_ref, bdw_ref, gln_ref, bln_ref, wco_ref, wao_ref, wout_ref,
                out_ref, zp_ref, zs_ref, conv_ref, wgate_ref):
    D = D_MODEL
    ts = TS_MIX
    i = pl.program_id(1)
    n_i = pl.num_programs(1)
    pl.when(_first_grid_step())(lambda: _relayout_w_gate(w_ref, wgate_ref))

    zp_ref[0:HALO, :] = jnp.where(i > 0, zprev_ref[...], 0.0)
    zp_ref[HALO:HALO + ts, :] = z_ref[...]
    zp_ref[HALO + ts:2 * HALO + ts, :] = jnp.where(i < n_i - 1, znext_ref[...], 0.0)

    n_shift = zs_ref.shape[1]
    for s in range(1, SUBLANES):
        zs_ref[s - 1] = zp_ref[s:s + n_shift, :]

    gate_logits = _dot(h_ref[...], wgate_ref[...])
    y_a = _dot(o_ref[...], wao_ref[...])

    row_chunk = 64
    base = HALO - CONV_K // 2
    for cb in range(CONV_CH // LANES):
        cs = slice(cb * LANES, (cb + 1) * LANES)
        for rb in range(ts // row_chunk):
            r0 = rb * row_chunk
            acc = jnp.broadcast_to(bdw_ref[:, cs], (row_chunk, LANES))
            for kk in range(CONV_K):
                s = (base + kk) % SUBLANES
                a = r0 + base + kk - s
                src = zp_ref if s == 0 else zs_ref.at[s - 1]
                acc = acc + wdw_ref[kk:kk + 1, cs] * src[a:a + row_chunk, cs]
            conv_ref[r0:r0 + row_chunk, cs] = acc

    zc = conv_ref[...]
    mu = jnp.mean(zc, axis=-1, keepdims=True)
    zd = zc - mu
    var = jnp.mean(zd * zd, axis=-1, keepdims=True)
    zn = zd * lax.rsqrt(var + EPS_LN) * gln_ref[...] + bln_ref[...]
    zs = (zn * _sigmoid(zn)).astype(BF16)
    y_b = _dot(zs, wco_ref[...])
    gates = _sigmoid(gate_logits)
    merged = (gates[:, 0:D] * y_a + gates[:, D:2 * D] * y_b).astype(BF16)
    gate_m = mod_ref[:, 2 * D:3 * D]
    out_ref[...] = x_ref[...] + gate_m * _dot(merged, wout_ref[...])


def _mix(x, mod3, z, o, h, w_in, w_dw, b_dw, g_ln, b_ln, w_co, w_ao, w_out):
    B, S, D = x.shape
    ts = TS_MIX
    hb = ts // HALO
    n_halo = S // HALO
    tok = lambda w: pl.BlockSpec((None, ts, w), lambda b, i: (b, i, 0))
    return pl.pallas_call(
        _mix_kernel,
        out_shape=jax.ShapeDtypeStruct((B, S, D), F32),
        grid=(B, S // ts),
        in_specs=[
            tok(D),
            pl.BlockSpec((None, 1, N_MOD * D), lambda b, i: (b, 0, 0)),
            tok(CONV_CH),
            pl.BlockSpec((None, HALO, CONV_CH), lambda b, i: (b, jnp.maximum(i * hb - 1, 0), 0)),
            pl.BlockSpec((None, HALO, CONV_CH),
                         lambda b, i: (b, jnp.minimum((i + 1) * hb, n_halo - 1), 0)),
            tok(N_HEADS * V_HEAD),
            tok(D),
            _const_spec(w_in.shape),
            _const_spec(w_dw.shape),
            _const_spec(b_dw.shape),
            _const_spec(g_ln.shape),
            _const_spec(b_ln.shape),
            _const_spec(w_co.shape),
            _const_spec(w_ao.shape),
            _const_spec(w_out.shape),
        ],
        out_specs=tok(D),
        scratch_shapes=[
            pltpu.VMEM((ts + 2 * HALO, CONV_CH), F32),
            pltpu.VMEM((SUBLANES - 1, ts + 2 * HALO - SUBLANES, CONV_CH), F32),
            pltpu.VMEM((ts, CONV_CH), F32),
            pltpu.VMEM((D, 2 * D), BF16),
        ],
        compiler_params=pltpu.CompilerParams(
            dimension_semantics=("arbitrary", "arbitrary"), vmem_limit_bytes=VMEM_LIMIT),
        name="conv_merge_out",
    )(x, mod3, z, z, z, o, h, w_in, w_dw, b_dw, g_ln, b_ln, w_co, w_ao, w_out)


def _ffn_kernel(x_ref, mod_ref, gffn_ref, wg_ref, wu_ref, wd_ref, gfin_ref, out_ref, *, final_norm):
    D = D_MODEL
    shift = mod_ref[:, 3 * D:4 * D]
    scale = mod_ref[:, 4 * D:5 * D]
    gate = mod_ref[:, 5 * D:6 * D]
    rows = x_ref.shape[0] // ROW_SUBTILES
    for j in range(ROW_SUBTILES):
        rs = slice(j * rows, (j + 1) * rows)
        x = x_ref[rs, :]
        h = (_rms(x, gffn_ref[...]) * (1.0 + scale) + shift).astype(BF16)
        g = _dot(h, wg_ref[...])
        u = _dot(h, wu_ref[...])
        a = (g * _sigmoid(g) * u).astype(BF16)
        x2 = x + gate * _dot(a, wd_ref[...])
        out_ref[rs, :] = _rms(x2, gfin_ref[...]) if final_norm else x2


def _ffn(x, mod3, g_ffn, w_gate, w_up, w_down, g_final, final_norm):
    B, S, D = x.shape
    tm = TM_FFN
    tok = pl.BlockSpec((None, tm, D), lambda b, i: (b, i, 0))
    return pl.pallas_call(
        functools.partial(_ffn_kernel, final_norm=final_norm),
        out_shape=jax.ShapeDtypeStruct((B, S, D), F32),
        grid=(B, S // tm),
        in_specs=[
            tok,
            pl.BlockSpec((None, 1, N_MOD * D), lambda b, i: (b, 0, 0)),
            _const_spec(g_ffn.shape),
            _const_spec(w_gate.shape),
            _const_spec(w_up.shape),
            _const_spec(w_down.shape),
            _const_spec(g_final.shape),
        ],
        out_specs=tok,
        compiler_params=pltpu.CompilerParams(
            dimension_semantics=("parallel", "parallel"), vmem_limit_bytes=VMEM_LIMIT),
        name="swiglu_final_norm",
    )(x, mod3, g_ffn, w_gate, w_up, w_down, g_final)


def _prep_w_q(w_q_up):
    r = w_q_up.shape[0]
    w = w_q_up.reshape(r, N_HEADS, QK_HEAD)
    nope = w[..., :QK_NOPE]
    r1 = w[..., QK_NOPE:QK_NOPE + HALF_ROPE]
    r2 = w[..., QK_NOPE + HALF_ROPE:]
    return jnp.concatenate([nope, r1, r2, -r2, r1], axis=-1).reshape(r, N_HEADS * HEAD_PAD).astype(BF16)


def _prep_w_kv(w_kv_up):
    r = w_kv_up.shape[0]
    w = w_kv_up.reshape(r, N_HEADS, QK_NOPE + V_HEAD)
    k_nope = w[..., :QK_NOPE]
    v = w[..., QK_NOPE:]
    k_pad = jnp.concatenate(
        [k_nope, jnp.zeros((r, N_HEADS, HEAD_PAD - QK_NOPE), w.dtype)], axis=-1)
    wk = k_pad.reshape(r, N_HEADS * HEAD_PAD).astype(BF16)
    wvt = v.reshape(r, N_HEADS * V_HEAD).T.astype(BF16)
    return wk, wvt


def kernel(x, c, positions, w_ada, b_ada, g_norm_mix, w_in, g_q_a, w_q_up, g_kv_a, w_kv_up,
           w_attn_o, w_dw, b_dw, g_conv_ln, b_conv_ln, w_conv_out, w_out, g_norm_ffn,
           w_ffn_gate, w_ffn_up, w_ffn_down, g_final):
    B, S, D = x.shape
    depth = w_ada.shape[0]
    cos_c, sin_c = _rope_tables(positions)
    for l in range(depth):
        mod3 = _modulation(c, w_ada[l], b_ada[l][None, :]).reshape(B, 1, N_MOD * D)
        wk, wvt = _prep_w_kv(w_kv_up[l])
        w_in_t = w_in[l].T
        q, k, vt, z, h = _inproj(
            x, mod3, g_norm_mix[l][None, :], w_in_t, g_q_a[l][None, :],
            _prep_w_q(w_q_up[l]), g_kv_a[l][None, :], wk, wvt, cos_c, sin_c)
        o = _attention(q, k, vt)
        x = _mix(x, mod3, z, o, h, w_in_t, w_dw[l], b_dw[l][None, :], g_conv_ln[l][None, :],
                 b_conv_ln[l][None, :], w_conv_out[l].astype(BF16), w_attn_o[l].astype(BF16),
                 w_out[l].astype(BF16))
        x = _ffn(x, mod3, g_norm_ffn[l][None, :], w_ffn_gate[l].astype(BF16),
                 w_ffn_up[l].astype(BF16), w_ffn_down[l].astype(BF16), g_final[None, :],
                 final_norm=(l == depth - 1))
    return x
```

```python
import functools
import math

import jax
import jax.numpy as jnp
from jax import lax
from jax.experimental import pallas as pl
from jax.experimental.pallas import tpu as pltpu

F32 = jnp.float32
BF16 = jnp.bfloat16

D_MODEL = 1024
N_HEADS = 8
Q_LORA = 256
KV_LORA = 128
QK_NOPE = 64
QK_ROPE = 32
HALF_ROPE = QK_ROPE // 2
V_HEAD = 64
QK_HEAD = QK_NOPE + QK_ROPE
ATTN_SCALE = 1.0 / math.sqrt(QK_HEAD)
Q_SCALE = ATTN_SCALE * math.log2(math.e)
ROPE_THETA = 10000.0
CONV_CH = 512
CONV_K = 31
N_MOD = 6
EPS_RMS = 1e-6
EPS_LN = 1e-5

LANES = 128
SUBLANES = 8
HEAD_PAD = LANES
VMEM_LIMIT = 56 * 1024 * 1024

C_QA = 0
C_KVA = C_QA + Q_LORA
C_KR = C_KVA + KV_LORA
C_CONV = C_KR + HEAD_PAD

TS_IN = 512
TQ = 512
KEY_CHUNK = 1024
TS_MIX = 512
HALO = 16
TM_FFN = 512
ROW_SUBTILES = 2


def _sigmoid(x):
    return 1.0 / (1.0 + jnp.exp(-x))


def _rms(x, g):
    return x * lax.rsqrt(jnp.mean(x * x, axis=-1, keepdims=True) + EPS_RMS) * g


def _dot(a, b):
    return jnp.dot(a, b, preferred_element_type=F32)


def _const_spec(shape):
    nd = len(shape)
    return pl.BlockSpec(shape, lambda *_: (0,) * nd, pipeline_mode=pl.Buffered(1))


def _rope_kernel(pos_ref, freq_ref, cos_ref, sin_ref):
    ang = pos_ref[...] * freq_ref[...]
    cos_ref[...] = jnp.cos(ang)
    sin_ref[...] = jnp.sin(ang)


def _rope_tables(positions):
    B, S = positions.shape
    inv_freq = ROPE_THETA ** (-jnp.arange(0, QK_ROPE, 2, dtype=F32) / QK_ROPE)
    rows = B * S * HALF_ROPE // LANES
    pos_rep = jnp.repeat(positions.reshape(-1).astype(F32), HALF_ROPE).reshape(rows, LANES)
    freq = jnp.tile(inv_freq, LANES // HALF_ROPE)[None, :]
    cos, sin = pl.pallas_call(
        _rope_kernel,
        out_shape=(jax.ShapeDtypeStruct((rows, LANES), F32),) * 2,
        name="rope_tables",
    )(pos_rep, freq)
    return cos, sin


def _expand_rope_table(compact_ref, out_ref, nope_value):
    pos_per_row = LANES // HALF_ROPE
    n = compact_ref.shape[0]
    comp = compact_ref[...]
    lane = lax.broadcasted_iota(jnp.int32, (n, LANES), 1)
    for j in range(pos_per_row):
        shift = (QK_NOPE - HALF_ROPE * j) % LANES
        first = pltpu.roll(comp, shift, 1) if shift else comp
        second = pltpu.roll(first, HALF_ROPE, 1)
        out_ref[pl.ds(j, n, stride=pos_per_row), :] = jnp.where(
            lane < QK_NOPE, nope_value,
            jnp.where(lane < QK_NOPE + HALF_ROPE, first,
                      jnp.where(lane < QK_HEAD, second, 0.0)))


def _rope_tile(t, ctab, stab):
    swapped = pltpu.roll(t, HEAD_PAD - QK_ROPE, 1)
    return t * ctab + swapped * stab


def _mod_kernel(c_ref, w_ref, b_ref, o_ref):
    c = c_ref[...]
    c_act = (c * _sigmoid(c)).astype(BF16)
    o_ref[...] = _dot(c_act, w_ref[...].astype(BF16)) + b_ref[...]


def _modulation(c, w_ada, b_ada):
    B, D = c.shape
    n = w_ada.shape[1]
    bn = 1536
    return pl.pallas_call(
        _mod_kernel,
        out_shape=jax.ShapeDtypeStruct((B, n), F32),
        grid=(n // bn,),
        in_specs=[
            pl.BlockSpec((B, D), lambda j: (0, 0)),
            pl.BlockSpec((D, bn), lambda j: (0, j)),
            pl.BlockSpec((1, bn), lambda j: (0, j)),
        ],
        out_specs=pl.BlockSpec((B, bn), lambda j: (0, j)),
        compiler_params=pltpu.CompilerParams(
            dimension_semantics=("parallel",), vmem_limit_bytes=VMEM_LIMIT),
        name="adaln_modulation",
    )(c, w_ada, b_ada)


def _mixer_input(x, mod_ref, gmix_ref):
    D = D_MODEL
    shift = mod_ref[:, 0:D]
    scale = mod_ref[:, D:2 * D]
    return (_rms(x, gmix_ref[...]) * (1.0 + scale) + shift).astype(BF16)


def _first_grid_step():
    return jnp.logical_and(pl.program_id(0) == 0, pl.program_id(1) == 0)


def _transpose_rows(wt_ref, row0, n_rows, out_ref, col0):
    for j in range(n_rows // LANES):
        rows = wt_ref[row0 + j * LANES:row0 + (j + 1) * LANES, :]
        out_ref[:, col0 + j * LANES:col0 + (j + 1) * LANES] = rows.T.astype(BF16)


def _relayout_w_proj(wt_ref, out_ref):
    o_kr = Q_LORA + KV_LORA
    o_conv = o_kr + QK_ROPE
    _transpose_rows(wt_ref, 0, o_kr, out_ref, C_QA)
    k1 = wt_ref[o_kr:o_kr + HALF_ROPE, :]
    k2 = wt_ref[o_kr + HALF_ROPE:o_conv, :]
    key_tile_t = jnp.concatenate([jnp.zeros((QK_NOPE, k1.shape[1]), F32), k1, k2, -k2, k1], axis=0)
    out_ref[:, C_KR:C_KR + HEAD_PAD] = key_tile_t.T.astype(BF16)
    _transpose_rows(wt_ref, o_conv, 2 * CONV_CH, out_ref, C_CONV)


def _inproj_kernel(x_ref, mod_ref, gmix_ref, w_ref, gq_ref, wq_ref, gkv_ref, wk_ref, wvt_ref,
                   cosc_ref, sinc_ref,
                   q_ref, k_ref, vt_ref, z_ref, h_ref, ctab_ref, stab_ref, win_ref):
    pl.when(_first_grid_step())(lambda: _relayout_w_proj(w_ref, win_ref))
    h = _mixer_input(x_ref[...], mod_ref, gmix_ref)
    h_ref[...] = h
    _expand_rope_table(cosc_ref, ctab_ref, 1.0)
    _expand_rope_table(sinc_ref, stab_ref, 0.0)
    ctab = ctab_ref[...]
    stab = stab_ref[...]

    proj = _dot(h, win_ref[...])

    z_ref[...] = (proj[:, C_CONV:C_CONV + CONV_CH]
                  * _sigmoid(proj[:, C_CONV + CONV_CH:C_CONV + 2 * CONV_CH]))

    qn = _rms(proj[:, C_QA:C_QA + Q_LORA], gq_ref[...]).astype(BF16)
    q_all = _dot(qn, wq_ref[...])
    for hd in range(N_HEADS):
        sl = slice(hd * HEAD_PAD, (hd + 1) * HEAD_PAD)
        q_ref[:, sl] = (_rope_tile(q_all[:, sl], ctab, stab) * Q_SCALE).astype(BF16)

    kvn = _rms(proj[:, C_KVA:C_KVA + KV_LORA], gkv_ref[...]).astype(BF16)
    k_pad = _dot(kvn, wk_ref[...])
    vt_ref[...] = lax.dot_general(wvt_ref[...], kvn, (((1,), (1,)), ((), ())),
                                  preferred_element_type=F32).astype(BF16)
    kr_rot = _rope_tile(proj[:, C_KR:C_KR + HEAD_PAD], ctab, stab)
    for hd in range(N_HEADS):
        sl = slice(hd * HEAD_PAD, (hd + 1) * HEAD_PAD)
        k_ref[:, sl] = (k_pad[:, sl] + kr_rot).astype(BF16)


def _inproj(x, mod3, g_mix, w_in, g_q, wq2, g_kv, wk, wvt, cos_c, sin_c):
    B, S, D = x.shape
    ts = TS_IN
    nq = N_HEADS * HEAD_PAD
    nv = N_HEADS * V_HEAD
    tok = lambda w: pl.BlockSpec((None, ts, w), lambda b, i: (b, i, 0))
    rows = ts * HALF_ROPE // LANES
    n_i = S // ts
    compact = pl.BlockSpec((rows, LANES), lambda b, i: (b * n_i + i, 0))
    return pl.pallas_call(
        _inproj_kernel,
        out_shape=(
            jax.ShapeDtypeStruct((B, S, nq), BF16),
            jax.ShapeDtypeStruct((B, S, nq), BF16),
            jax.ShapeDtypeStruct((B, nv, S), BF16),
            jax.ShapeDtypeStruct((B, S, CONV_CH), F32),
            jax.ShapeDtypeStruct((B, S, D), BF16),
        ),
        grid=(B, S // ts),
        in_specs=[
            tok(D),
            pl.BlockSpec((None, 1, N_MOD * D), lambda b, i: (b, 0, 0)),
            _const_spec(g_mix.shape),
            _const_spec(w_in.shape),
            _const_spec(g_q.shape),
            _const_spec(wq2.shape),
            _const_spec(g_kv.shape),
            _const_spec(wk.shape),
            _const_spec(wvt.shape),
            compact,
            compact,
        ],
        out_specs=(tok(nq), tok(nq), pl.BlockSpec((None, nv, ts), lambda b, i: (b, 0, i)),
                   tok(CONV_CH), tok(D)),
        scratch_shapes=[pltpu.VMEM((ts, HEAD_PAD), F32), pltpu.VMEM((ts, HEAD_PAD), F32),
                        pltpu.VMEM((D, C_CONV + 2 * CONV_CH), BF16)],
        compiler_params=pltpu.CompilerParams(
            dimension_semantics=("arbitrary", "arbitrary"), vmem_limit_bytes=VMEM_LIMIT),
        name="input_projection",
    )(x, mod3, g_mix, w_in, g_q, wq2, g_kv, wk, wvt, cos_c, sin_c)


def _attn_kernel(q_ref, k_ref, vt_ref, o_ref):
    n_kc = k_ref.shape[0] // KEY_CHUNK

    def score_chunk(hd, c):
        sl = slice(hd * HEAD_PAD, (hd + 1) * HEAD_PAD)
        ks = slice(c * KEY_CHUNK, (c + 1) * KEY_CHUNK)
        return lax.dot_general(k_ref[ks, sl], q_ref[:, sl], (((1,), (1,)), ((), ())),
                               preferred_element_type=F32)

    def col_max(chunks):
        m = jnp.max(chunks[0], axis=0, keepdims=True)
        for st in chunks[1:]:
            m = jnp.maximum(m, jnp.max(st, axis=0, keepdims=True))
        return m

    st_next = [score_chunk(0, c) for c in range(n_kc)]
    outs = []
    for hd in range(N_HEADS):
        st_cur, st_next = st_next, []
        m = col_max(st_cur)
        l = acc = None
        for c in range(n_kc):
            if hd + 1 < N_HEADS:
                st_next.append(score_chunk(hd + 1, c))
            ks = slice(c * KEY_CHUNK, (c + 1) * KEY_CHUNK)
            p = jnp.exp2(st_cur[c] - m)
            ls = jnp.sum(p, axis=0, keepdims=True)
            pv = _dot(vt_ref[hd * V_HEAD:(hd + 1) * V_HEAD, ks], p.astype(BF16))
            l, acc = (ls, pv) if c == 0 else (l + ls, acc + pv)
        outs.append(acc / l)
        if hd % 2 == 1:
            o_pair = jnp.concatenate(outs, axis=0).T
            o_ref[:, (hd // 2) * 2 * V_HEAD:(hd // 2 + 1) * 2 * V_HEAD] = o_pair.astype(BF16)
            outs = []


def _attention(q, k, vt):
    B, S, nq = q.shape
    nv = N_HEADS * V_HEAD
    return pl.pallas_call(
        _attn_kernel,
        out_shape=jax.ShapeDtypeStruct((B, S, nv), BF16),
        grid=(B, S // TQ),
        in_specs=[
            pl.BlockSpec((None, TQ, nq), lambda b, i: (b, i, 0)),
            pl.BlockSpec((None, S, nq), lambda b, i: (b, 0, 0)),
            pl.BlockSpec((None, nv, S), lambda b, i: (b, 0, 0)),
        ],
        out_specs=pl.BlockSpec((None, TQ, nv), lambda b, i: (b, i, 0)),
        compiler_params=pltpu.CompilerParams(
            dimension_semantics=("parallel", "parallel"), vmem_limit_bytes=VMEM_LIMIT),
        name="mla_attention",
    )(q, k, vt)


def _relayout_w_gate(wt_ref, out_ref):
    o_gate = Q_LORA + KV_LORA + QK_ROPE + 2 * CONV_CH
    _transpose_rows(wt_ref, o_gate, out_ref.shape[1], out_ref, 0)


def _mix_kernel(x_ref, mod_ref, z_ref, zprev_ref, znext_ref, o_ref, h_ref, w_ref,
                wdw_ref, bdw_ref, gln_ref, bln_ref, wco_ref, wao_ref, wout_ref, *rest, n_cast):
    cast_in = rest[:n_cast]
    out_ref = rest[n_cast]
    cast_out = rest[n_cast + 1:2 * n_cast + 1]
    zp_ref, zs_ref, conv_ref, wgate_ref = rest[2 * n_cast + 1:]
    for src, dst in zip(cast_in, cast_out):
        dst[...] = src[...].astype(BF16)
    D = D_MODEL
    ts = TS_MIX
    i = pl.program_id(1)
    n_i = pl.num_programs(1)
    pl.when(_first_grid_step())(lambda: _relayout_w_gate(w_ref, wgate_ref))

    zp_ref[0:HALO, :] = jnp.where(i > 0, zprev_ref[...], 0.0)
    zp_ref[HALO:HALO + ts, :] = z_ref[...]
    zp_ref[HALO + ts:2 * HALO + ts, :] = jnp.where(i < n_i - 1, znext_ref[...], 0.0)

    n_shift = zs_ref.shape[1]
    for s in range(1, SUBLANES):
        zs_ref[s - 1] = zp_ref[s:s + n_shift, :]

    gate_logits = _dot(h_ref[...], wgate_ref[...])
    y_a = _dot(o_ref[...], wao_ref[...])

    row_chunk = 64
    base = HALO - CONV_K // 2
    for cb in range(CONV_CH // LANES):
        cs = slice(cb * LANES, (cb + 1) * LANES)
        for rb in range(ts // row_chunk):
            r0 = rb * row_chunk
            acc = jnp.broadcast_to(bdw_ref[:, cs], (row_chunk, LANES))
            for kk in range(CONV_K):
                s = (base + kk) % SUBLANES
                a = r0 + base + kk - s
                src = zp_ref if s == 0 else zs_ref.at[s - 1]
                acc = acc + wdw_ref[kk:kk + 1, cs] * src[a:a + row_chunk, cs]
            conv_ref[r0:r0 + row_chunk, cs] = acc

    zc = conv_ref[...]
    mu = jnp.mean(zc, axis=-1, keepdims=True)
    zd = zc - mu
    var = jnp.mean(zd * zd, axis=-1, keepdims=True)
    zn = zd * lax.rsqrt(var + EPS_LN) * gln_ref[...] + bln_ref[...]
    zs = (zn * _sigmoid(zn)).astype(BF16)
    y_b = _dot(zs, wco_ref[...])
    gates = _sigmoid(gate_logits)
    merged = (gates[:, 0:D] * y_a + gates[:, D:2 * D] * y_b).astype(BF16)
    gate_m = mod_ref[:, 2 * D:3 * D]
    out_ref[...] = x_ref[...] + gate_m * _dot(merged, wout_ref[...])


def _cast_chunk_spec(rows, cols, n_i, n_steps):
    every = 1
    while (rows * every) % n_steps or (rows * every // n_steps) % (2 * SUBLANES):
        every *= 2
    chunk = rows * every // n_steps
    return pl.BlockSpec((chunk, cols), lambda b, i: ((b * n_i + i) // every, 0))


def _mix(x, mod3, z, o, h, w_in, w_dw, b_dw, g_ln, b_ln, w_co, w_ao, w_out, cast_ws):
    B, S, D = x.shape
    ts = TS_MIX
    hb = ts // HALO
    n_halo = S // HALO
    n_i = S // ts
    tok = lambda w: pl.BlockSpec((None, ts, w), lambda b, i: (b, i, 0))
    cast_specs = [_cast_chunk_spec(w.shape[0], w.shape[1], n_i, B * n_i) for w in cast_ws]
    return pl.pallas_call(
        functools.partial(_mix_kernel, n_cast=len(cast_ws)),
        out_shape=(jax.ShapeDtypeStruct((B, S, D), F32),
                   *[jax.ShapeDtypeStruct(w.shape, BF16) for w in cast_ws]),
        grid=(B, S // ts),
        in_specs=[
            tok(D),
            pl.BlockSpec((None, 1, N_MOD * D), lambda b, i: (b, 0, 0)),
            tok(CONV_CH),
            pl.BlockSpec((None, HALO, CONV_CH), lambda b, i: (b, jnp.maximum(i * hb - 1, 0), 0)),
            pl.BlockSpec((None, HALO, CONV_CH),
                         lambda b, i: (b, jnp.minimum((i + 1) * hb, n_halo - 1), 0)),
            tok(N_HEADS * V_HEAD),
            tok(D),
            _const_spec(w_in.shape),
            _const_spec(w_dw.shape),
            _const_spec(b_dw.shape),
            _const_spec(g_ln.shape),
            _const_spec(b_ln.shape),
            _const_spec(w_co.shape),
            _const_spec(w_ao.shape),
            _const_spec(w_out.shape),
            *cast_specs,
        ],
        out_specs=(tok(D), *cast_specs),
        scratch_shapes=[
            pltpu.VMEM((ts + 2 * HALO, CONV_CH), F32),
            pltpu.VMEM((SUBLANES - 1, ts + 2 * HALO - SUBLANES, CONV_CH), F32),
            pltpu.VMEM((ts, CONV_CH), F32),
            pltpu.VMEM((D, 2 * D), BF16),
        ],
        compiler_params=pltpu.CompilerParams(
            dimension_semantics=("arbitrary", "arbitrary"), vmem_limit_bytes=VMEM_LIMIT),
        name="conv_merge_out",
    )(x, mod3, z, z, z, o, h, w_in, w_dw, b_dw, g_ln, b_ln, w_co, w_ao, w_out, *cast_ws)


def _ffn_kernel(x_ref, mod_ref, gffn_ref, wg_ref, wu_ref, wd_ref, gfin_ref, out_ref, *, final_norm):
    D = D_MODEL
    shift = mod_ref[:, 3 * D:4 * D]
    scale = mod_ref[:, 4 * D:5 * D]
    gate = mod_ref[:, 5 * D:6 * D]
    rows = x_ref.shape[0] // ROW_SUBTILES
    for j in range(ROW_SUBTILES):
        rs = slice(j * rows, (j + 1) * rows)
        x = x_ref[rs, :]
        h = (_rms(x, gffn_ref[...]) * (1.0 + scale) + shift).astype(BF16)
        g = _dot(h, wg_ref[...])
        u = _dot(h, wu_ref[...])
        a = (g * _sigmoid(g) * u).astype(BF16)
        x2 = x + gate * _dot(a, wd_ref[...])
        out_ref[rs, :] = _rms(x2, gfin_ref[...]) if final_norm else x2


def _ffn(x, mod3, g_ffn, w_gate, w_up, w_down, g_final, final_norm):
    B, S, D = x.shape
    tm = TM_FFN
    tok = pl.BlockSpec((None, tm, D), lambda b, i: (b, i, 0))
    return pl.pallas_call(
        functools.partial(_ffn_kernel, final_norm=final_norm),
        out_shape=jax.ShapeDtypeStruct((B, S, D), F32),
        grid=(B, S // tm),
        in_specs=[
            tok,
            pl.BlockSpec((None, 1, N_MOD * D), lambda b, i: (b, 0, 0)),
            _const_spec(g_ffn.shape),
            _const_spec(w_gate.shape),
            _const_spec(w_up.shape),
            _const_spec(w_down.shape),
            _const_spec(g_final.shape),
        ],
        out_specs=tok,
        compiler_params=pltpu.CompilerParams(
            dimension_semantics=("parallel", "parallel"), vmem_limit_bytes=VMEM_LIMIT),
        name="swiglu_final_norm",
    )(x, mod3, g_ffn, w_gate, w_up, w_down, g_final)


def _prep_w_q(w_q_up):
    r = w_q_up.shape[0]
    w = w_q_up.reshape(r, N_HEADS, QK_HEAD)
    nope = w[..., :QK_NOPE]
    r1 = w[..., QK_NOPE:QK_NOPE + HALF_ROPE]
    r2 = w[..., QK_NOPE + HALF_ROPE:]
    return jnp.concatenate([nope, r1, r2, -r2, r1], axis=-1).reshape(r, N_HEADS * HEAD_PAD).astype(BF16)


def _prep_w_kv(w_kv_up):
    r = w_kv_up.shape[0]
    w = w_kv_up.reshape(r, N_HEADS, QK_NOPE + V_HEAD)
    k_nope = w[..., :QK_NOPE]
    v = w[..., QK_NOPE:]
    k_pad = jnp.concatenate(
        [k_nope, jnp.zeros((r, N_HEADS, HEAD_PAD - QK_NOPE), w.dtype)], axis=-1)
    wk = k_pad.reshape(r, N_HEADS * HEAD_PAD).astype(BF16)
    wvt = v.reshape(r, N_HEADS * V_HEAD).T.astype(BF16)
    return wk, wvt


def kernel(x, c, positions, w_ada, b_ada, g_norm_mix, w_in, g_q_a, w_q_up, g_kv_a, w_kv_up,
           w_attn_o, w_dw, b_dw, g_conv_ln, b_conv_ln, w_conv_out, w_out, g_norm_ffn,
           w_ffn_gate, w_ffn_up, w_ffn_down, g_final):
    B, S, D = x.shape
    depth = w_ada.shape[0]
    cos_c, sin_c = _rope_tables(positions)
    for l in range(depth):
        mod3 = _modulation(c, w_ada[l], b_ada[l][None, :]).reshape(B, 1, N_MOD * D)
        wk, wvt = _prep_w_kv(w_kv_up[l])
        w_in_t = w_in[l].T
        q, k, vt, z, h = _inproj(
            x, mod3, g_norm_mix[l][None, :], w_in_t, g_q_a[l][None, :],
            _prep_w_q(w_q_up[l]), g_kv_a[l][None, :], wk, wvt, cos_c, sin_c)
        o = _attention(q, k, vt)
        x, w_gate_b, w_up_b, w_down_b = _mix(
            x, mod3, z, o, h, w_in_t, w_dw[l], b_dw[l][None, :], g_conv_ln[l][None, :],
            b_conv_ln[l][None, :], w_conv_out[l].astype(BF16), w_attn_o[l].astype(BF16),
            w_out[l].astype(BF16), (w_ffn_gate[l], w_ffn_up[l], w_ffn_down[l]))
        x = _ffn(x, mod3, g_norm_ffn[l][None, :], w_gate_b, w_up_b, w_down_b, g_final[None, :],
                 final_norm=(l == depth - 1))
    return x
```

```python
import functools
import math

import jax
import jax.numpy as jnp
from jax import lax
from jax.experimental import pallas as pl
from jax.experimental.pallas import tpu as pltpu

F32 = jnp.float32
BF16 = jnp.bfloat16

D_MODEL = 1024
N_HEADS = 8
Q_LORA = 256
KV_LORA = 128
QK_NOPE = 64
QK_ROPE = 32
HALF_ROPE = QK_ROPE // 2
V_HEAD = 64
QK_HEAD = QK_NOPE + QK_ROPE
ATTN_SCALE = 1.0 / math.sqrt(QK_HEAD)
Q_SCALE = ATTN_SCALE * math.log2(math.e)
ROPE_THETA = 10000.0
CONV_CH = 512
CONV_K = 31
N_MOD = 6
EPS_RMS = 1e-6
EPS_LN = 1e-5

LANES = 128
SUBLANES = 8
HEAD_PAD = LANES
VMEM_LIMIT = 56 * 1024 * 1024

C_QA = 0
C_KVA = C_QA + Q_LORA
C_KR = C_KVA + KV_LORA
C_CONV = C_KR + HEAD_PAD

TS_IN = 512
TQ = 512
KEY_CHUNK = 1024
TS_MIX = 512
HALO = 16
TM_FFN = 512
ROW_SUBTILES = 2


def _sigmoid(x):
    return 1.0 / (1.0 + jnp.exp(-x))


def _rms(x, g):
    return x * lax.rsqrt(jnp.mean(x * x, axis=-1, keepdims=True) + EPS_RMS) * g


def _dot(a, b):
    return jnp.dot(a, b, preferred_element_type=F32)


def _const_spec(shape):
    nd = len(shape)
    return pl.BlockSpec(shape, lambda *_: (0,) * nd, pipeline_mode=pl.Buffered(1))


def _rope_kernel(pos_ref, freq_ref, cos_ref, sin_ref):
    ang = pos_ref[...] * freq_ref[...]
    cos_ref[...] = jnp.cos(ang)
    sin_ref[...] = jnp.sin(ang)


def _rope_tables(positions):
    B, S = positions.shape
    inv_freq = ROPE_THETA ** (-jnp.arange(0, QK_ROPE, 2, dtype=F32) / QK_ROPE)
    rows = B * S * HALF_ROPE // LANES
    pos_rep = jnp.repeat(positions.reshape(-1).astype(F32), HALF_ROPE).reshape(rows, LANES)
    freq = jnp.tile(inv_freq, LANES // HALF_ROPE)[None, :]
    cos, sin = pl.pallas_call(
        _rope_kernel,
        out_shape=(jax.ShapeDtypeStruct((rows, LANES), F32),) * 2,
        name="rope_tables",
    )(pos_rep, freq)
    return cos, sin


def _expand_rope_table(compact_ref, out_ref, nope_value):
    pos_per_row = LANES // HALF_ROPE
    n = compact_ref.shape[0]
    comp = compact_ref[...]
    lane = lax.broadcasted_iota(jnp.int32, (n, LANES), 1)
    for j in range(pos_per_row):
        shift = (QK_NOPE - HALF_ROPE * j) % LANES
        first = pltpu.roll(comp, shift, 1) if shift else comp
        second = pltpu.roll(first, HALF_ROPE, 1)
        out_ref[pl.ds(j, n, stride=pos_per_row), :] = jnp.where(
            lane < QK_NOPE, nope_value,
            jnp.where(lane < QK_NOPE + HALF_ROPE, first,
                      jnp.where(lane < QK_HEAD, second, 0.0)))


def _rope_tile(t, ctab, stab):
    swapped = pltpu.roll(t, HEAD_PAD - QK_ROPE, 1)
    return t * ctab + swapped * stab


def _mod_kernel(c_ref, w_ref, b_ref, o_ref):
    c = c_ref[...]
    c_act = (c * _sigmoid(c)).astype(BF16)
    o_ref[...] = _dot(c_act, w_ref[...].astype(BF16)) + b_ref[...]


def _modulation(c, w_ada, b_ada):
    B, D = c.shape
    n = w_ada.shape[1]
    bn = 1536
    return pl.pallas_call(
        _mod_kernel,
        out_shape=jax.ShapeDtypeStruct((B, n), F32),
        grid=(n // bn,),
        in_specs=[
            pl.BlockSpec((B, D), lambda j: (0, 0)),
            pl.BlockSpec((D, bn), lambda j: (0, j)),
            pl.BlockSpec((1, bn), lambda j: (0, j)),
        ],
        out_specs=pl.BlockSpec((B, bn), lambda j: (0, j)),
        compiler_params=pltpu.CompilerParams(
            dimension_semantics=("parallel",), vmem_limit_bytes=VMEM_LIMIT),
        name="adaln_modulation",
    )(c, w_ada, b_ada)


def _mixer_input(x, mod_ref, gmix_ref):
    D = D_MODEL
    shift = mod_ref[:, 0:D]
    scale = mod_ref[:, D:2 * D]
    return (_rms(x, gmix_ref[...]) * (1.0 + scale) + shift).astype(BF16)


def _first_grid_step():
    return jnp.logical_and(pl.program_id(0) == 0, pl.program_id(1) == 0)


def _transpose_rows(wt_ref, row0, n_rows, out_ref, col0):
    for j in range(n_rows // LANES):
        rows = wt_ref[row0 + j * LANES:row0 + (j + 1) * LANES, :]
        out_ref[:, col0 + j * LANES:col0 + (j + 1) * LANES] = rows.T.astype(BF16)


def _relayout_w_proj(wt_ref, out_ref):
    o_kr = Q_LORA + KV_LORA
    o_conv = o_kr + QK_ROPE
    _transpose_rows(wt_ref, 0, o_kr, out_ref, C_QA)
    k1 = wt_ref[o_kr:o_kr + HALF_ROPE, :]
    k2 = wt_ref[o_kr + HALF_ROPE:o_conv, :]
    key_tile_t = jnp.concatenate([jnp.zeros((QK_NOPE, k1.shape[1]), F32), k1, k2, -k2, k1], axis=0)
    out_ref[:, C_KR:C_KR + HEAD_PAD] = key_tile_t.T.astype(BF16)
    _transpose_rows(wt_ref, o_conv, 2 * CONV_CH, out_ref, C_CONV)


def _inproj_kernel(x_ref, mod_ref, gmix_ref, w_ref, gq_ref, wq_ref, gkv_ref, wk_ref, wvt_ref,
                   cosc_ref, sinc_ref, *rest, n_cast):
    cast_in = rest[:n_cast]
    q_ref, k_ref, vt_ref, z_ref, h_ref = rest[n_cast:n_cast + 5]
    cast_out = rest[n_cast + 5:2 * n_cast + 5]
    ctab_ref, stab_ref, win_ref = rest[2 * n_cast + 5:]
    for src, dst in zip(cast_in, cast_out):
        dst[...] = src[...].astype(BF16)
    pl.when(_first_grid_step())(lambda: _relayout_w_proj(w_ref, win_ref))
    h = _mixer_input(x_ref[...], mod_ref, gmix_ref)
    h_ref[...] = h
    _expand_rope_table(cosc_ref, ctab_ref, 1.0)
    _expand_rope_table(sinc_ref, stab_ref, 0.0)
    ctab = ctab_ref[...]
    stab = stab_ref[...]

    proj = _dot(h, win_ref[...])

    z_ref[...] = (proj[:, C_CONV:C_CONV + CONV_CH]
                  * _sigmoid(proj[:, C_CONV + CONV_CH:C_CONV + 2 * CONV_CH]))

    qn = _rms(proj[:, C_QA:C_QA + Q_LORA], gq_ref[...]).astype(BF16)
    q_all = _dot(qn, wq_ref[...])
    for hd in range(N_HEADS):
        sl = slice(hd * HEAD_PAD, (hd + 1) * HEAD_PAD)
        q_ref[:, sl] = (_rope_tile(q_all[:, sl], ctab, stab) * Q_SCALE).astype(BF16)

    kvn = _rms(proj[:, C_KVA:C_KVA + KV_LORA], gkv_ref[...]).astype(BF16)
    k_pad = _dot(kvn, wk_ref[...])
    vt_ref[...] = lax.dot_general(wvt_ref[...], kvn, (((1,), (1,)), ((), ())),
                                  preferred_element_type=F32).astype(BF16)
    kr_rot = _rope_tile(proj[:, C_KR:C_KR + HEAD_PAD], ctab, stab)
    for hd in range(N_HEADS):
        sl = slice(hd * HEAD_PAD, (hd + 1) * HEAD_PAD)
        k_ref[:, sl] = (k_pad[:, sl] + kr_rot).astype(BF16)


def _inproj(x, mod3, g_mix, w_in, g_q, wq2, g_kv, wk, wvt, cos_c, sin_c, cast_ws):
    B, S, D = x.shape
    ts = TS_IN
    nq = N_HEADS * HEAD_PAD
    nv = N_HEADS * V_HEAD
    tok = lambda w: pl.BlockSpec((None, ts, w), lambda b, i: (b, i, 0))
    rows = ts * HALF_ROPE // LANES
    n_i = S // ts
    compact = pl.BlockSpec((rows, LANES), lambda b, i: (b * n_i + i, 0))
    cast_specs = [_cast_chunk_spec(w.shape[0], w.shape[1], n_i, B * n_i) for w in cast_ws]
    return pl.pallas_call(
        functools.partial(_inproj_kernel, n_cast=len(cast_ws)),
        out_shape=(
            jax.ShapeDtypeStruct((B, S, nq), BF16),
            jax.ShapeDtypeStruct((B, S, nq), BF16),
            jax.ShapeDtypeStruct((B, nv, S), BF16),
            jax.ShapeDtypeStruct((B, S, CONV_CH), F32),
            jax.ShapeDtypeStruct((B, S, D), BF16),
            *[jax.ShapeDtypeStruct(w.shape, BF16) for w in cast_ws],
        ),
        grid=(B, S // ts),
        in_specs=[
            tok(D),
            pl.BlockSpec((None, 1, N_MOD * D), lambda b, i: (b, 0, 0)),
            _const_spec(g_mix.shape),
            _const_spec(w_in.shape),
            _const_spec(g_q.shape),
            _const_spec(wq2.shape),
            _const_spec(g_kv.shape),
            _const_spec(wk.shape),
            _const_spec(wvt.shape),
            compact,
            compact,
            *cast_specs,
        ],
        out_specs=(tok(nq), tok(nq), pl.BlockSpec((None, nv, ts), lambda b, i: (b, 0, i)),
                   tok(CONV_CH), tok(D), *cast_specs),
        scratch_shapes=[pltpu.VMEM((ts, HEAD_PAD), F32), pltpu.VMEM((ts, HEAD_PAD), F32),
                        pltpu.VMEM((D, C_CONV + 2 * CONV_CH), BF16)],
        compiler_params=pltpu.CompilerParams(
            dimension_semantics=("arbitrary", "arbitrary"), vmem_limit_bytes=VMEM_LIMIT),
        name="input_projection",
    )(x, mod3, g_mix, w_in, g_q, wq2, g_kv, wk, wvt, cos_c, sin_c, *cast_ws)


def _attn_kernel(q_ref, k_ref, vt_ref, o_ref):
    n_kc = k_ref.shape[0] // KEY_CHUNK

    def score_chunk(hd, c):
        sl = slice(hd * HEAD_PAD, (hd + 1) * HEAD_PAD)
        ks = slice(c * KEY_CHUNK, (c + 1) * KEY_CHUNK)
        return lax.dot_general(k_ref[ks, sl], q_ref[:, sl], (((1,), (1,)), ((), ())),
                               preferred_element_type=F32)

    def col_max(chunks):
        m = jnp.max(chunks[0], axis=0, keepdims=True)
        for st in chunks[1:]:
            m = jnp.maximum(m, jnp.max(st, axis=0, keepdims=True))
        return m

    st_next = [score_chunk(0, c) for c in range(n_kc)]
    outs = []
    for hd in range(N_HEADS):
        st_cur, st_next = st_next, []
        m = col_max(st_cur)
        l = acc = None
        for c in range(n_kc):
            if hd + 1 < N_HEADS:
                st_next.append(score_chunk(hd + 1, c))
            ks = slice(c * KEY_CHUNK, (c + 1) * KEY_CHUNK)
            p = jnp.exp2(st_cur[c] - m)
            ls = jnp.sum(p, axis=0, keepdims=True)
            pv = _dot(vt_ref[hd * V_HEAD:(hd + 1) * V_HEAD, ks], p.astype(BF16))
            l, acc = (ls, pv) if c == 0 else (l + ls, acc + pv)
        outs.append(acc / l)
        if hd % 2 == 1:
            o_pair = jnp.concatenate(outs, axis=0).T
            o_ref[:, (hd // 2) * 2 * V_HEAD:(hd // 2 + 1) * 2 * V_HEAD] = o_pair.astype(BF16)
            outs = []


def _attention(q, k, vt):
    B, S, nq = q.shape
    nv = N_HEADS * V_HEAD
    return pl.pallas_call(
        _attn_kernel,
        out_shape=jax.ShapeDtypeStruct((B, S, nv), BF16),
        grid=(B, S // TQ),
        in_specs=[
            pl.BlockSpec((None, TQ, nq), lambda b, i: (b, i, 0)),
            pl.BlockSpec((None, S, nq), lambda b, i: (b, 0, 0)),
            pl.BlockSpec((None, nv, S), lambda b, i: (b, 0, 0)),
        ],
        out_specs=pl.BlockSpec((None, TQ, nv), lambda b, i: (b, i, 0)),
        compiler_params=pltpu.CompilerParams(
            dimension_semantics=("parallel", "parallel"), vmem_limit_bytes=VMEM_LIMIT),
        name="mla_attention",
    )(q, k, vt)


def _relayout_w_gate(wt_ref, out_ref):
    o_gate = Q_LORA + KV_LORA + QK_ROPE + 2 * CONV_CH
    _transpose_rows(wt_ref, o_gate, out_ref.shape[1], out_ref, 0)


def _mix_kernel(x_ref, mod_ref, z_ref, zprev_ref, znext_ref, o_ref, h_ref, w_ref,
                wdw_ref, bdw_ref, gln_ref, bln_ref, wco_ref, wao_ref, wout_ref, *rest, n_cast):
    cast_in = rest[:n_cast]
    out_ref = rest[n_cast]
    cast_out = rest[n_cast + 1:2 * n_cast + 1]
    zp_ref, zs_ref, conv_ref, wgate_ref = rest[2 * n_cast + 1:]
    for src, dst in zip(cast_in, cast_out):
        dst[...] = src[...].astype(BF16)
    D = D_MODEL
    ts = TS_MIX
    i = pl.program_id(1)
    n_i = pl.num_programs(1)
    pl.when(_first_grid_step())(lambda: _relayout_w_gate(w_ref, wgate_ref))

    zp_ref[0:HALO, :] = jnp.where(i > 0, zprev_ref[...], 0.0)
    zp_ref[HALO:HALO + ts, :] = z_ref[...]
    zp_ref[HALO + ts:2 * HALO + ts, :] = jnp.where(i < n_i - 1, znext_ref[...], 0.0)

    n_shift = zs_ref.shape[1]
    for s in range(1, SUBLANES):
        zs_ref[s - 1] = zp_ref[s:s + n_shift, :]

    gate_logits = _dot(h_ref[...], wgate_ref[...])
    y_a = _dot(o_ref[...], wao_ref[...])

    row_chunk = 64
    base = HALO - CONV_K // 2
    for cb in range(CONV_CH // LANES):
        cs = slice(cb * LANES, (cb + 1) * LANES)
        for rb in range(ts // row_chunk):
            r0 = rb * row_chunk
            acc = jnp.broadcast_to(bdw_ref[:, cs], (row_chunk, LANES))
            for kk in range(CONV_K):
                s = (base + kk) % SUBLANES
                a = r0 + base + kk - s
                src = zp_ref if s == 0 else zs_ref.at[s - 1]
                acc = acc + wdw_ref[kk:kk + 1, cs] * src[a:a + row_chunk, cs]
            conv_ref[r0:r0 + row_chunk, cs] = acc

    zc = conv_ref[...]
    mu = jnp.mean(zc, axis=-1, keepdims=True)
    zd = zc - mu
    var = jnp.mean(zd * zd, axis=-1, keepdims=True)
    zn = zd * lax.rsqrt(var + EPS_LN) * gln_ref[...] + bln_ref[...]
    zs = (zn * _sigmoid(zn)).astype(BF16)
    y_b = _dot(zs, wco_ref[...])
    gates = _sigmoid(gate_logits)
    merged = (gates[:, 0:D] * y_a + gates[:, D:2 * D] * y_b).astype(BF16)
    gate_m = mod_ref[:, 2 * D:3 * D]
    out_ref[...] = x_ref[...] + gate_m * _dot(merged, wout_ref[...])


def _cast_chunk_spec(rows, cols, n_i, n_steps):
    every = 1
    while (rows * every) % n_steps or (rows * every // n_steps) % (2 * SUBLANES):
        every *= 2
    chunk = rows * every // n_steps
    return pl.BlockSpec((chunk, cols), lambda b, i: ((b * n_i + i) // every, 0))


def _mix(x, mod3, z, o, h, w_in, w_dw, b_dw, g_ln, b_ln, w_co, w_ao, w_out, cast_ws):
    B, S, D = x.shape
    ts = TS_MIX
    hb = ts // HALO
    n_halo = S // HALO
    n_i = S // ts
    tok = lambda w: pl.BlockSpec((None, ts, w), lambda b, i: (b, i, 0))
    cast_specs = [_cast_chunk_spec(w.shape[0], w.shape[1], n_i, B * n_i) for w in cast_ws]
    return pl.pallas_call(
        functools.partial(_mix_kernel, n_cast=len(cast_ws)),
        out_shape=(jax.ShapeDtypeStruct((B, S, D), F32),
                   *[jax.ShapeDtypeStruct(w.shape, BF16) for w in cast_ws]),
        grid=(B, S // ts),
        in_specs=[
            tok(D),
            pl.BlockSpec((None, 1, N_MOD * D), lambda b, i: (b, 0, 0)),
            tok(CONV_CH),
            pl.BlockSpec((None, HALO, CONV_CH), lambda b, i: (b, jnp.maximum(i * hb - 1, 0), 0)),
            pl.BlockSpec((None, HALO, CONV_CH),
                         lambda b, i: (b, jnp.minimum((i + 1) * hb, n_halo - 1), 0)),
            tok(N_HEADS * V_HEAD),
            tok(D),
            _const_spec(w_in.shape),
            _const_spec(w_dw.shape),
            _const_spec(b_dw.shape),
            _const_spec(g_ln.shape),
            _const_spec(b_ln.shape),
            _const_spec(w_co.shape),
            _const_spec(w_ao.shape),
            _const_spec(w_out.shape),
            *cast_specs,
        ],
        out_specs=(tok(D), *cast_specs),
        scratch_shapes=[
            pltpu.VMEM((ts + 2 * HALO, CONV_CH), F32),
            pltpu.VMEM((SUBLANES - 1, ts + 2 * HALO - SUBLANES, CONV_CH), F32),
            pltpu.VMEM((ts, CONV_CH), F32),
            pltpu.VMEM((D, 2 * D), BF16),
        ],
        compiler_params=pltpu.CompilerParams(
            dimension_semantics=("arbitrary", "arbitrary"), vmem_limit_bytes=VMEM_LIMIT),
        name="conv_merge_out",
    )(x, mod3, z, z, z, o, h, w_in, w_dw, b_dw, g_ln, b_ln, w_co, w_ao, w_out, *cast_ws)


def _ffn_kernel(x_ref, mod_ref, gffn_ref, wg_ref, wu_ref, wd_ref, gfin_ref, out_ref, *, final_norm):
    D = D_MODEL
    shift = mod_ref[:, 3 * D:4 * D]
    scale = mod_ref[:, 4 * D:5 * D]
    gate = mod_ref[:, 5 * D:6 * D]
    rows = x_ref.shape[0] // ROW_SUBTILES
    for j in range(ROW_SUBTILES):
        rs = slice(j * rows, (j + 1) * rows)
        x = x_ref[rs, :]
        h = (_rms(x, gffn_ref[...]) * (1.0 + scale) + shift).astype(BF16)
        g = _dot(h, wg_ref[...])
        u = _dot(h, wu_ref[...])
        a = (g * _sigmoid(g) * u).astype(BF16)
        x2 = x + gate * _dot(a, wd_ref[...])
        out_ref[rs, :] = _rms(x2, gfin_ref[...]) if final_norm else x2


def _ffn(x, mod3, g_ffn, w_gate, w_up, w_down, g_final, final_norm):
    B, S, D = x.shape
    tm = TM_FFN
    tok = pl.BlockSpec((None, tm, D), lambda b, i: (b, i, 0))
    return pl.pallas_call(
        functools.partial(_ffn_kernel, final_norm=final_norm),
        out_shape=jax.ShapeDtypeStruct((B, S, D), F32),
        grid=(B, S // tm),
        in_specs=[
            tok,
            pl.BlockSpec((None, 1, N_MOD * D), lambda b, i: (b, 0, 0)),
            _const_spec(g_ffn.shape),
            _const_spec(w_gate.shape),
            _const_spec(w_up.shape),
            _const_spec(w_down.shape),
            _const_spec(g_final.shape),
        ],
        out_specs=tok,
        compiler_params=pltpu.CompilerParams(
            dimension_semantics=("parallel", "parallel"), vmem_limit_bytes=VMEM_LIMIT),
        name="swiglu_final_norm",
    )(x, mod3, g_ffn, w_gate, w_up, w_down, g_final)


def _prep_w_q(w_q_up):
    r = w_q_up.shape[0]
    w = w_q_up.reshape(r, N_HEADS, QK_HEAD)
    nope = w[..., :QK_NOPE]
    r1 = w[..., QK_NOPE:QK_NOPE + HALF_ROPE]
    r2 = w[..., QK_NOPE + HALF_ROPE:]
    return jnp.concatenate([nope, r1, r2, -r2, r1], axis=-1).reshape(r, N_HEADS * HEAD_PAD).astype(BF16)


def _prep_w_kv(w_kv_up):
    r = w_kv_up.shape[0]
    w = w_kv_up.reshape(r, N_HEADS, QK_NOPE + V_HEAD)
    k_nope = w[..., :QK_NOPE]
    v = w[..., QK_NOPE:]
    k_pad = jnp.concatenate(
        [k_nope, jnp.zeros((r, N_HEADS, HEAD_PAD - QK_NOPE), w.dtype)], axis=-1)
    wk = k_pad.reshape(r, N_HEADS * HEAD_PAD).astype(BF16)
    wvt = v.reshape(r, N_HEADS * V_HEAD).T.astype(BF16)
    return wk, wvt


def kernel(x, c, positions, w_ada, b_ada, g_norm_mix, w_in, g_q_a, w_q_up, g_kv_a, w_kv_up,
           w_attn_o, w_dw, b_dw, g_conv_ln, b_conv_ln, w_conv_out, w_out, g_norm_ffn,
           w_ffn_gate, w_ffn_up, w_ffn_down, g_final):
    B, S, D = x.shape
    depth = w_ada.shape[0]
    cos_c, sin_c = _rope_tables(positions)
    for l in range(depth):
        mod3 = _modulation(c, w_ada[l], b_ada[l][None, :]).reshape(B, 1, N_MOD * D)
        wk, wvt = _prep_w_kv(w_kv_up[l])
        w_in_t = w_in[l].T
        q, k, vt, z, h, w_co_b, w_ao_b, w_out_b = _inproj(
            x, mod3, g_norm_mix[l][None, :], w_in_t, g_q_a[l][None, :],
            _prep_w_q(w_q_up[l]), g_kv_a[l][None, :], wk, wvt, cos_c, sin_c,
            (w_conv_out[l], w_attn_o[l], w_out[l]))
        o = _attention(q, k, vt)
        x, w_gate_b, w_up_b, w_down_b = _mix(
            x, mod3, z, o, h, w_in_t, w_dw[l], b_dw[l][None, :], g_conv_ln[l][None, :],
            b_conv_ln[l][None, :], w_co_b, w_ao_b, w_out_b,
            (w_ffn_gate[l], w_ffn_up[l], w_ffn_down[l]))
        x = _ffn(x, mod3, g_norm_ffn[l][None, :], w_gate_b, w_up_b, w_down_b, g_final[None, :],
                 final_norm=(l == depth - 1))
    return x
```

```python
import functools
import math

import jax
import jax.numpy as jnp
from jax import lax
from jax.experimental import pallas as pl
from jax.experimental.pallas import tpu as pltpu

F32 = jnp.float32
BF16 = jnp.bfloat16

D_MODEL = 1024
N_HEADS = 8
Q_LORA = 256
KV_LORA = 128
QK_NOPE = 64
QK_ROPE = 32
HALF_ROPE = QK_ROPE // 2
V_HEAD = 64
QK_HEAD = QK_NOPE + QK_ROPE
ATTN_SCALE = 1.0 / math.sqrt(QK_HEAD)
Q_SCALE = ATTN_SCALE * math.log2(math.e)
ROPE_THETA = 10000.0
CONV_CH = 512
CONV_K = 31
N_MOD = 6
EPS_RMS = 1e-6
EPS_LN = 1e-5

LANES = 128
SUBLANES = 8
HEAD_PAD = LANES
VMEM_LIMIT = 56 * 1024 * 1024

C_QA = 0
C_KVA = C_QA + Q_LORA
C_KR = C_KVA + KV_LORA
C_CONV = C_KR + HEAD_PAD

TS_IN = 1024
TQ = 512
KEY_CHUNK = 1024
TS_MIX = 512
HALO = 16
TM_FFN = 512
ROW_SUBTILES = 2


def _sigmoid(x):
    return 1.0 / (1.0 + jnp.exp(-x))


def _rms(x, g):
    return x * lax.rsqrt(jnp.mean(x * x, axis=-1, keepdims=True) + EPS_RMS) * g


def _dot(a, b):
    return jnp.dot(a, b, preferred_element_type=F32)


def _const_spec(shape):
    nd = len(shape)
    return pl.BlockSpec(shape, lambda *_: (0,) * nd, pipeline_mode=pl.Buffered(1))


def _rope_kernel(pos_ref, freq_ref, cos_ref, sin_ref):
    ang = pos_ref[...] * freq_ref[...]
    cos_ref[...] = jnp.cos(ang)
    sin_ref[...] = jnp.sin(ang)


def _rope_tables(positions):
    B, S = positions.shape
    inv_freq = ROPE_THETA ** (-jnp.arange(0, QK_ROPE, 2, dtype=F32) / QK_ROPE)
    rows = B * S * HALF_ROPE // LANES
    pos_rep = jnp.repeat(positions.reshape(-1).astype(F32), HALF_ROPE).reshape(rows, LANES)
    freq = jnp.tile(inv_freq, LANES // HALF_ROPE)[None, :]
    cos, sin = pl.pallas_call(
        _rope_kernel,
        out_shape=(jax.ShapeDtypeStruct((rows, LANES), F32),) * 2,
        name="rope_tables",
    )(pos_rep, freq)
    return cos, sin


def _expand_rope_table(compact_ref, out_ref, nope_value):
    pos_per_row = LANES // HALF_ROPE
    n = compact_ref.shape[0]
    comp = compact_ref[...]
    lane = lax.broadcasted_iota(jnp.int32, (n, LANES), 1)
    for j in range(pos_per_row):
        shift = (QK_NOPE - HALF_ROPE * j) % LANES
        first = pltpu.roll(comp, shift, 1) if shift else comp
        second = pltpu.roll(first, HALF_ROPE, 1)
        out_ref[pl.ds(j, n, stride=pos_per_row), :] = jnp.where(
            lane < QK_NOPE, nope_value,
            jnp.where(lane < QK_NOPE + HALF_ROPE, first,
                      jnp.where(lane < QK_HEAD, second, 0.0)))


def _rope_tile(t, ctab, stab):
    swapped = pltpu.roll(t, HEAD_PAD - QK_ROPE, 1)
    return t * ctab + swapped * stab


def _mod_kernel(c_ref, w_ref, b_ref, o_ref):
    c = c_ref[...]
    c_act = (c * _sigmoid(c)).astype(BF16)
    o_ref[...] = _dot(c_act, w_ref[...].astype(BF16)) + b_ref[...]


def _modulation(c, w_ada, b_ada):
    B, D = c.shape
    n = w_ada.shape[1]
    bn = 1536
    return pl.pallas_call(
        _mod_kernel,
        out_shape=jax.ShapeDtypeStruct((B, n), F32),
        grid=(n // bn,),
        in_specs=[
            pl.BlockSpec((B, D), lambda j: (0, 0)),
            pl.BlockSpec((D, bn), lambda j: (0, j)),
            pl.BlockSpec((1, bn), lambda j: (0, j)),
        ],
        out_specs=pl.BlockSpec((B, bn), lambda j: (0, j)),
        compiler_params=pltpu.CompilerParams(
            dimension_semantics=("parallel",), vmem_limit_bytes=VMEM_LIMIT),
        name="adaln_modulation",
    )(c, w_ada, b_ada)


def _mixer_input(x, mod_ref, gmix_ref):
    D = D_MODEL
    shift = mod_ref[:, 0:D]
    scale = mod_ref[:, D:2 * D]
    return (_rms(x, gmix_ref[...]) * (1.0 + scale) + shift).astype(BF16)


def _first_grid_step():
    return jnp.logical_and(pl.program_id(0) == 0, pl.program_id(1) == 0)


def _transpose_rows(wt_ref, row0, n_rows, out_ref, col0):
    for j in range(n_rows // LANES):
        rows = wt_ref[row0 + j * LANES:row0 + (j + 1) * LANES, :]
        out_ref[:, col0 + j * LANES:col0 + (j + 1) * LANES] = rows.T.astype(BF16)


def _relayout_w_proj(wt_ref, out_ref):
    o_kr = Q_LORA + KV_LORA
    o_conv = o_kr + QK_ROPE
    _transpose_rows(wt_ref, 0, o_kr, out_ref, C_QA)
    k1 = wt_ref[o_kr:o_kr + HALF_ROPE, :]
    k2 = wt_ref[o_kr + HALF_ROPE:o_conv, :]
    key_tile_t = jnp.concatenate([jnp.zeros((QK_NOPE, k1.shape[1]), F32), k1, k2, -k2, k1], axis=0)
    out_ref[:, C_KR:C_KR + HEAD_PAD] = key_tile_t.T.astype(BF16)
    _transpose_rows(wt_ref, o_conv, 2 * CONV_CH, out_ref, C_CONV)


def _inproj_kernel(x_ref, mod_ref, gmix_ref, w_ref, gq_ref, wq_ref, gkv_ref, wk_ref, wvt_ref,
                   cosc_ref, sinc_ref, *rest, n_cast):
    cast_in = rest[:n_cast]
    q_ref, k_ref, vt_ref, z_ref, h_ref = rest[n_cast:n_cast + 5]
    cast_out = rest[n_cast + 5:2 * n_cast + 5]
    ctab_ref, stab_ref, win_ref = rest[2 * n_cast + 5:]
    for src, dst in zip(cast_in, cast_out):
        dst[...] = src[...].astype(BF16)
    pl.when(_first_grid_step())(lambda: _relayout_w_proj(w_ref, win_ref))
    h = _mixer_input(x_ref[...], mod_ref, gmix_ref)
    h_ref[...] = h
    _expand_rope_table(cosc_ref, ctab_ref, 1.0)
    _expand_rope_table(sinc_ref, stab_ref, 0.0)
    ctab = ctab_ref[...]
    stab = stab_ref[...]

    proj = _dot(h, win_ref[...])

    z_ref[...] = (proj[:, C_CONV:C_CONV + CONV_CH]
                  * _sigmoid(proj[:, C_CONV + CONV_CH:C_CONV + 2 * CONV_CH]))

    qn = _rms(proj[:, C_QA:C_QA + Q_LORA], gq_ref[...]).astype(BF16)
    q_all = _dot(qn, wq_ref[...])
    for hd in range(N_HEADS):
        sl = slice(hd * HEAD_PAD, (hd + 1) * HEAD_PAD)
        q_ref[:, sl] = (_rope_tile(q_all[:, sl], ctab, stab) * Q_SCALE).astype(BF16)

    kvn = _rms(proj[:, C_KVA:C_KVA + KV_LORA], gkv_ref[...]).astype(BF16)
    k_pad = _dot(kvn, wk_ref[...])
    vt_ref[...] = lax.dot_general(wvt_ref[...], kvn, (((1,), (1,)), ((), ())),
                                  preferred_element_type=F32).astype(BF16)
    kr_rot = _rope_tile(proj[:, C_KR:C_KR + HEAD_PAD], ctab, stab)
    for hd in range(N_HEADS):
        sl = slice(hd * HEAD_PAD, (hd + 1) * HEAD_PAD)
        k_ref[:, sl] = (k_pad[:, sl] + kr_rot).astype(BF16)


def _inproj(x, mod3, g_mix, w_in, g_q, wq2, g_kv, wk, wvt, cos_c, sin_c, cast_ws):
    B, S, D = x.shape
    ts = TS_IN
    nq = N_HEADS * HEAD_PAD
    nv = N_HEADS * V_HEAD
    tok = lambda w: pl.BlockSpec((None, ts, w), lambda b, i: (b, i, 0))
    rows = ts * HALF_ROPE // LANES
    n_i = S // ts
    compact = pl.BlockSpec((rows, LANES), lambda b, i: (b * n_i + i, 0))
    cast_specs = [_cast_chunk_spec(w.shape[0], w.shape[1], n_i, B * n_i) for w in cast_ws]
    return pl.pallas_call(
        functools.partial(_inproj_kernel, n_cast=len(cast_ws)),
        out_shape=(
            jax.ShapeDtypeStruct((B, S, nq), BF16),
            jax.ShapeDtypeStruct((B, S, nq), BF16),
            jax.ShapeDtypeStruct((B, nv, S), BF16),
            jax.ShapeDtypeStruct((B, S, CONV_CH), F32),
            jax.ShapeDtypeStruct((B, S, D), BF16),
            *[jax.ShapeDtypeStruct(w.shape, BF16) for w in cast_ws],
        ),
        grid=(B, S // ts),
        in_specs=[
            tok(D),
            pl.BlockSpec((None, 1, N_MOD * D), lambda b, i: (b, 0, 0)),
            _const_spec(g_mix.shape),
            _const_spec(w_in.shape),
            _const_spec(g_q.shape),
            _const_spec(wq2.shape),
            _const_spec(g_kv.shape),
            _const_spec(wk.shape),
            _const_spec(wvt.shape),
            compact,
            compact,
            *cast_specs,
        ],
        out_specs=(tok(nq), tok(nq), pl.BlockSpec((None, nv, ts), lambda b, i: (b, 0, i)),
                   tok(CONV_CH), tok(D), *cast_specs),
        scratch_shapes=[pltpu.VMEM((ts, HEAD_PAD), F32), pltpu.VMEM((ts, HEAD_PAD), F32),
                        pltpu.VMEM((D, C_CONV + 2 * CONV_CH), BF16)],
        compiler_params=pltpu.CompilerParams(
            dimension_semantics=("arbitrary", "arbitrary"), vmem_limit_bytes=VMEM_LIMIT),
        name="input_projection",
    )(x, mod3, g_mix, w_in, g_q, wq2, g_kv, wk, wvt, cos_c, sin_c, *cast_ws)


def _attn_kernel(q_ref, k_ref, vt_ref, o_ref):
    n_kc = k_ref.shape[0] // KEY_CHUNK

    def score_chunk(hd, c):
        sl = slice(hd * HEAD_PAD, (hd + 1) * HEAD_PAD)
        ks = slice(c * KEY_CHUNK, (c + 1) * KEY_CHUNK)
        return lax.dot_general(k_ref[ks, sl], q_ref[:, sl], (((1,), (1,)), ((), ())),
                               preferred_element_type=F32)

    def col_max(chunks):
        m = jnp.max(chunks[0], axis=0, keepdims=True)
        for st in chunks[1:]:
            m = jnp.maximum(m, jnp.max(st, axis=0, keepdims=True))
        return m

    st_next = [score_chunk(0, c) for c in range(n_kc)]
    outs = []
    for hd in range(N_HEADS):
        st_cur, st_next = st_next, []
        m = col_max(st_cur)
        l = acc = None
        for c in range(n_kc):
            if hd + 1 < N_HEADS:
                st_next.append(score_chunk(hd + 1, c))
            ks = slice(c * KEY_CHUNK, (c + 1) * KEY_CHUNK)
            p = jnp.exp2(st_cur[c] - m)
            ls = jnp.sum(p, axis=0, keepdims=True)
            pv = _dot(vt_ref[hd * V_HEAD:(hd + 1) * V_HEAD, ks], p.astype(BF16))
            l, acc = (ls, pv) if c == 0 else (l + ls, acc + pv)
        outs.append(acc / l)
        if hd % 2 == 1:
            o_pair = jnp.concatenate(outs, axis=0).T
            o_ref[:, (hd // 2) * 2 * V_HEAD:(hd // 2 + 1) * 2 * V_HEAD] = o_pair.astype(BF16)
            outs = []


def _attention(q, k, vt):
    B, S, nq = q.shape
    nv = N_HEADS * V_HEAD
    return pl.pallas_call(
        _attn_kernel,
        out_shape=jax.ShapeDtypeStruct((B, S, nv), BF16),
        grid=(B, S // TQ),
        in_specs=[
            pl.BlockSpec((None, TQ, nq), lambda b, i: (b, i, 0)),
            pl.BlockSpec((None, S, nq), lambda b, i: (b, 0, 0)),
            pl.BlockSpec((None, nv, S), lambda b, i: (b, 0, 0)),
        ],
        out_specs=pl.BlockSpec((None, TQ, nv), lambda b, i: (b, i, 0)),
        compiler_params=pltpu.CompilerParams(
            dimension_semantics=("parallel", "parallel"), vmem_limit_bytes=VMEM_LIMIT),
        name="mla_attention",
    )(q, k, vt)


def _relayout_w_gate(wt_ref, out_ref):
    o_gate = Q_LORA + KV_LORA + QK_ROPE + 2 * CONV_CH
    _transpose_rows(wt_ref, o_gate, out_ref.shape[1], out_ref, 0)


def _mix_kernel(x_ref, mod_ref, z_ref, zprev_ref, znext_ref, o_ref, h_ref, w_ref,
                wdw_ref, bdw_ref, gln_ref, bln_ref, wco_ref, wao_ref, wout_ref, *rest, n_cast):
    cast_in = rest[:n_cast]
    out_ref = rest[n_cast]
    cast_out = rest[n_cast + 1:2 * n_cast + 1]
    zp_ref, zs_ref, conv_ref, wgate_ref = rest[2 * n_cast + 1:]
    for src, dst in zip(cast_in, cast_out):
        dst[...] = src[...].astype(BF16)
    D = D_MODEL
    ts = TS_MIX
    i = pl.program_id(1)
    n_i = pl.num_programs(1)
    pl.when(_first_grid_step())(lambda: _relayout_w_gate(w_ref, wgate_ref))

    zp_ref[0:HALO, :] = jnp.where(i > 0, zprev_ref[...], 0.0)
    zp_ref[HALO:HALO + ts, :] = z_ref[...]
    zp_ref[HALO + ts:2 * HALO + ts, :] = jnp.where(i < n_i - 1, znext_ref[...], 0.0)

    n_shift = zs_ref.shape[1]
    for s in range(1, SUBLANES):
        zs_ref[s - 1] = zp_ref[s:s + n_shift, :]

    gate_logits = _dot(h_ref[...], wgate_ref[...])
    y_a = _dot(o_ref[...], wao_ref[...])

    row_chunk = 64
    base = HALO - CONV_K // 2
    for cb in range(CONV_CH // LANES):
        cs = slice(cb * LANES, (cb + 1) * LANES)
        for rb in range(ts // row_chunk):
            r0 = rb * row_chunk
            acc = jnp.broadcast_to(bdw_ref[:, cs], (row_chunk, LANES))
            for kk in range(CONV_K):
                s = (base + kk) % SUBLANES
                a = r0 + base + kk - s
                src = zp_ref if s == 0 else zs_ref.at[s - 1]
                acc = acc + wdw_ref[kk:kk + 1, cs] * src[a:a + row_chunk, cs]
            conv_ref[r0:r0 + row_chunk, cs] = acc

    zc = conv_ref[...]
    mu = jnp.mean(zc, axis=-1, keepdims=True)
    zd = zc - mu
    var = jnp.mean(zd * zd, axis=-1, keepdims=True)
    zn = zd * lax.rsqrt(var + EPS_LN) * gln_ref[...] + bln_ref[...]
    zs = (zn * _sigmoid(zn)).astype(BF16)
    y_b = _dot(zs, wco_ref[...])
    gates = _sigmoid(gate_logits)
    merged = (gates[:, 0:D] * y_a + gates[:, D:2 * D] * y_b).astype(BF16)
    gate_m = mod_ref[:, 2 * D:3 * D]
    out_ref[...] = x_ref[...] + gate_m * _dot(merged, wout_ref[...])


def _cast_chunk_spec(rows, cols, n_i, n_steps):
    every = 1
    while (rows * every) % n_steps or (rows * every // n_steps) % (2 * SUBLANES):
        every *= 2
    chunk = rows * every // n_steps
    return pl.BlockSpec((chunk, cols), lambda b, i: ((b * n_i + i) // every, 0))


def _mix(x, mod3, z, o, h, w_in, w_dw, b_dw, g_ln, b_ln, w_co, w_ao, w_out, cast_ws):
    B, S, D = x.shape
    ts = TS_MIX
    hb = ts // HALO
    n_halo = S // HALO
    n_i = S // ts
    tok = lambda w: pl.BlockSpec((None, ts, w), lambda b, i: (b, i, 0))
    cast_specs = [_cast_chunk_spec(w.shape[0], w.shape[1], n_i, B * n_i) for w in cast_ws]
    return pl.pallas_call(
        functools.partial(_mix_kernel, n_cast=len(cast_ws)),
        out_shape=(jax.ShapeDtypeStruct((B, S, D), F32),
                   *[jax.ShapeDtypeStruct(w.shape, BF16) for w in cast_ws]),
        grid=(B, S // ts),
        in_specs=[
            tok(D),
            pl.BlockSpec((None, 1, N_MOD * D), lambda b, i: (b, 0, 0)),
            tok(CONV_CH),
            pl.BlockSpec((None, HALO, CONV_CH), lambda b, i: (b, jnp.maximum(i * hb - 1, 0), 0)),
            pl.BlockSpec((None, HALO, CONV_CH),
                         lambda b, i: (b, jnp.minimum((i + 1) * hb, n_halo - 1), 0)),
            tok(N_HEADS * V_HEAD),
            tok(D),
            _const_spec(w_in.shape),
            _const_spec(w_dw.shape),
            _const_spec(b_dw.shape),
            _const_spec(g_ln.shape),
            _const_spec(b_ln.shape),
            _const_spec(w_co.shape),
            _const_spec(w_ao.shape),
            _const_spec(w_out.shape),
            *cast_specs,
        ],
        out_specs=(tok(D), *cast_specs),
        scratch_shapes=[
            pltpu.VMEM((ts + 2 * HALO, CONV_CH), F32),
            pltpu.VMEM((SUBLANES - 1, ts + 2 * HALO - SUBLANES, CONV_CH), F32),
            pltpu.VMEM((ts, CONV_CH), F32),
            pltpu.VMEM((D, 2 * D), BF16),
        ],
        compiler_params=pltpu.CompilerParams(
            dimension_semantics=("arbitrary", "arbitrary"), vmem_limit_bytes=VMEM_LIMIT),
        name="conv_merge_out",
    )(x, mod3, z, z, z, o, h, w_in, w_dw, b_dw, g_ln, b_ln, w_co, w_ao, w_out, *cast_ws)


def _ffn_kernel(x_ref, mod_ref, gffn_ref, wg_ref, wu_ref, wd_ref, gfin_ref, out_ref, *, final_norm):
    D = D_MODEL
    shift = mod_ref[:, 3 * D:4 * D]
    scale = mod_ref[:, 4 * D:5 * D]
    gate = mod_ref[:, 5 * D:6 * D]
    rows = x_ref.shape[0] // ROW_SUBTILES
    for j in range(ROW_SUBTILES):
        rs = slice(j * rows, (j + 1) * rows)
        x = x_ref[rs, :]
        h = (_rms(x, gffn_ref[...]) * (1.0 + scale) + shift).astype(BF16)
        g = _dot(h, wg_ref[...])
        u = _dot(h, wu_ref[...])
        a = (g * _sigmoid(g) * u).astype(BF16)
        x2 = x + gate * _dot(a, wd_ref[...])
        out_ref[rs, :] = _rms(x2, gfin_ref[...]) if final_norm else x2


def _ffn(x, mod3, g_ffn, w_gate, w_up, w_down, g_final, final_norm):
    B, S, D = x.shape
    tm = TM_FFN
    tok = pl.BlockSpec((None, tm, D), lambda b, i: (b, i, 0))
    return pl.pallas_call(
        functools.partial(_ffn_kernel, final_norm=final_norm),
        out_shape=jax.ShapeDtypeStruct((B, S, D), F32),
        grid=(B, S // tm),
        in_specs=[
            tok,
            pl.BlockSpec((None, 1, N_MOD * D), lambda b, i: (b, 0, 0)),
            _const_spec(g_ffn.shape),
            _const_spec(w_gate.shape),
            _const_spec(w_up.shape),
            _const_spec(w_down.shape),
            _const_spec(g_final.shape),
        ],
        out_specs=tok,
        compiler_params=pltpu.CompilerParams(
            dimension_semantics=("parallel", "parallel"), vmem_limit_bytes=VMEM_LIMIT),
        name="swiglu_final_norm",
    )(x, mod3, g_ffn, w_gate, w_up, w_down, g_final)


def _prep_w_q(w_q_up):
    r = w_q_up.shape[0]
    w = w_q_up.reshape(r, N_HEADS, QK_HEAD)
    nope = w[..., :QK_NOPE]
    r1 = w[..., QK_NOPE:QK_NOPE + HALF_ROPE]
    r2 = w[..., QK_NOPE + HALF_ROPE:]
    return jnp.concatenate([nope, r1, r2, -r2, r1], axis=-1).reshape(r, N_HEADS * HEAD_PAD).astype(BF16)


def _prep_w_kv(w_kv_up):
    r = w_kv_up.shape[0]
    w = w_kv_up.reshape(r, N_HEADS, QK_NOPE + V_HEAD)
    k_nope = w[..., :QK_NOPE]
    v = w[..., QK_NOPE:]
    k_pad = jnp.concatenate(
        [k_nope, jnp.zeros((r, N_HEADS, HEAD_PAD - QK_NOPE), w.dtype)], axis=-1)
    wk = k_pad.reshape(r, N_HEADS * HEAD_PAD).astype(BF16)
    wvt = v.reshape(r, N_HEADS * V_HEAD).T.astype(BF16)
    return wk, wvt


def kernel(x, c, positions, w_ada, b_ada, g_norm_mix, w_in, g_q_a, w_q_up, g_kv_a, w_kv_up,
           w_attn_o, w_dw, b_dw, g_conv_ln, b_conv_ln, w_conv_out, w_out, g_norm_ffn,
           w_ffn_gate, w_ffn_up, w_ffn_down, g_final):
    B, S, D = x.shape
    depth = w_ada.shape[0]
    cos_c, sin_c = _rope_tables(positions)
    for l in range(depth):
        mod3 = _modulation(c, w_ada[l], b_ada[l][None, :]).reshape(B, 1, N_MOD * D)
        wk, wvt = _prep_w_kv(w_kv_up[l])
        w_in_t = w_in[l].T
        q, k, vt, z, h, w_co_b, w_ao_b, w_out_b = _inproj(
            x, mod3, g_norm_mix[l][None, :], w_in_t, g_q_a[l][None, :],
            _prep_w_q(w_q_up[l]), g_kv_a[l][None, :], wk, wvt, cos_c, sin_c,
            (w_conv_out[l], w_attn_o[l], w_out[l]))
        o = _attention(q, k, vt)
        x, w_gate_b, w_up_b, w_down_b = _mix(
            x, mod3, z, o, h, w_in_t, w_dw[l], b_dw[l][None, :], g_conv_ln[l][None, :],
            b_conv_ln[l][None, :], w_co_b, w_ao_b, w_out_b,
            (w_ffn_gate[l], w_ffn_up[l], w_ffn_down[l]))
        x = _ffn(x, mod3, g_norm_ffn[l][None, :], w_gate_b, w_up_b, w_down_b, g_final[None, :],
                 final_norm=(l == depth - 1))
    return x
```

```python
import functools
import math

import jax
import jax.numpy as jnp
from jax import lax
from jax.experimental import pallas as pl
from jax.experimental.pallas import tpu as pltpu

F32 = jnp.float32
BF16 = jnp.bfloat16

D_MODEL = 1024
N_HEADS = 8
Q_LORA = 256
KV_LORA = 128
QK_NOPE = 64
QK_ROPE = 32
HALF_ROPE = QK_ROPE // 2
V_HEAD = 64
QK_HEAD = QK_NOPE + QK_ROPE
ATTN_SCALE = 1.0 / math.sqrt(QK_HEAD)
Q_SCALE = ATTN_SCALE * math.log2(math.e)
ROPE_THETA = 10000.0
CONV_CH = 512
CONV_K = 31
N_MOD = 6
EPS_RMS = 1e-6
EPS_LN = 1e-5

LANES = 128
SUBLANES = 8
HEAD_PAD = LANES
VMEM_LIMIT = 56 * 1024 * 1024

C_QA = 0
C_KVA = C_QA + Q_LORA
C_KR = C_KVA + KV_LORA
C_CONV = C_KR + HEAD_PAD

TS_IN = 1024
TQ = 512
KEY_CHUNK = 1024
TS_MIX = 512
HALO = 16
TM_FFN = 1024
ROW_SUBTILES = 4


def _sigmoid(x):
    return 1.0 / (1.0 + jnp.exp(-x))


def _rms(x, g):
    return x * lax.rsqrt(jnp.mean(x * x, axis=-1, keepdims=True) + EPS_RMS) * g


def _dot(a, b):
    return jnp.dot(a, b, preferred_element_type=F32)


def _const_spec(shape):
    nd = len(shape)
    return pl.BlockSpec(shape, lambda *_: (0,) * nd, pipeline_mode=pl.Buffered(1))


def _rope_kernel(pos_ref, freq_ref, cos_ref, sin_ref):
    ang = pos_ref[...] * freq_ref[...]
    cos_ref[...] = jnp.cos(ang)
    sin_ref[...] = jnp.sin(ang)


def _rope_tables(positions):
    B, S = positions.shape
    inv_freq = ROPE_THETA ** (-jnp.arange(0, QK_ROPE, 2, dtype=F32) / QK_ROPE)
    rows = B * S * HALF_ROPE // LANES
    pos_rep = jnp.repeat(positions.reshape(-1).astype(F32), HALF_ROPE).reshape(rows, LANES)
    freq = jnp.tile(inv_freq, LANES // HALF_ROPE)[None, :]
    cos, sin = pl.pallas_call(
        _rope_kernel,
        out_shape=(jax.ShapeDtypeStruct((rows, LANES), F32),) * 2,
        name="rope_tables",
    )(pos_rep, freq)
    return cos, sin


def _expand_rope_table(compact_ref, out_ref, nope_value):
    pos_per_row = LANES // HALF_ROPE
    n = compact_ref.shape[0]
    comp = compact_ref[...]
    lane = lax.broadcasted_iota(jnp.int32, (n, LANES), 1)
    for j in range(pos_per_row):
        shift = (QK_NOPE - HALF_ROPE * j) % LANES
        first = pltpu.roll(comp, shift, 1) if shift else comp
        second = pltpu.roll(first, HALF_ROPE, 1)
        out_ref[pl.ds(j, n, stride=pos_per_row), :] = jnp.where(
            lane < QK_NOPE, nope_value,
            jnp.where(lane < QK_NOPE + HALF_ROPE, first,
                      jnp.where(lane < QK_HEAD, second, 0.0)))


def _rope_tile(t, ctab, stab):
    swapped = pltpu.roll(t, HEAD_PAD - QK_ROPE, 1)
    return t * ctab + swapped * stab


def _mod_kernel(c_ref, w_ref, b_ref, o_ref):
    c = c_ref[...]
    c_act = (c * _sigmoid(c)).astype(BF16)
    o_ref[...] = _dot(c_act, w_ref[...].astype(BF16)) + b_ref[...]


def _modulation(c, w_ada, b_ada):
    B, D = c.shape
    n = w_ada.shape[1]
    bn = 1536
    return pl.pallas_call(
        _mod_kernel,
        out_shape=jax.ShapeDtypeStruct((B, n), F32),
        grid=(n // bn,),
        in_specs=[
            pl.BlockSpec((B, D), lambda j: (0, 0)),
            pl.BlockSpec((D, bn), lambda j: (0, j)),
            pl.BlockSpec((1, bn), lambda j: (0, j)),
        ],
        out_specs=pl.BlockSpec((B, bn), lambda j: (0, j)),
        compiler_params=pltpu.CompilerParams(
            dimension_semantics=("parallel",), vmem_limit_bytes=VMEM_LIMIT),
        name="adaln_modulation",
    )(c, w_ada, b_ada)


def _mixer_input(x, mod_ref, gmix_ref):
    D = D_MODEL
    shift = mod_ref[:, 0:D]
    scale = mod_ref[:, D:2 * D]
    return (_rms(x, gmix_ref[...]) * (1.0 + scale) + shift).astype(BF16)


def _first_grid_step():
    return jnp.logical_and(pl.program_id(0) == 0, pl.program_id(1) == 0)


def _transpose_rows(wt_ref, row0, n_rows, out_ref, col0):
    for j in range(n_rows // LANES):
        rows = wt_ref[row0 + j * LANES:row0 + (j + 1) * LANES, :]
        out_ref[:, col0 + j * LANES:col0 + (j + 1) * LANES] = rows.T.astype(BF16)


def _relayout_w_proj(wt_ref, out_ref):
    o_kr = Q_LORA + KV_LORA
    o_conv = o_kr + QK_ROPE
    _transpose_rows(wt_ref, 0, o_kr, out_ref, C_QA)
    k1 = wt_ref[o_kr:o_kr + HALF_ROPE, :]
    k2 = wt_ref[o_kr + HALF_ROPE:o_conv, :]
    key_tile_t = jnp.concatenate([jnp.zeros((QK_NOPE, k1.shape[1]), F32), k1, k2, -k2, k1], axis=0)
    out_ref[:, C_KR:C_KR + HEAD_PAD] = key_tile_t.T.astype(BF16)
    _transpose_rows(wt_ref, o_conv, 2 * CONV_CH, out_ref, C_CONV)


def _inproj_kernel(x_ref, mod_ref, gmix_ref, w_ref, gq_ref, wq_ref, gkv_ref, wk_ref, wvt_ref,
                   cosc_ref, sinc_ref, *rest, n_cast):
    cast_in = rest[:n_cast]
    q_ref, k_ref, vt_ref, z_ref, h_ref = rest[n_cast:n_cast + 5]
    cast_out = rest[n_cast + 5:2 * n_cast + 5]
    ctab_ref, stab_ref, win_ref = rest[2 * n_cast + 5:]
    for src, dst in zip(cast_in, cast_out):
        dst[...] = src[...].astype(BF16)
    pl.when(_first_grid_step())(lambda: _relayout_w_proj(w_ref, win_ref))
    h = _mixer_input(x_ref[...], mod_ref, gmix_ref)
    h_ref[...] = h
    _expand_rope_table(cosc_ref, ctab_ref, 1.0)
    _expand_rope_table(sinc_ref, stab_ref, 0.0)
    ctab = ctab_ref[...]
    stab = stab_ref[...]

    proj = _dot(h, win_ref[...])

    z_ref[...] = (proj[:, C_CONV:C_CONV + CONV_CH]
                  * _sigmoid(proj[:, C_CONV + CONV_CH:C_CONV + 2 * CONV_CH]))

    qn = _rms(proj[:, C_QA:C_QA + Q_LORA], gq_ref[...]).astype(BF16)
    q_all = _dot(qn, wq_ref[...])
    for hd in range(N_HEADS):
        sl = slice(hd * HEAD_PAD, (hd + 1) * HEAD_PAD)
        q_ref[:, sl] = (_rope_tile(q_all[:, sl], ctab, stab) * Q_SCALE).astype(BF16)

    kvn = _rms(proj[:, C_KVA:C_KVA + KV_LORA], gkv_ref[...]).astype(BF16)
    k_pad = _dot(kvn, wk_ref[...])
    vt_ref[...] = lax.dot_general(wvt_ref[...], kvn, (((1,), (1,)), ((), ())),
                                  preferred_element_type=F32).astype(BF16)
    kr_rot = _rope_tile(proj[:, C_KR:C_KR + HEAD_PAD], ctab, stab)
    for hd in range(N_HEADS):
        sl = slice(hd * HEAD_PAD, (hd + 1) * HEAD_PAD)
        k_ref[:, sl] = (k_pad[:, sl] + kr_rot).astype(BF16)


def _inproj(x, mod3, g_mix, w_in, g_q, wq2, g_kv, wk, wvt, cos_c, sin_c, cast_ws):
    B, S, D = x.shape
    ts = TS_IN
    nq = N_HEADS * HEAD_PAD
    nv = N_HEADS * V_HEAD
    tok = lambda w: pl.BlockSpec((None, ts, w), lambda b, i: (b, i, 0))
    rows = ts * HALF_ROPE // LANES
    n_i = S // ts
    compact = pl.BlockSpec((rows, LANES), lambda b, i: (b * n_i + i, 0))
    cast_specs = [_cast_chunk_spec(w.shape[0], w.shape[1], n_i, B * n_i) for w in cast_ws]
    return pl.pallas_call(
        functools.partial(_inproj_kernel, n_cast=len(cast_ws)),
        out_shape=(
            jax.ShapeDtypeStruct((B, S, nq), BF16),
            jax.ShapeDtypeStruct((B, S, nq), BF16),
            jax.ShapeDtypeStruct((B, nv, S), BF16),
            jax.ShapeDtypeStruct((B, S, CONV_CH), F32),
            jax.ShapeDtypeStruct((B, S, D), BF16),
            *[jax.ShapeDtypeStruct(w.shape, BF16) for w in cast_ws],
        ),
        grid=(B, S // ts),
        in_specs=[
            tok(D),
            pl.BlockSpec((None, 1, N_MOD * D), lambda b, i: (b, 0, 0)),
            _const_spec(g_mix.shape),
            _const_spec(w_in.shape),
            _const_spec(g_q.shape),
            _const_spec(wq2.shape),
            _const_spec(g_kv.shape),
            _const_spec(wk.shape),
            _const_spec(wvt.shape),
            compact,
            compact,
            *cast_specs,
        ],
        out_specs=(tok(nq), tok(nq), pl.BlockSpec((None, nv, ts), lambda b, i: (b, 0, i)),
                   tok(CONV_CH), tok(D), *cast_specs),
        scratch_shapes=[pltpu.VMEM((ts, HEAD_PAD), F32), pltpu.VMEM((ts, HEAD_PAD), F32),
                        pltpu.VMEM((D, C_CONV + 2 * CONV_CH), BF16)],
        compiler_params=pltpu.CompilerParams(
            dimension_semantics=("arbitrary", "arbitrary"), vmem_limit_bytes=VMEM_LIMIT),
        name="input_projection",
    )(x, mod3, g_mix, w_in, g_q, wq2, g_kv, wk, wvt, cos_c, sin_c, *cast_ws)


def _attn_kernel(q_ref, k_ref, vt_ref, o_ref):
    n_kc = k_ref.shape[0] // KEY_CHUNK

    def score_chunk(hd, c):
        sl = slice(hd * HEAD_PAD, (hd + 1) * HEAD_PAD)
        ks = slice(c * KEY_CHUNK, (c + 1) * KEY_CHUNK)
        return lax.dot_general(k_ref[ks, sl], q_ref[:, sl], (((1,), (1,)), ((), ())),
                               preferred_element_type=F32)

    def col_max(chunks):
        m = jnp.max(chunks[0], axis=0, keepdims=True)
        for st in chunks[1:]:
            m = jnp.maximum(m, jnp.max(st, axis=0, keepdims=True))
        return m

    st_next = [score_chunk(0, c) for c in range(n_kc)]
    outs = []
    for hd in range(N_HEADS):
        st_cur, st_next = st_next, []
        m = col_max(st_cur)
        l = acc = None
        for c in range(n_kc):
            if hd + 1 < N_HEADS:
                st_next.append(score_chunk(hd + 1, c))
            ks = slice(c * KEY_CHUNK, (c + 1) * KEY_CHUNK)
            p = jnp.exp2(st_cur[c] - m)
            ls = jnp.sum(p, axis=0, keepdims=True)
            pv = _dot(vt_ref[hd * V_HEAD:(hd + 1) * V_HEAD, ks], p.astype(BF16))
            l, acc = (ls, pv) if c == 0 else (l + ls, acc + pv)
        outs.append(acc / l)
        if hd % 2 == 1:
            o_pair = jnp.concatenate(outs, axis=0).T
            o_ref[:, (hd // 2) * 2 * V_HEAD:(hd // 2 + 1) * 2 * V_HEAD] = o_pair.astype(BF16)
            outs = []


def _attention(q, k, vt):
    B, S, nq = q.shape
    nv = N_HEADS * V_HEAD
    return pl.pallas_call(
        _attn_kernel,
        out_shape=jax.ShapeDtypeStruct((B, S, nv), BF16),
        grid=(B, S // TQ),
        in_specs=[
            pl.BlockSpec((None, TQ, nq), lambda b, i: (b, i, 0)),
            pl.BlockSpec((None, S, nq), lambda b, i: (b, 0, 0)),
            pl.BlockSpec((None, nv, S), lambda b, i: (b, 0, 0)),
        ],
        out_specs=pl.BlockSpec((None, TQ, nv), lambda b, i: (b, i, 0)),
        compiler_params=pltpu.CompilerParams(
            dimension_semantics=("parallel", "parallel"), vmem_limit_bytes=VMEM_LIMIT),
        name="mla_attention",
    )(q, k, vt)


def _relayout_w_gate(wt_ref, out_ref):
    o_gate = Q_LORA + KV_LORA + QK_ROPE + 2 * CONV_CH
    _transpose_rows(wt_ref, o_gate, out_ref.shape[1], out_ref, 0)


def _mix_kernel(x_ref, mod_ref, z_ref, zprev_ref, znext_ref, o_ref, h_ref, w_ref,
                wdw_ref, bdw_ref, gln_ref, bln_ref, wco_ref, wao_ref, wout_ref, *rest, n_cast):
    cast_in = rest[:n_cast]
    out_ref = rest[n_cast]
    cast_out = rest[n_cast + 1:2 * n_cast + 1]
    zp_ref, zs_ref, conv_ref, wgate_ref = rest[2 * n_cast + 1:]
    for src, dst in zip(cast_in, cast_out):
        dst[...] = src[...].astype(BF16)
    D = D_MODEL
    ts = TS_MIX
    i = pl.program_id(1)
    n_i = pl.num_programs(1)
    pl.when(_first_grid_step())(lambda: _relayout_w_gate(w_ref, wgate_ref))

    zp_ref[0:HALO, :] = jnp.where(i > 0, zprev_ref[...], 0.0)
    zp_ref[HALO:HALO + ts, :] = z_ref[...]
    zp_ref[HALO + ts:2 * HALO + ts, :] = jnp.where(i < n_i - 1, znext_ref[...], 0.0)

    n_shift = zs_ref.shape[1]
    for s in range(1, SUBLANES):
        zs_ref[s - 1] = zp_ref[s:s + n_shift, :]

    gate_logits = _dot(h_ref[...], wgate_ref[...])
    y_a = _dot(o_ref[...], wao_ref[...])

    row_chunk = 64
    base = HALO - CONV_K // 2
    for cb in range(CONV_CH // LANES):
        cs = slice(cb * LANES, (cb + 1) * LANES)
        for rb in range(ts // row_chunk):
            r0 = rb * row_chunk
            acc = jnp.broadcast_to(bdw_ref[:, cs], (row_chunk, LANES))
            for kk in range(CONV_K):
                s = (base + kk) % SUBLANES
                a = r0 + base + kk - s
                src = zp_ref if s == 0 else zs_ref.at[s - 1]
                acc = acc + wdw_ref[kk:kk + 1, cs] * src[a:a + row_chunk, cs]
            conv_ref[r0:r0 + row_chunk, cs] = acc

    zc = conv_ref[...]
    mu = jnp.mean(zc, axis=-1, keepdims=True)
    zd = zc - mu
    var = jnp.mean(zd * zd, axis=-1, keepdims=True)
    zn = zd * lax.rsqrt(var + EPS_LN) * gln_ref[...] + bln_ref[...]
    zs = (zn * _sigmoid(zn)).astype(BF16)
    y_b = _dot(zs, wco_ref[...])
    gates = _sigmoid(gate_logits)
    merged = (gates[:, 0:D] * y_a + gates[:, D:2 * D] * y_b).astype(BF16)
    gate_m = mod_ref[:, 2 * D:3 * D]
    out_ref[...] = x_ref[...] + gate_m * _dot(merged, wout_ref[...])


def _cast_chunk_spec(rows, cols, n_i, n_steps):
    every = 1
    while (rows * every) % n_steps or (rows * every // n_steps) % (2 * SUBLANES):
        every *= 2
    chunk = rows * every // n_steps
    return pl.BlockSpec((chunk, cols), lambda b, i: ((b * n_i + i) // every, 0))


def _mix(x, mod3, z, o, h, w_in, w_dw, b_dw, g_ln, b_ln, w_co, w_ao, w_out, cast_ws):
    B, S, D = x.shape
    ts = TS_MIX
    hb = ts // HALO
    n_halo = S // HALO
    n_i = S // ts
    tok = lambda w: pl.BlockSpec((None, ts, w), lambda b, i: (b, i, 0))
    cast_specs = [_cast_chunk_spec(w.shape[0], w.shape[1], n_i, B * n_i) for w in cast_ws]
    return pl.pallas_call(
        functools.partial(_mix_kernel, n_cast=len(cast_ws)),
        out_shape=(jax.ShapeDtypeStruct((B, S, D), F32),
                   *[jax.ShapeDtypeStruct(w.shape, BF16) for w in cast_ws]),
        grid=(B, S // ts),
        in_specs=[
            tok(D),
            pl.BlockSpec((None, 1, N_MOD * D), lambda b, i: (b, 0, 0)),
            tok(CONV_CH),
            pl.BlockSpec((None, HALO, CONV_CH), lambda b, i: (b, jnp.maximum(i * hb - 1, 0), 0)),
            pl.BlockSpec((None, HALO, CONV_CH),
                         lambda b, i: (b, jnp.minimum((i + 1) * hb, n_halo - 1), 0)),
            tok(N_HEADS * V_HEAD),
            tok(D),
            _const_spec(w_in.shape),
            _const_spec(w_dw.shape),
            _const_spec(b_dw.shape),
            _const_spec(g_ln.shape),
            _const_spec(b_ln.shape),
            _const_spec(w_co.shape),
            _const_spec(w_ao.shape),
            _const_spec(w_out.shape),
            *cast_specs,
        ],
        out_specs=(tok(D), *cast_specs),
        scratch_shapes=[
            pltpu.VMEM((ts + 2 * HALO, CONV_CH), F32),
            pltpu.VMEM((SUBLANES - 1, ts + 2 * HALO - SUBLANES, CONV_CH), F32),
            pltpu.VMEM((ts, CONV_CH), F32),
            pltpu.VMEM((D, 2 * D), BF16),
        ],
        compiler_params=pltpu.CompilerParams(
            dimension_semantics=("arbitrary", "arbitrary"), vmem_limit_bytes=VMEM_LIMIT),
        name="conv_merge_out",
    )(x, mod3, z, z, z, o, h, w_in, w_dw, b_dw, g_ln, b_ln, w_co, w_ao, w_out, *cast_ws)


def _ffn_kernel(x_ref, mod_ref, gffn_ref, wg_ref, wu_ref, wd_ref, gfin_ref, out_ref, *, final_norm):
    D = D_MODEL
    shift = mod_ref[:, 3 * D:4 * D]
    scale = mod_ref[:, 4 * D:5 * D]
    gate = mod_ref[:, 5 * D:6 * D]
    rows = x_ref.shape[0] // ROW_SUBTILES
    for j in range(ROW_SUBTILES):
        rs = slice(j * rows, (j + 1) * rows)
        x = x_ref[rs, :]
        h = (_rms(x, gffn_ref[...]) * (1.0 + scale) + shift).astype(BF16)
        g = _dot(h, wg_ref[...])
        u = _dot(h, wu_ref[...])
        a = (g * _sigmoid(g) * u).astype(BF16)
        x2 = x + gate * _dot(a, wd_ref[...])
        out_ref[rs, :] = _rms(x2, gfin_ref[...]) if final_norm else x2


def _ffn(x, mod3, g_ffn, w_gate, w_up, w_down, g_final, final_norm):
    B, S, D = x.shape
    tm = TM_FFN
    tok = pl.BlockSpec((None, tm, D), lambda b, i: (b, i, 0))
    return pl.pallas_call(
        functools.partial(_ffn_kernel, final_norm=final_norm),
        out_shape=jax.ShapeDtypeStruct((B, S, D), F32),
        grid=(B, S // tm),
        in_specs=[
            tok,
            pl.BlockSpec((None, 1, N_MOD * D), lambda b, i: (b, 0, 0)),
            _const_spec(g_ffn.shape),
            _const_spec(w_gate.shape),
            _const_spec(w_up.shape),
            _const_spec(w_down.shape),
            _const_spec(g_final.shape),
        ],
        out_specs=tok,
        compiler_params=pltpu.CompilerParams(
            dimension_semantics=("parallel", "parallel"), vmem_limit_bytes=VMEM_LIMIT),
        name="swiglu_final_norm",
    )(x, mod3, g_ffn, w_gate, w_up, w_down, g_final)


def _prep_w_q(w_q_up):
    r = w_q_up.shape[0]
    w = w_q_up.reshape(r, N_HEADS, QK_HEAD)
    nope = w[..., :QK_NOPE]
    r1 = w[..., QK_NOPE:QK_NOPE + HALF_ROPE]
    r2 = w[..., QK_NOPE + HALF_ROPE:]
    return jnp.concatenate([nope, r1, r2, -r2, r1], axis=-1).reshape(r, N_HEADS * HEAD_PAD).astype(BF16)


def _prep_w_kv(w_kv_up):
    r = w_kv_up.shape[0]
    w = w_kv_up.reshape(r, N_HEADS, QK_NOPE + V_HEAD)
    k_nope = w[..., :QK_NOPE]
    v = w[..., QK_NOPE:]
    k_pad = jnp.concatenate(
        [k_nope, jnp.zeros((r, N_HEADS, HEAD_PAD - QK_NOPE), w.dtype)], axis=-1)
    wk = k_pad.reshape(r, N_HEADS * HEAD_PAD).astype(BF16)
    wvt = v.reshape(r, N_HEADS * V_HEAD).T.astype(BF16)
    return wk, wvt


def kernel(x, c, positions, w_ada, b_ada, g_norm_mix, w_in, g_q_a, w_q_up, g_kv_a, w_kv_up,
           w_attn_o, w_dw, b_dw, g_conv_ln, b_conv_ln, w_conv_out, w_out, g_norm_ffn,
           w_ffn_gate, w_ffn_up, w_ffn_down, g_final):
    B, S, D = x.shape
    depth = w_ada.shape[0]
    cos_c, sin_c = _rope_tables(positions)
    for l in range(depth):
        mod3 = _modulation(c, w_ada[l], b_ada[l][None, :]).reshape(B, 1, N_MOD * D)
        wk, wvt = _prep_w_kv(w_kv_up[l])
        w_in_t = w_in[l].T
        q, k, vt, z, h, w_co_b, w_ao_b, w_out_b = _inproj(
            x, mod3, g_norm_mix[l][None, :], w_in_t, g_q_a[l][None, :],
            _prep_w_q(w_q_up[l]), g_kv_a[l][None, :], wk, wvt, cos_c, sin_c,
            (w_conv_out[l], w_attn_o[l], w_out[l]))
        o = _attention(q, k, vt)
        x, w_gate_b, w_up_b, w_down_b = _mix(
            x, mod3, z, o, h, w_in_t, w_dw[l], b_dw[l][None, :], g_conv_ln[l][None, :],
            b_conv_ln[l][None, :], w_co_b, w_ao_b, w_out_b,
            (w_ffn_gate[l], w_ffn_up[l], w_ffn_down[l]))
        x = _ffn(x, mod3, g_norm_ffn[l][None, :], w_gate_b, w_up_b, w_down_b, g_final[None, :],
                 final_norm=(l == depth - 1))
    return x
```

```python
import functools
import math

import jax
import jax.numpy as jnp
from jax import lax
from jax.experimental import pallas as pl
from jax.experimental.pallas import tpu as pltpu

F32 = jnp.float32
BF16 = jnp.bfloat16

D_MODEL = 1024
N_HEADS = 8
Q_LORA = 256
KV_LORA = 128
QK_NOPE = 64
QK_ROPE = 32
HALF_ROPE = QK_ROPE // 2
V_HEAD = 64
QK_HEAD = QK_NOPE + QK_ROPE
ATTN_SCALE = 1.0 / math.sqrt(QK_HEAD)
Q_SCALE = ATTN_SCALE * math.log2(math.e)
ROPE_THETA = 10000.0
CONV_CH = 512
CONV_K = 31
N_MOD = 6
EPS_RMS = 1e-6
EPS_LN = 1e-5

LANES = 128
SUBLANES = 8
HEAD_PAD = LANES
VMEM_LIMIT = 56 * 1024 * 1024

C_QA = 0
C_KVA = C_QA + Q_LORA
C_KR = C_KVA + KV_LORA
C_CONV = C_KR + HEAD_PAD

TS_IN = 1024
TQ = 512
KEY_CHUNK = 1024
TS_MIX = 512
HALO = 16
TM_FFN = 512
ROW_SUBTILES = 2


def _sigmoid(x):
    return 1.0 / (1.0 + jnp.exp(-x))


def _rms(x, g):
    return x * lax.rsqrt(jnp.mean(x * x, axis=-1, keepdims=True) + EPS_RMS) * g


def _dot(a, b):
    return jnp.dot(a, b, preferred_element_type=F32)


def _const_spec(shape):
    nd = len(shape)
    return pl.BlockSpec(shape, lambda *_: (0,) * nd, pipeline_mode=pl.Buffered(1))


def _rope_kernel(pos_ref, freq_ref, cos_ref, sin_ref):
    ang = pos_ref[...] * freq_ref[...]
    cos_ref[...] = jnp.cos(ang)
    sin_ref[...] = jnp.sin(ang)


def _rope_tables(positions):
    B, S = positions.shape
    inv_freq = ROPE_THETA ** (-jnp.arange(0, QK_ROPE, 2, dtype=F32) / QK_ROPE)
    rows = B * S * HALF_ROPE // LANES
    pos_rep = jnp.repeat(positions.reshape(-1).astype(F32), HALF_ROPE).reshape(rows, LANES)
    freq = jnp.tile(inv_freq, LANES // HALF_ROPE)[None, :]
    cos, sin = pl.pallas_call(
        _rope_kernel,
        out_shape=(jax.ShapeDtypeStruct((rows, LANES), F32),) * 2,
        name="rope_tables",
    )(pos_rep, freq)
    return cos, sin


def _expand_rope_table(compact_ref, out_ref, nope_value):
    pos_per_row = LANES // HALF_ROPE
    n = compact_ref.shape[0]
    comp = compact_ref[...]
    lane = lax.broadcasted_iota(jnp.int32, (n, LANES), 1)
    for j in range(pos_per_row):
        shift = (QK_NOPE - HALF_ROPE * j) % LANES
        first = pltpu.roll(comp, shift, 1) if shift else comp
        second = pltpu.roll(first, HALF_ROPE, 1)
        out_ref[pl.ds(j, n, stride=pos_per_row), :] = jnp.where(
            lane < QK_NOPE, nope_value,
            jnp.where(lane < QK_NOPE + HALF_ROPE, first,
                      jnp.where(lane < QK_HEAD, second, 0.0)))


def _rope_tile(t, ctab, stab):
    swapped = pltpu.roll(t, HEAD_PAD - QK_ROPE, 1)
    return t * ctab + swapped * stab


def _mod_kernel(c_ref, w_ref, b_ref, o_ref):
    c = c_ref[...]
    c_act = (c * _sigmoid(c)).astype(BF16)
    o_ref[...] = _dot(c_act, w_ref[...].astype(BF16)) + b_ref[...]


def _modulation(c, w_ada, b_ada):
    B, D = c.shape
    n = w_ada.shape[1]
    bn = 1536
    return pl.pallas_call(
        _mod_kernel,
        out_shape=jax.ShapeDtypeStruct((B, n), F32),
        grid=(n // bn,),
        in_specs=[
            pl.BlockSpec((B, D), lambda j: (0, 0)),
            pl.BlockSpec((D, bn), lambda j: (0, j)),
            pl.BlockSpec((1, bn), lambda j: (0, j)),
        ],
        out_specs=pl.BlockSpec((B, bn), lambda j: (0, j)),
        compiler_params=pltpu.CompilerParams(
            dimension_semantics=("parallel",), vmem_limit_bytes=VMEM_LIMIT),
        name="adaln_modulation",
    )(c, w_ada, b_ada)


def _mixer_input(x, mod_ref, gmix_ref):
    D = D_MODEL
    shift = mod_ref[:, 0:D]
    scale = mod_ref[:, D:2 * D]
    return (_rms(x, gmix_ref[...]) * (1.0 + scale) + shift).astype(BF16)


def _first_grid_step():
    return jnp.logical_and(pl.program_id(0) == 0, pl.program_id(1) == 0)


def _transpose_rows(wt_ref, row0, n_rows, out_ref, col0):
    for j in range(n_rows // LANES):
        rows = wt_ref[row0 + j * LANES:row0 + (j + 1) * LANES, :]
        out_ref[:, col0 + j * LANES:col0 + (j + 1) * LANES] = rows.T.astype(BF16)


def _relayout_w_proj(wt_ref, out_ref):
    o_kr = Q_LORA + KV_LORA
    o_conv = o_kr + QK_ROPE
    _transpose_rows(wt_ref, 0, o_kr, out_ref, C_QA)
    k1 = wt_ref[o_kr:o_kr + HALF_ROPE, :]
    k2 = wt_ref[o_kr + HALF_ROPE:o_conv, :]
    key_tile_t = jnp.concatenate([jnp.zeros((QK_NOPE, k1.shape[1]), F32), k1, k2, -k2, k1], axis=0)
    out_ref[:, C_KR:C_KR + HEAD_PAD] = key_tile_t.T.astype(BF16)
    _transpose_rows(wt_ref, o_conv, 2 * CONV_CH, out_ref, C_CONV)


def _inproj_kernel(x_ref, mod_ref, gmix_ref, w_ref, gq_ref, wq_ref, gkv_ref, wk_ref, wvt_ref,
                   cosc_ref, sinc_ref, *rest, n_cast):
    cast_in = rest[:n_cast]
    q_ref, k_ref, vt_ref, z_ref, h_ref = rest[n_cast:n_cast + 5]
    cast_out = rest[n_cast + 5:2 * n_cast + 5]
    ctab_ref, stab_ref, win_ref = rest[2 * n_cast + 5:]
    for src, dst in zip(cast_in, cast_out):
        dst[...] = src[...].astype(BF16)
    pl.when(_first_grid_step())(lambda: _relayout_w_proj(w_ref, win_ref))
    h = _mixer_input(x_ref[...], mod_ref, gmix_ref)
    h_ref[...] = h
    _expand_rope_table(cosc_ref, ctab_ref, 1.0)
    _expand_rope_table(sinc_ref, stab_ref, 0.0)
    ctab = ctab_ref[...]
    stab = stab_ref[...]

    proj = _dot(h, win_ref[...])

    z_ref[...] = (proj[:, C_CONV:C_CONV + CONV_CH]
                  * _sigmoid(proj[:, C_CONV + CONV_CH:C_CONV + 2 * CONV_CH]))

    qn = _rms(proj[:, C_QA:C_QA + Q_LORA], gq_ref[...]).astype(BF16)
    q_all = _dot(qn, wq_ref[...])
    for hd in range(N_HEADS):
        sl = slice(hd * HEAD_PAD, (hd + 1) * HEAD_PAD)
        q_ref[sl, :] = (_rope_tile(q_all[:, sl], ctab, stab) * Q_SCALE).T.astype(BF16)

    kvn = _rms(proj[:, C_KVA:C_KVA + KV_LORA], gkv_ref[...]).astype(BF16)
    k_pad = _dot(kvn, wk_ref[...])
    vt_ref[...] = lax.dot_general(wvt_ref[...], kvn, (((1,), (1,)), ((), ())),
                                  preferred_element_type=F32).astype(BF16)
    kr_rot = _rope_tile(proj[:, C_KR:C_KR + HEAD_PAD], ctab, stab)
    for hd in range(N_HEADS):
        sl = slice(hd * HEAD_PAD, (hd + 1) * HEAD_PAD)
        k_ref[:, sl] = (k_pad[:, sl] + kr_rot).astype(BF16)


def _inproj(x, mod3, g_mix, w_in, g_q, wq2, g_kv, wk, wvt, cos_c, sin_c, cast_ws):
    B, S, D = x.shape
    ts = TS_IN
    nq = N_HEADS * HEAD_PAD
    nv = N_HEADS * V_HEAD
    tok = lambda w: pl.BlockSpec((None, ts, w), lambda b, i: (b, i, 0))
    rows = ts * HALF_ROPE // LANES
    n_i = S // ts
    compact = pl.BlockSpec((rows, LANES), lambda b, i: (b * n_i + i, 0))
    cast_specs = [_cast_chunk_spec(w.shape[0], w.shape[1], n_i, B * n_i) for w in cast_ws]
    return pl.pallas_call(
        functools.partial(_inproj_kernel, n_cast=len(cast_ws)),
        out_shape=(
            jax.ShapeDtypeStruct((B, nq, S), BF16),
            jax.ShapeDtypeStruct((B, S, nq), BF16),
            jax.ShapeDtypeStruct((B, nv, S), BF16),
            jax.ShapeDtypeStruct((B, S, CONV_CH), F32),
            jax.ShapeDtypeStruct((B, S, D), BF16),
            *[jax.ShapeDtypeStruct(w.shape, BF16) for w in cast_ws],
        ),
        grid=(B, S // ts),
        in_specs=[
            tok(D),
            pl.BlockSpec((None, 1, N_MOD * D), lambda b, i: (b, 0, 0)),
            _const_spec(g_mix.shape),
            _const_spec(w_in.shape),
            _const_spec(g_q.shape),
            _const_spec(wq2.shape),
            _const_spec(g_kv.shape),
            _const_spec(wk.shape),
            _const_spec(wvt.shape),
            compact,
            compact,
            *cast_specs,
        ],
        out_specs=(pl.BlockSpec((None, nq, ts), lambda b, i: (b, 0, i)), tok(nq),
                   pl.BlockSpec((None, nv, ts), lambda b, i: (b, 0, i)),
                   tok(CONV_CH), tok(D), *cast_specs),
        scratch_shapes=[pltpu.VMEM((ts, HEAD_PAD), F32), pltpu.VMEM((ts, HEAD_PAD), F32),
                        pltpu.VMEM((D, C_CONV + 2 * CONV_CH), BF16)],
        compiler_params=pltpu.CompilerParams(
            dimension_semantics=("arbitrary", "arbitrary"), vmem_limit_bytes=VMEM_LIMIT),
        name="input_projection",
    )(x, mod3, g_mix, w_in, g_q, wq2, g_kv, wk, wvt, cos_c, sin_c, *cast_ws)


def _attn_kernel(qt_ref, k_ref, vt_ref, o_ref):
    n_kc = k_ref.shape[0] // KEY_CHUNK

    def score_chunk(hd, c):
        sl = slice(hd * HEAD_PAD, (hd + 1) * HEAD_PAD)
        ks = slice(c * KEY_CHUNK, (c + 1) * KEY_CHUNK)
        return _dot(k_ref[ks, sl], qt_ref[sl, :])

    def col_max(chunks):
        m = jnp.max(chunks[0], axis=0, keepdims=True)
        for st in chunks[1:]:
            m = jnp.maximum(m, jnp.max(st, axis=0, keepdims=True))
        return m

    st_next = [score_chunk(0, c) for c in range(n_kc)]
    outs = []
    for hd in range(N_HEADS):
        st_cur, st_next = st_next, []
        m = col_max(st_cur)
        l = acc = None
        for c in range(n_kc):
            if hd + 1 < N_HEADS:
                st_next.append(score_chunk(hd + 1, c))
            ks = slice(c * KEY_CHUNK, (c + 1) * KEY_CHUNK)
            p = jnp.exp2(st_cur[c] - m)
            ls = jnp.sum(p, axis=0, keepdims=True)
            pv = _dot(vt_ref[hd * V_HEAD:(hd + 1) * V_HEAD, ks], p.astype(BF16))
            l, acc = (ls, pv) if c == 0 else (l + ls, acc + pv)
        outs.append(acc / l)
        if hd % 2 == 1:
            o_pair = jnp.concatenate(outs, axis=0).T
            o_ref[:, (hd // 2) * 2 * V_HEAD:(hd // 2 + 1) * 2 * V_HEAD] = o_pair.astype(BF16)
            outs = []


def _attention(qt, k, vt):
    B, nq, S = qt.shape
    nv = N_HEADS * V_HEAD
    return pl.pallas_call(
        _attn_kernel,
        out_shape=jax.ShapeDtypeStruct((B, S, nv), BF16),
        grid=(B, S // TQ),
        in_specs=[
            pl.BlockSpec((None, nq, TQ), lambda b, i: (b, 0, i)),
            pl.BlockSpec((None, S, nq), lambda b, i: (b, 0, 0)),
            pl.BlockSpec((None, nv, S), lambda b, i: (b, 0, 0)),
        ],
        out_specs=pl.BlockSpec((None, TQ, nv), lambda b, i: (b, i, 0)),
        compiler_params=pltpu.CompilerParams(
            dimension_semantics=("parallel", "parallel"), vmem_limit_bytes=VMEM_LIMIT),
        name="mla_attention",
    )(qt, k, vt)


def _relayout_w_gate(wt_ref, out_ref):
    o_gate = Q_LORA + KV_LORA + QK_ROPE + 2 * CONV_CH
    _transpose_rows(wt_ref, o_gate, out_ref.shape[1], out_ref, 0)


def _mix_kernel(x_ref, mod_ref, z_ref, zprev_ref, znext_ref, o_ref, h_ref, w_ref,
                wdw_ref, bdw_ref, gln_ref, bln_ref, wco_ref, wao_ref, wout_ref, *rest, n_cast):
    cast_in = rest[:n_cast]
    out_ref = rest[n_cast]
    cast_out = rest[n_cast + 1:2 * n_cast + 1]
    zp_ref, zs_ref, conv_ref, wgate_ref = rest[2 * n_cast + 1:]
    for src, dst in zip(cast_in, cast_out):
        dst[...] = src[...].astype(BF16)
    D = D_MODEL
    ts = TS_MIX
    i = pl.program_id(1)
    n_i = pl.num_programs(1)
    pl.when(_first_grid_step())(lambda: _relayout_w_gate(w_ref, wgate_ref))

    zp_ref[0:HALO, :] = jnp.where(i > 0, zprev_ref[...], 0.0)
    zp_ref[HALO:HALO + ts, :] = z_ref[...]
    zp_ref[HALO + ts:2 * HALO + ts, :] = jnp.where(i < n_i - 1, znext_ref[...], 0.0)

    n_shift = zs_ref.shape[1]
    for s in range(1, SUBLANES):
        zs_ref[s - 1] = zp_ref[s:s + n_shift, :]

    gate_logits = _dot(h_ref[...], wgate_ref[...])
    y_a = _dot(o_ref[...], wao_ref[...])

    row_chunk = 64
    base = HALO - CONV_K // 2
    for cb in range(CONV_CH // LANES):
        cs = slice(cb * LANES, (cb + 1) * LANES)
        for rb in range(ts // row_chunk):
            r0 = rb * row_chunk
            acc = jnp.broadcast_to(bdw_ref[:, cs], (row_chunk, LANES))
            for kk in range(CONV_K):
                s = (base + kk) % SUBLANES
                a = r0 + base + kk - s
                src = zp_ref if s == 0 else zs_ref.at[s - 1]
                acc = acc + wdw_ref[kk:kk + 1, cs] * src[a:a + row_chunk, cs]
            conv_ref[r0:r0 + row_chunk, cs] = acc

    zc = conv_ref[...]
    mu = jnp.mean(zc, axis=-1, keepdims=True)
    zd = zc - mu
    var = jnp.mean(zd * zd, axis=-1, keepdims=True)
    zn = zd * lax.rsqrt(var + EPS_LN) * gln_ref[...] + bln_ref[...]
    zs = (zn * _sigmoid(zn)).astype(BF16)
    y_b = _dot(zs, wco_ref[...])
    gates = _sigmoid(gate_logits)
    merged = (gates[:, 0:D] * y_a + gates[:, D:2 * D] * y_b).astype(BF16)
    gate_m = mod_ref[:, 2 * D:3 * D]
    out_ref[...] = x_ref[...] + gate_m * _dot(merged, wout_ref[...])


def _cast_chunk_spec(rows, cols, n_i, n_steps):
    every = 1
    while (rows * every) % n_steps or (rows * every // n_steps) % (2 * SUBLANES):
        every *= 2
    chunk = rows * every // n_steps
    return pl.BlockSpec((chunk, cols), lambda b, i: ((b * n_i + i) // every, 0))


def _mix(x, mod3, z, o, h, w_in, w_dw, b_dw, g_ln, b_ln, w_co, w_ao, w_out, cast_ws):
    B, S, D = x.shape
    ts = TS_MIX
    hb = ts // HALO
    n_halo = S // HALO
    n_i = S // ts
    tok = lambda w: pl.BlockSpec((None, ts, w), lambda b, i: (b, i, 0))
    cast_specs = [_cast_chunk_spec(w.shape[0], w.shape[1], n_i, B * n_i) for w in cast_ws]
    return pl.pallas_call(
        functools.partial(_mix_kernel, n_cast=len(cast_ws)),
        out_shape=(jax.ShapeDtypeStruct((B, S, D), F32),
                   *[jax.ShapeDtypeStruct(w.shape, BF16) for w in cast_ws]),
        grid=(B, S // ts),
        in_specs=[
            tok(D),
            pl.BlockSpec((None, 1, N_MOD * D), lambda b, i: (b, 0, 0)),
            tok(CONV_CH),
            pl.BlockSpec((None, HALO, CONV_CH), lambda b, i: (b, jnp.maximum(i * hb - 1, 0), 0)),
            pl.BlockSpec((None, HALO, CONV_CH),
                         lambda b, i: (b, jnp.minimum((i + 1) * hb, n_halo - 1), 0)),
            tok(N_HEADS * V_HEAD),
            tok(D),
            _const_spec(w_in.shape),
            _const_spec(w_dw.shape),
            _const_spec(b_dw.shape),
            _const_spec(g_ln.shape),
            _const_spec(b_ln.shape),
            _const_spec(w_co.shape),
            _const_spec(w_ao.shape),
            _const_spec(w_out.shape),
            *cast_specs,
        ],
        out_specs=(tok(D), *cast_specs),
        scratch_shapes=[
            pltpu.VMEM((ts + 2 * HALO, CONV_CH), F32),
            pltpu.VMEM((SUBLANES - 1, ts + 2 * HALO - SUBLANES, CONV_CH), F32),
            pltpu.VMEM((ts, CONV_CH), F32),
            pltpu.VMEM((D, 2 * D), BF16),
        ],
        compiler_params=pltpu.CompilerParams(
            dimension_semantics=("arbitrary", "arbitrary"), vmem_limit_bytes=VMEM_LIMIT),
        name="conv_merge_out",
    )(x, mod3, z, z, z, o, h, w_in, w_dw, b_dw, g_ln, b_ln, w_co, w_ao, w_out, *cast_ws)


def _ffn_kernel(x_ref, mod_ref, gffn_ref, wg_ref, wu_ref, wd_ref, gfin_ref, out_ref, *, final_norm):
    D = D_MODEL
    shift = mod_ref[:, 3 * D:4 * D]
    scale = mod_ref[:, 4 * D:5 * D]
    gate = mod_ref[:, 5 * D:6 * D]
    rows = x_ref.shape[0] // ROW_SUBTILES
    for j in range(ROW_SUBTILES):
        rs = slice(j * rows, (j + 1) * rows)
        x = x_ref[rs, :]
        h = (_rms(x, gffn_ref[...]) * (1.0 + scale) + shift).astype(BF16)
        g = _dot(h, wg_ref[...])
        u = _dot(h, wu_ref[...])
        a = (g * _sigmoid(g) * u).astype(BF16)
        x2 = x + gate * _dot(a, wd_ref[...])
        out_ref[rs, :] = _rms(x2, gfin_ref[...]) if final_norm else x2


def _ffn(x, mod3, g_ffn, w_gate, w_up, w_down, g_final, final_norm):
    B, S, D = x.shape
    tm = TM_FFN
    tok = pl.BlockSpec((None, tm, D), lambda b, i: (b, i, 0))
    return pl.pallas_call(
        functools.partial(_ffn_kernel, final_norm=final_norm),
        out_shape=jax.ShapeDtypeStruct((B, S, D), F32),
        grid=(B, S // tm),
        in_specs=[
            tok,
            pl.BlockSpec((None, 1, N_MOD * D), lambda b, i: (b, 0, 0)),
            _const_spec(g_ffn.shape),
            _const_spec(w_gate.shape),
            _const_spec(w_up.shape),
            _const_spec(w_down.shape),
            _const_spec(g_final.shape),
        ],
        out_specs=tok,
        compiler_params=pltpu.CompilerParams(
            dimension_semantics=("parallel", "parallel"), vmem_limit_bytes=VMEM_LIMIT),
        name="swiglu_final_norm",
    )(x, mod3, g_ffn, w_gate, w_up, w_down, g_final)


def _prep_w_q(w_q_up):
    r = w_q_up.shape[0]
    w = w_q_up.reshape(r, N_HEADS, QK_HEAD)
    nope = w[..., :QK_NOPE]
    r1 = w[..., QK_NOPE:QK_NOPE + HALF_ROPE]
    r2 = w[..., QK_NOPE + HALF_ROPE:]
    return jnp.concatenate([nope, r1, r2, -r2, r1], axis=-1).reshape(r, N_HEADS * HEAD_PAD).astype(BF16)


def _prep_w_kv(w_kv_up):
    r = w_kv_up.shape[0]
    w = w_kv_up.reshape(r, N_HEADS, QK_NOPE + V_HEAD)
    k_nope = w[..., :QK_NOPE]
    v = w[..., QK_NOPE:]
    k_pad = jnp.concatenate(
        [k_nope, jnp.zeros((r, N_HEADS, HEAD_PAD - QK_NOPE), w.dtype)], axis=-1)
    wk = k_pad.reshape(r, N_HEADS * HEAD_PAD).astype(BF16)
    wvt = v.reshape(r, N_HEADS * V_HEAD).T.astype(BF16)
    return wk, wvt


def kernel(x, c, positions, w_ada, b_ada, g_norm_mix, w_in, g_q_a, w_q_up, g_kv_a, w_kv_up,
           w_attn_o, w_dw, b_dw, g_conv_ln, b_conv_ln, w_conv_out, w_out, g_norm_ffn,
           w_ffn_gate, w_ffn_up, w_ffn_down, g_final):
    B, S, D = x.shape
    depth = w_ada.shape[0]
    cos_c, sin_c = _rope_tables(positions)
    for l in range(depth):
        mod3 = _modulation(c, w_ada[l], b_ada[l][None, :]).reshape(B, 1, N_MOD * D)
        wk, wvt = _prep_w_kv(w_kv_up[l])
        w_in_t = w_in[l].T
        q, k, vt, z, h, w_co_b, w_ao_b, w_out_b = _inproj(
            x, mod3, g_norm_mix[l][None, :], w_in_t, g_q_a[l][None, :],
            _prep_w_q(w_q_up[l]), g_kv_a[l][None, :], wk, wvt, cos_c, sin_c,
            (w_conv_out[l], w_attn_o[l], w_out[l]))
        o = _attention(q, k, vt)
        x, w_gate_b, w_up_b, w_down_b = _mix(
            x, mod3, z, o, h, w_in_t, w_dw[l], b_dw[l][None, :], g_conv_ln[l][None, :],
            b_conv_ln[l][None, :], w_co_b, w_ao_b, w_out_b,
            (w_ffn_gate[l], w_ffn_up[l], w_ffn_down[l]))
        x = _ffn(x, mod3, g_norm_ffn[l][None, :], w_gate_b, w_up_b, w_down_b, g_final[None, :],
                 final_norm=(l == depth - 1))
    return x
```

```python
import functools
import math

import jax
import jax.numpy as jnp
from jax import lax
from jax.experimental import pallas as pl
from jax.experimental.pallas import tpu as pltpu

F32 = jnp.float32
BF16 = jnp.bfloat16

D_MODEL = 1024
N_HEADS = 8
Q_LORA = 256
KV_LORA = 128
QK_NOPE = 64
QK_ROPE = 32
HALF_ROPE = QK_ROPE // 2
V_HEAD = 64
QK_HEAD = QK_NOPE + QK_ROPE
ATTN_SCALE = 1.0 / math.sqrt(QK_HEAD)
Q_SCALE = ATTN_SCALE * math.log2(math.e)
ROPE_THETA = 10000.0
CONV_CH = 512
CONV_K = 31
N_MOD = 6
EPS_RMS = 1e-6
EPS_LN = 1e-5

LANES = 128
SUBLANES = 8
HEAD_PAD = LANES
VMEM_LIMIT = 56 * 1024 * 1024

C_QA = 0
C_KVA = C_QA + Q_LORA
C_KR = C_KVA + KV_LORA
C_CONV = C_KR + HEAD_PAD

TS_IN = 1024
TQ = 512
KEY_CHUNK = 1024
TS_MIX = 512
HALO = 16
TM_FFN = 512
ROW_SUBTILES = 2


def _sigmoid(x):
    return 1.0 / (1.0 + jnp.exp(-x))


def _rms(x, g):
    return x * lax.rsqrt(jnp.mean(x * x, axis=-1, keepdims=True) + EPS_RMS) * g


def _dot(a, b):
    return jnp.dot(a, b, preferred_element_type=F32)


def _const_spec(shape):
    nd = len(shape)
    return pl.BlockSpec(shape, lambda *_: (0,) * nd, pipeline_mode=pl.Buffered(1))


def _rope_kernel(pos_ref, freq_ref, cos_ref, sin_ref):
    ang = pos_ref[...] * freq_ref[...]
    cos_ref[...] = jnp.cos(ang)
    sin_ref[...] = jnp.sin(ang)


def _rope_tables(positions):
    B, S = positions.shape
    inv_freq = ROPE_THETA ** (-jnp.arange(0, QK_ROPE, 2, dtype=F32) / QK_ROPE)
    rows = B * S * HALF_ROPE // LANES
    pos_rep = jnp.repeat(positions.reshape(-1).astype(F32), HALF_ROPE).reshape(rows, LANES)
    freq = jnp.tile(inv_freq, LANES // HALF_ROPE)[None, :]
    cos, sin = pl.pallas_call(
        _rope_kernel,
        out_shape=(jax.ShapeDtypeStruct((rows, LANES), F32),) * 2,
        name="rope_tables",
    )(pos_rep, freq)
    return cos, sin


def _expand_rope_table(compact_ref, out_ref, nope_value):
    pos_per_row = LANES // HALF_ROPE
    n = compact_ref.shape[0]
    comp = compact_ref[...]
    lane = lax.broadcasted_iota(jnp.int32, (n, LANES), 1)
    for j in range(pos_per_row):
        shift = (QK_NOPE - HALF_ROPE * j) % LANES
        first = pltpu.roll(comp, shift, 1) if shift else comp
        second = pltpu.roll(first, HALF_ROPE, 1)
        out_ref[pl.ds(j, n, stride=pos_per_row), :] = jnp.where(
            lane < QK_NOPE, nope_value,
            jnp.where(lane < QK_NOPE + HALF_ROPE, first,
                      jnp.where(lane < QK_HEAD, second, 0.0)))


def _rope_tile(t, ctab, stab):
    swapped = pltpu.roll(t, HEAD_PAD - QK_ROPE, 1)
    return t * ctab + swapped * stab


def _mod_kernel(c_ref, w_ref, b_ref, o_ref):
    c = c_ref[...]
    c_act = (c * _sigmoid(c)).astype(BF16)
    o_ref[...] = _dot(c_act, w_ref[...].astype(BF16)) + b_ref[...]


def _modulation(c, w_ada, b_ada):
    B, D = c.shape
    n = w_ada.shape[1]
    bn = 1536
    return pl.pallas_call(
        _mod_kernel,
        out_shape=jax.ShapeDtypeStruct((B, n), F32),
        grid=(n // bn,),
        in_specs=[
            pl.BlockSpec((B, D), lambda j: (0, 0)),
            pl.BlockSpec((D, bn), lambda j: (0, j)),
            pl.BlockSpec((1, bn), lambda j: (0, j)),
        ],
        out_specs=pl.BlockSpec((B, bn), lambda j: (0, j)),
        compiler_params=pltpu.CompilerParams(
            dimension_semantics=("parallel",), vmem_limit_bytes=VMEM_LIMIT),
        name="adaln_modulation",
    )(c, w_ada, b_ada)


def _mixer_input(x, mod_ref, gmix_ref):
    D = D_MODEL
    shift = mod_ref[:, 0:D]
    scale = mod_ref[:, D:2 * D]
    return (_rms(x, gmix_ref[...]) * (1.0 + scale) + shift).astype(BF16)


def _first_grid_step():
    return jnp.logical_and(pl.program_id(0) == 0, pl.program_id(1) == 0)


def _transpose_rows(wt_ref, row0, n_rows, out_ref, col0):
    for j in range(n_rows // LANES):
        rows = wt_ref[row0 + j * LANES:row0 + (j + 1) * LANES, :]
        out_ref[:, col0 + j * LANES:col0 + (j + 1) * LANES] = rows.T.astype(BF16)


def _relayout_w_proj(wt_ref, out_ref):
    o_kr = Q_LORA + KV_LORA
    o_conv = o_kr + QK_ROPE
    _transpose_rows(wt_ref, 0, o_kr, out_ref, C_QA)
    k1 = wt_ref[o_kr:o_kr + HALF_ROPE, :]
    k2 = wt_ref[o_kr + HALF_ROPE:o_conv, :]
    key_tile_t = jnp.concatenate([jnp.zeros((QK_NOPE, k1.shape[1]), F32), k1, k2, -k2, k1], axis=0)
    out_ref[:, C_KR:C_KR + HEAD_PAD] = key_tile_t.T.astype(BF16)
    _transpose_rows(wt_ref, o_conv, 2 * CONV_CH, out_ref, C_CONV)


def _inproj_kernel(x_ref, mod_ref, gmix_ref, w_ref, gq_ref, wq_ref, gkv_ref, wk_ref, wvt_ref,
                   cosc_ref, sinc_ref, *rest, n_cast):
    cast_in = rest[:n_cast]
    q_ref, k_ref, vt_ref, z_ref, h_ref = rest[n_cast:n_cast + 5]
    cast_out = rest[n_cast + 5:2 * n_cast + 5]
    ctab_ref, stab_ref, win_ref = rest[2 * n_cast + 5:]
    for src, dst in zip(cast_in, cast_out):
        dst[...] = src[...].astype(BF16)
    pl.when(_first_grid_step())(lambda: _relayout_w_proj(w_ref, win_ref))
    h = _mixer_input(x_ref[...], mod_ref, gmix_ref)
    h_ref[...] = h
    _expand_rope_table(cosc_ref, ctab_ref, 1.0)
    _expand_rope_table(sinc_ref, stab_ref, 0.0)
    ctab = ctab_ref[...]
    stab = stab_ref[...]

    proj = _dot(h, win_ref[...])

    z_ref[...] = (proj[:, C_CONV:C_CONV + CONV_CH]
                  * _sigmoid(proj[:, C_CONV + CONV_CH:C_CONV + 2 * CONV_CH]))

    qn = _rms(proj[:, C_QA:C_QA + Q_LORA], gq_ref[...]).astype(BF16)
    q_all_t = lax.dot_general(wq_ref[...], qn, (((1,), (1,)), ((), ())), preferred_element_type=F32)
    ctab_t = ctab.T
    stab_t = stab.T
    for hd in range(N_HEADS):
        sl = slice(hd * HEAD_PAD, (hd + 1) * HEAD_PAD)
        t = q_all_t[sl, :]
        swapped = jnp.concatenate([t[0:QK_NOPE], t[QK_HEAD:HEAD_PAD], t[QK_HEAD:HEAD_PAD]], axis=0)
        q_ref[sl, :] = ((t * ctab_t + swapped * stab_t) * Q_SCALE).astype(BF16)

    kvn = _rms(proj[:, C_KVA:C_KVA + KV_LORA], gkv_ref[...]).astype(BF16)
    k_pad = _dot(kvn, wk_ref[...])
    vt_ref[...] = lax.dot_general(wvt_ref[...], kvn, (((1,), (1,)), ((), ())),
                                  preferred_element_type=F32).astype(BF16)
    kr_rot = _rope_tile(proj[:, C_KR:C_KR + HEAD_PAD], ctab, stab)
    for hd in range(N_HEADS):
        sl = slice(hd * HEAD_PAD, (hd + 1) * HEAD_PAD)
        k_ref[:, sl] = (k_pad[:, sl] + kr_rot).astype(BF16)


def _inproj(x, mod3, g_mix, w_in, g_q, wq2, g_kv, wk, wvt, cos_c, sin_c, cast_ws):
    B, S, D = x.shape
    ts = TS_IN
    nq = N_HEADS * HEAD_PAD
    nv = N_HEADS * V_HEAD
    tok = lambda w: pl.BlockSpec((None, ts, w), lambda b, i: (b, i, 0))
    rows = ts * HALF_ROPE // LANES
    n_i = S // ts
    compact = pl.BlockSpec((rows, LANES), lambda b, i: (b * n_i + i, 0))
    cast_specs = [_cast_chunk_spec(w.shape[0], w.shape[1], n_i, B * n_i) for w in cast_ws]
    return pl.pallas_call(
        functools.partial(_inproj_kernel, n_cast=len(cast_ws)),
        out_shape=(
            jax.ShapeDtypeStruct((B, nq, S), BF16),
            jax.ShapeDtypeStruct((B, S, nq), BF16),
            jax.ShapeDtypeStruct((B, nv, S), BF16),
            jax.ShapeDtypeStruct((B, S, CONV_CH), F32),
            jax.ShapeDtypeStruct((B, S, D), BF16),
            *[jax.ShapeDtypeStruct(w.shape, BF16) for w in cast_ws],
        ),
        grid=(B, S // ts),
        in_specs=[
            tok(D),
            pl.BlockSpec((None, 1, N_MOD * D), lambda b, i: (b, 0, 0)),
            _const_spec(g_mix.shape),
            _const_spec(w_in.shape),
            _const_spec(g_q.shape),
            _const_spec(wq2.shape),
            _const_spec(g_kv.shape),
            _const_spec(wk.shape),
            _const_spec(wvt.shape),
            compact,
            compact,
            *cast_specs,
        ],
        out_specs=(pl.BlockSpec((None, nq, ts), lambda b, i: (b, 0, i)), tok(nq),
                   pl.BlockSpec((None, nv, ts), lambda b, i: (b, 0, i)),
                   tok(CONV_CH), tok(D), *cast_specs),
        scratch_shapes=[pltpu.VMEM((ts, HEAD_PAD), F32), pltpu.VMEM((ts, HEAD_PAD), F32),
                        pltpu.VMEM((D, C_CONV + 2 * CONV_CH), BF16)],
        compiler_params=pltpu.CompilerParams(
            dimension_semantics=("arbitrary", "arbitrary"), vmem_limit_bytes=VMEM_LIMIT),
        name="input_projection",
    )(x, mod3, g_mix, w_in, g_q, wq2, g_kv, wk, wvt, cos_c, sin_c, *cast_ws)


def _attn_kernel(qt_ref, k_ref, vt_ref, o_ref):
    n_kc = k_ref.shape[0] // KEY_CHUNK

    def score_chunk(hd, c):
        sl = slice(hd * HEAD_PAD, (hd + 1) * HEAD_PAD)
        ks = slice(c * KEY_CHUNK, (c + 1) * KEY_CHUNK)
        return _dot(k_ref[ks, sl], qt_ref[sl, :])

    def col_max(chunks):
        m = jnp.max(chunks[0], axis=0, keepdims=True)
        for st in chunks[1:]:
            m = jnp.maximum(m, jnp.max(st, axis=0, keepdims=True))
        return m

    st_next = [score_chunk(0, c) for c in range(n_kc)]
    outs = []
    for hd in range(N_HEADS):
        st_cur, st_next = st_next, []
        m = col_max(st_cur)
        l = acc = None
        for c in range(n_kc):
            if hd + 1 < N_HEADS:
                st_next.append(score_chunk(hd + 1, c))
            ks = slice(c * KEY_CHUNK, (c + 1) * KEY_CHUNK)
            p = jnp.exp2(st_cur[c] - m)
            ls = jnp.sum(p, axis=0, keepdims=True)
            pv = _dot(vt_ref[hd * V_HEAD:(hd + 1) * V_HEAD, ks], p.astype(BF16))
            l, acc = (ls, pv) if c == 0 else (l + ls, acc + pv)
        outs.append(acc / l)
        if hd % 2 == 1:
            o_pair = jnp.concatenate(outs, axis=0).T
            o_ref[:, (hd // 2) * 2 * V_HEAD:(hd // 2 + 1) * 2 * V_HEAD] = o_pair.astype(BF16)
            outs = []


def _attention(qt, k, vt):
    B, nq, S = qt.shape
    nv = N_HEADS * V_HEAD
    return pl.pallas_call(
        _attn_kernel,
        out_shape=jax.ShapeDtypeStruct((B, S, nv), BF16),
        grid=(B, S // TQ),
        in_specs=[
            pl.BlockSpec((None, nq, TQ), lambda b, i: (b, 0, i)),
            pl.BlockSpec((None, S, nq), lambda b, i: (b, 0, 0)),
            pl.BlockSpec((None, nv, S), lambda b, i: (b, 0, 0)),
        ],
        out_specs=pl.BlockSpec((None, TQ, nv), lambda b, i: (b, i, 0)),
        compiler_params=pltpu.CompilerParams(
            dimension_semantics=("parallel", "parallel"), vmem_limit_bytes=VMEM_LIMIT),
        name="mla_attention",
    )(qt, k, vt)


def _relayout_w_gate(wt_ref, out_ref):
    o_gate = Q_LORA + KV_LORA + QK_ROPE + 2 * CONV_CH
    _transpose_rows(wt_ref, o_gate, out_ref.shape[1], out_ref, 0)


def _mix_kernel(x_ref, mod_ref, z_ref, zprev_ref, znext_ref, o_ref, h_ref, w_ref,
                wdw_ref, bdw_ref, gln_ref, bln_ref, wco_ref, wao_ref, wout_ref, *rest, n_cast):
    cast_in = rest[:n_cast]
    out_ref = rest[n_cast]
    cast_out = rest[n_cast + 1:2 * n_cast + 1]
    zp_ref, zs_ref, conv_ref, wgate_ref = rest[2 * n_cast + 1:]
    for src, dst in zip(cast_in, cast_out):
        dst[...] = src[...].astype(BF16)
    D = D_MODEL
    ts = TS_MIX
    i = pl.program_id(1)
    n_i = pl.num_programs(1)
    pl.when(_first_grid_step())(lambda: _relayout_w_gate(w_ref, wgate_ref))

    zp_ref[0:HALO, :] = jnp.where(i > 0, zprev_ref[...], 0.0)
    zp_ref[HALO:HALO + ts, :] = z_ref[...]
    zp_ref[HALO + ts:2 * HALO + ts, :] = jnp.where(i < n_i - 1, znext_ref[...], 0.0)

    n_shift = zs_ref.shape[1]
    for s in range(1, SUBLANES):
        zs_ref[s - 1] = zp_ref[s:s + n_shift, :]

    gate_logits = _dot(h_ref[...], wgate_ref[...])
    y_a = _dot(o_ref[...], wao_ref[...])

    row_chunk = 64
    base = HALO - CONV_K // 2
    for cb in range(CONV_CH // LANES):
        cs = slice(cb * LANES, (cb + 1) * LANES)
        for rb in range(ts // row_chunk):
            r0 = rb * row_chunk
            acc = jnp.broadcast_to(bdw_ref[:, cs], (row_chunk, LANES))
            for kk in range(CONV_K):
                s = (base + kk) % SUBLANES
                a = r0 + base + kk - s
                src = zp_ref if s == 0 else zs_ref.at[s - 1]
                acc = acc + wdw_ref[kk:kk + 1, cs] * src[a:a + row_chunk, cs]
            conv_ref[r0:r0 + row_chunk, cs] = acc

    zc = conv_ref[...]
    mu = jnp.mean(zc, axis=-1, keepdims=True)
    zd = zc - mu
    var = jnp.mean(zd * zd, axis=-1, keepdims=True)
    zn = zd * lax.rsqrt(var + EPS_LN) * gln_ref[...] + bln_ref[...]
    zs = (zn * _sigmoid(zn)).astype(BF16)
    y_b = _dot(zs, wco_ref[...])
    gates = _sigmoid(gate_logits)
    merged = (gates[:, 0:D] * y_a + gates[:, D:2 * D] * y_b).astype(BF16)
    gate_m = mod_ref[:, 2 * D:3 * D]
    out_ref[...] = x_ref[...] + gate_m * _dot(merged, wout_ref[...])


def _cast_chunk_spec(rows, cols, n_i, n_steps):
    every = 1
    while (rows * every) % n_steps or (rows * every // n_steps) % (2 * SUBLANES):
        every *= 2
    chunk = rows * every // n_steps
    return pl.BlockSpec((chunk, cols), lambda b, i: ((b * n_i + i) // every, 0))


def _mix(x, mod3, z, o, h, w_in, w_dw, b_dw, g_ln, b_ln, w_co, w_ao, w_out, cast_ws):
    B, S, D = x.shape
    ts = TS_MIX
    hb = ts // HALO
    n_halo = S // HALO
    n_i = S // ts
    tok = lambda w: pl.BlockSpec((None, ts, w), lambda b, i: (b, i, 0))
    cast_specs = [_cast_chunk_spec(w.shape[0], w.shape[1], n_i, B * n_i) for w in cast_ws]
    return pl.pallas_call(
        functools.partial(_mix_kernel, n_cast=len(cast_ws)),
        out_shape=(jax.ShapeDtypeStruct((B, S, D), F32),
                   *[jax.ShapeDtypeStruct(w.shape, BF16) for w in cast_ws]),
        grid=(B, S // ts),
        in_specs=[
            tok(D),
            pl.BlockSpec((None, 1, N_MOD * D), lambda b, i: (b, 0, 0)),
            tok(CONV_CH),
            pl.BlockSpec((None, HALO, CONV_CH), lambda b, i: (b, jnp.maximum(i * hb - 1, 0), 0)),
            pl.BlockSpec((None, HALO, CONV_CH),
                         lambda b, i: (b, jnp.minimum((i + 1) * hb, n_halo - 1), 0)),
            tok(N_HEADS * V_HEAD),
            tok(D),
            _const_spec(w_in.shape),
            _const_spec(w_dw.shape),
            _const_spec(b_dw.shape),
            _const_spec(g_ln.shape),
            _const_spec(b_ln.shape),
            _const_spec(w_co.shape),
            _const_spec(w_ao.shape),
            _const_spec(w_out.shape),
            *cast_specs,
        ],
        out_specs=(tok(D), *cast_specs),
        scratch_shapes=[
            pltpu.VMEM((ts + 2 * HALO, CONV_CH), F32),
            pltpu.VMEM((SUBLANES - 1, ts + 2 * HALO - SUBLANES, CONV_CH), F32),
            pltpu.VMEM((ts, CONV_CH), F32),
            pltpu.VMEM((D, 2 * D), BF16),
        ],
        compiler_params=pltpu.CompilerParams(
            dimension_semantics=("arbitrary", "arbitrary"), vmem_limit_bytes=VMEM_LIMIT),
        name="conv_merge_out",
    )(x, mod3, z, z, z, o, h, w_in, w_dw, b_dw, g_ln, b_ln, w_co, w_ao, w_out, *cast_ws)


def _ffn_kernel(x_ref, mod_ref, gffn_ref, wg_ref, wu_ref, wd_ref, gfin_ref, out_ref, *, final_norm):
    D = D_MODEL
    shift = mod_ref[:, 3 * D:4 * D]
    scale = mod_ref[:, 4 * D:5 * D]
    gate = mod_ref[:, 5 * D:6 * D]
    rows = x_ref.shape[0] // ROW_SUBTILES
    for j in range(ROW_SUBTILES):
        rs = slice(j * rows, (j + 1) * rows)
        x = x_ref[rs, :]
        h = (_rms(x, gffn_ref[...]) * (1.0 + scale) + shift).astype(BF16)
        g = _dot(h, wg_ref[...])
        u = _dot(h, wu_ref[...])
        a = (g * _sigmoid(g) * u).astype(BF16)
        x2 = x + gate * _dot(a, wd_ref[...])
        out_ref[rs, :] = _rms(x2, gfin_ref[...]) if final_norm else x2


def _ffn(x, mod3, g_ffn, w_gate, w_up, w_down, g_final, final_norm):
    B, S, D = x.shape
    tm = TM_FFN
    tok = pl.BlockSpec((None, tm, D), lambda b, i: (b, i, 0))
    return pl.pallas_call(
        functools.partial(_ffn_kernel, final_norm=final_norm),
        out_shape=jax.ShapeDtypeStruct((B, S, D), F32),
        grid=(B, S // tm),
        in_specs=[
            tok,
            pl.BlockSpec((None, 1, N_MOD * D), lambda b, i: (b, 0, 0)),
            _const_spec(g_ffn.shape),
            _const_spec(w_gate.shape),
            _const_spec(w_up.shape),
            _const_spec(w_down.shape),
            _const_spec(g_final.shape),
        ],
        out_specs=tok,
        compiler_params=pltpu.CompilerParams(
            dimension_semantics=("parallel", "parallel"), vmem_limit_bytes=VMEM_LIMIT),
        name="swiglu_final_norm",
    )(x, mod3, g_ffn, w_gate, w_up, w_down, g_final)


def _prep_w_q(w_q_up):
    r = w_q_up.shape[0]
    w = w_q_up.reshape(r, N_HEADS, QK_HEAD)
    nope = w[..., :QK_NOPE]
    r1 = w[..., QK_NOPE:QK_NOPE + HALF_ROPE]
    r2 = w[..., QK_NOPE + HALF_ROPE:]
    return jnp.concatenate([nope, r1, r2, -r2, r1], axis=-1).reshape(r, N_HEADS * HEAD_PAD).T.astype(BF16)


def _prep_w_kv(w_kv_up):
    r = w_kv_up.shape[0]
    w = w_kv_up.reshape(r, N_HEADS, QK_NOPE + V_HEAD)
    k_nope = w[..., :QK_NOPE]
    v = w[..., QK_NOPE:]
    k_pad = jnp.concatenate(
        [k_nope, jnp.zeros((r, N_HEADS, HEAD_PAD - QK_NOPE), w.dtype)], axis=-1)
    wk = k_pad.reshape(r, N_HEADS * HEAD_PAD).astype(BF16)
    wvt = v.reshape(r, N_HEADS * V_HEAD).T.astype(BF16)
    return wk, wvt


def kernel(x, c, positions, w_ada, b_ada, g_norm_mix, w_in, g_q_a, w_q_up, g_kv_a, w_kv_up,
           w_attn_o, w_dw, b_dw, g_conv_ln, b_conv_ln, w_conv_out, w_out, g_norm_ffn,
           w_ffn_gate, w_ffn_up, w_ffn_down, g_final):
    B, S, D = x.shape
    depth = w_ada.shape[0]
    cos_c, sin_c = _rope_tables(positions)
    for l in range(depth):
        mod3 = _modulation(c, w_ada[l], b_ada[l][None, :]).reshape(B, 1, N_MOD * D)
        wk, wvt = _prep_w_kv(w_kv_up[l])
        w_in_t = w_in[l].T
        q, k, vt, z, h, w_co_b, w_ao_b, w_out_b = _inproj(
            x, mod3, g_norm_mix[l][None, :], w_in_t, g_q_a[l][None, :],
            _prep_w_q(w_q_up[l]), g_kv_a[l][None, :], wk, wvt, cos_c, sin_c,
            (w_conv_out[l], w_attn_o[l], w_out[l]))
        o = _attention(q, k, vt)
        x, w_gate_b, w_up_b, w_down_b = _mix(
            x, mod3, z, o, h, w_in_t, w_dw[l], b_dw[l][None, :], g_conv_ln[l][None, :],
            b_conv_ln[l][None, :], w_co_b, w_ao_b, w_out_b,
            (w_ffn_gate[l], w_ffn_up[l], w_ffn_down[l]))
        x = _ffn(x, mod3, g_norm_ffn[l][None, :], w_gate_b, w_up_b, w_down_b, g_final[None, :],
                 final_norm=(l == depth - 1))
    return x
```

```python
import functools
import math

import jax
import jax.numpy as jnp
from jax import lax
from jax.experimental import pallas as pl
from jax.experimental.pallas import tpu as pltpu

F32 = jnp.float32
BF16 = jnp.bfloat16

D_MODEL = 1024
N_HEADS = 8
Q_LORA = 256
KV_LORA = 128
QK_NOPE = 64
QK_ROPE = 32
HALF_ROPE = QK_ROPE // 2
V_HEAD = 64
QK_HEAD = QK_NOPE + QK_ROPE
ATTN_SCALE = 1.0 / math.sqrt(QK_HEAD)
Q_SCALE = ATTN_SCALE * math.log2(math.e)
ROPE_THETA = 10000.0
CONV_CH = 512
CONV_K = 31
N_MOD = 6
EPS_RMS = 1e-6
EPS_LN = 1e-5

LANES = 128
SUBLANES = 8
HEAD_PAD = LANES
VMEM_LIMIT = 56 * 1024 * 1024

C_QA = 0
C_KVA = C_QA + Q_LORA
C_KR = C_KVA + KV_LORA
C_CONV = C_KR + HEAD_PAD

TS_IN = 1024
TQ = 512
KEY_CHUNK = 1024
TS_MIX = 512
HALO = 16
TM_FFN = 512
ROW_SUBTILES = 2


def _sigmoid(x):
    return 1.0 / (1.0 + jnp.exp(-x))


def _rms(x, g):
    return x * lax.rsqrt(jnp.mean(x * x, axis=-1, keepdims=True) + EPS_RMS) * g


def _dot(a, b):
    return jnp.dot(a, b, preferred_element_type=F32)


def _const_spec(shape):
    nd = len(shape)
    return pl.BlockSpec(shape, lambda *_: (0,) * nd, pipeline_mode=pl.Buffered(1))


def _rope_kernel(pos_ref, freq_ref, cos_ref, sin_ref):
    ang = pos_ref[...] * freq_ref[...]
    cos_ref[...] = jnp.cos(ang)
    sin_ref[...] = jnp.sin(ang)


def _rope_tables(positions):
    B, S = positions.shape
    inv_freq = ROPE_THETA ** (-jnp.arange(0, QK_ROPE, 2, dtype=F32) / QK_ROPE)
    pos = positions.reshape(1, B * S).astype(F32)
    cos_t, sin_t = pl.pallas_call(
        _rope_kernel,
        out_shape=(jax.ShapeDtypeStruct((HALF_ROPE, B * S), F32),) * 2,
        name="rope_tables",
    )(pos, inv_freq[:, None])
    return cos_t, sin_t


def _rope_tile_t(t, ctab_t, stab_t):
    swapped = jnp.concatenate([t[0:QK_NOPE], t[QK_HEAD:HEAD_PAD], t[QK_HEAD:HEAD_PAD]], axis=0)
    return t * ctab_t + swapped * stab_t


def _mod_kernel(c_ref, w_ref, b_ref, o_ref):
    c = c_ref[...]
    c_act = (c * _sigmoid(c)).astype(BF16)
    o_ref[...] = _dot(c_act, w_ref[...].astype(BF16)) + b_ref[...]


def _modulation(c, w_ada, b_ada):
    B, D = c.shape
    n = w_ada.shape[1]
    bn = 1536
    return pl.pallas_call(
        _mod_kernel,
        out_shape=jax.ShapeDtypeStruct((B, n), F32),
        grid=(n // bn,),
        in_specs=[
            pl.BlockSpec((B, D), lambda j: (0, 0)),
            pl.BlockSpec((D, bn), lambda j: (0, j)),
            pl.BlockSpec((1, bn), lambda j: (0, j)),
        ],
        out_specs=pl.BlockSpec((B, bn), lambda j: (0, j)),
        compiler_params=pltpu.CompilerParams(
            dimension_semantics=("parallel",), vmem_limit_bytes=VMEM_LIMIT),
        name="adaln_modulation",
    )(c, w_ada, b_ada)


def _mixer_input(x, mod_ref, gmix_ref):
    D = D_MODEL
    shift = mod_ref[:, 0:D]
    scale = mod_ref[:, D:2 * D]
    return (_rms(x, gmix_ref[...]) * (1.0 + scale) + shift).astype(BF16)


def _first_grid_step():
    return jnp.logical_and(pl.program_id(0) == 0, pl.program_id(1) == 0)


def _transpose_rows(wt_ref, row0, n_rows, out_ref, col0):
    for j in range(n_rows // LANES):
        rows = wt_ref[row0 + j * LANES:row0 + (j + 1) * LANES, :]
        out_ref[:, col0 + j * LANES:col0 + (j + 1) * LANES] = rows.T.astype(BF16)


def _relayout_w_proj(wt_ref, out_ref):
    o_kr = Q_LORA + KV_LORA
    o_conv = o_kr + QK_ROPE
    _transpose_rows(wt_ref, 0, o_kr, out_ref, C_QA)
    k1 = wt_ref[o_kr:o_kr + HALF_ROPE, :]
    k2 = wt_ref[o_kr + HALF_ROPE:o_conv, :]
    key_tile_t = jnp.concatenate([jnp.zeros((QK_NOPE, k1.shape[1]), F32), k1, k2, -k2, k1], axis=0)
    out_ref[:, C_KR:C_KR + HEAD_PAD] = key_tile_t.T.astype(BF16)
    _transpose_rows(wt_ref, o_conv, 2 * CONV_CH, out_ref, C_CONV)


def _inproj_kernel(x_ref, mod_ref, gmix_ref, w_ref, gq_ref, wq_ref, gkv_ref, wk_ref, wvt_ref,
                   cosc_ref, sinc_ref, *rest, n_cast):
    cast_in = rest[:n_cast]
    q_ref, k_ref, vt_ref, z_ref, h_ref = rest[n_cast:n_cast + 5]
    cast_out = rest[n_cast + 5:2 * n_cast + 5]
    (win_ref,) = rest[2 * n_cast + 5:]
    for src, dst in zip(cast_in, cast_out):
        dst[...] = src[...].astype(BF16)
    pl.when(_first_grid_step())(lambda: _relayout_w_proj(w_ref, win_ref))
    h = _mixer_input(x_ref[...], mod_ref, gmix_ref)
    h_ref[...] = h
    cos_t = cosc_ref[...]
    sin_t = sinc_ref[...]
    n_pos = cos_t.shape[1]
    ctab_t = jnp.concatenate([jnp.ones((QK_NOPE, n_pos), F32), cos_t, cos_t,
                              jnp.zeros((HEAD_PAD - QK_HEAD, n_pos), F32)], axis=0)
    stab_t = jnp.concatenate([jnp.zeros((QK_NOPE, n_pos), F32), sin_t, sin_t,
                              jnp.zeros((HEAD_PAD - QK_HEAD, n_pos), F32)], axis=0)

    proj = _dot(h, win_ref[...])

    z_ref[...] = (proj[:, C_CONV:C_CONV + CONV_CH]
                  * _sigmoid(proj[:, C_CONV + CONV_CH:C_CONV + 2 * CONV_CH]))

    qn = _rms(proj[:, C_QA:C_QA + Q_LORA], gq_ref[...]).astype(BF16)
    q_all_t = lax.dot_general(wq_ref[...], qn, (((1,), (1,)), ((), ())), preferred_element_type=F32)
    for hd in range(N_HEADS):
        sl = slice(hd * HEAD_PAD, (hd + 1) * HEAD_PAD)
        q_ref[sl, :] = (_rope_tile_t(q_all_t[sl, :], ctab_t, stab_t) * Q_SCALE).astype(BF16)

    kvn = _rms(proj[:, C_KVA:C_KVA + KV_LORA], gkv_ref[...]).astype(BF16)
    k_pad = _dot(kvn, wk_ref[...])
    vt_ref[...] = lax.dot_general(wvt_ref[...], kvn, (((1,), (1,)), ((), ())),
                                  preferred_element_type=F32).astype(BF16)
    kr_rot = _rope_tile_t(proj[:, C_KR:C_KR + HEAD_PAD].T, ctab_t, stab_t).T
    for hd in range(N_HEADS):
        sl = slice(hd * HEAD_PAD, (hd + 1) * HEAD_PAD)
        k_ref[:, sl] = (k_pad[:, sl] + kr_rot).astype(BF16)


def _inproj(x, mod3, g_mix, w_in, g_q, wq2, g_kv, wk, wvt, cos_c, sin_c, cast_ws):
    B, S, D = x.shape
    ts = TS_IN
    nq = N_HEADS * HEAD_PAD
    nv = N_HEADS * V_HEAD
    tok = lambda w: pl.BlockSpec((None, ts, w), lambda b, i: (b, i, 0))
    n_i = S // ts
    compact = pl.BlockSpec((HALF_ROPE, ts), lambda b, i: (0, b * n_i + i))
    cast_specs = [_cast_chunk_spec(w.shape[0], w.shape[1], n_i, B * n_i) for w in cast_ws]
    return pl.pallas_call(
        functools.partial(_inproj_kernel, n_cast=len(cast_ws)),
        out_shape=(
            jax.ShapeDtypeStruct((B, nq, S), BF16),
            jax.ShapeDtypeStruct((B, S, nq), BF16),
            jax.ShapeDtypeStruct((B, nv, S), BF16),
            jax.ShapeDtypeStruct((B, S, CONV_CH), F32),
            jax.ShapeDtypeStruct((B, S, D), BF16),
            *[jax.ShapeDtypeStruct(w.shape, BF16) for w in cast_ws],
        ),
        grid=(B, S // ts),
        in_specs=[
            tok(D),
            pl.BlockSpec((None, 1, N_MOD * D), lambda b, i: (b, 0, 0)),
            _const_spec(g_mix.shape),
            _const_spec(w_in.shape),
            _const_spec(g_q.shape),
            _const_spec(wq2.shape),
            _const_spec(g_kv.shape),
            _const_spec(wk.shape),
            _const_spec(wvt.shape),
            compact,
            compact,
            *cast_specs,
        ],
        out_specs=(pl.BlockSpec((None, nq, ts), lambda b, i: (b, 0, i)), tok(nq),
                   pl.BlockSpec((None, nv, ts), lambda b, i: (b, 0, i)),
                   tok(CONV_CH), tok(D), *cast_specs),
        scratch_shapes=[pltpu.VMEM((D, C_CONV + 2 * CONV_CH), BF16)],
        compiler_params=pltpu.CompilerParams(
            dimension_semantics=("arbitrary", "arbitrary"), vmem_limit_bytes=VMEM_LIMIT),
        name="input_projection",
    )(x, mod3, g_mix, w_in, g_q, wq2, g_kv, wk, wvt, cos_c, sin_c, *cast_ws)


def _attn_kernel(qt_ref, k_ref, vt_ref, o_ref):
    n_kc = k_ref.shape[0] // KEY_CHUNK

    def score_chunk(hd, c):
        sl = slice(hd * HEAD_PAD, (hd + 1) * HEAD_PAD)
        ks = slice(c * KEY_CHUNK, (c + 1) * KEY_CHUNK)
        return _dot(k_ref[ks, sl], qt_ref[sl, :])

    def col_max(chunks):
        m = jnp.max(chunks[0], axis=0, keepdims=True)
        for st in chunks[1:]:
            m = jnp.maximum(m, jnp.max(st, axis=0, keepdims=True))
        return m

    st_next = [score_chunk(0, c) for c in range(n_kc)]
    outs = []
    for hd in range(N_HEADS):
        st_cur, st_next = st_next, []
        m = col_max(st_cur)
        l = acc = None
        for c in range(n_kc):
            if hd + 1 < N_HEADS:
                st_next.append(score_chunk(hd + 1, c))
            ks = slice(c * KEY_CHUNK, (c + 1) * KEY_CHUNK)
            p = jnp.exp2(st_cur[c] - m)
            ls = jnp.sum(p, axis=0, keepdims=True)
            pv = _dot(vt_ref[hd * V_HEAD:(hd + 1) * V_HEAD, ks], p.astype(BF16))
            l, acc = (ls, pv) if c == 0 else (l + ls, acc + pv)
        outs.append(acc / l)
        if hd % 2 == 1:
            o_pair = jnp.concatenate(outs, axis=0).T
            o_ref[:, (hd // 2) * 2 * V_HEAD:(hd // 2 + 1) * 2 * V_HEAD] = o_pair.astype(BF16)
            outs = []


def _attention(qt, k, vt):
    B, nq, S = qt.shape
    nv = N_HEADS * V_HEAD
    return pl.pallas_call(
        _attn_kernel,
        out_shape=jax.ShapeDtypeStruct((B, S, nv), BF16),
        grid=(B, S // TQ),
        in_specs=[
            pl.BlockSpec((None, nq, TQ), lambda b, i: (b, 0, i)),
            pl.BlockSpec((None, S, nq), lambda b, i: (b, 0, 0)),
            pl.BlockSpec((None, nv, S), lambda b, i: (b, 0, 0)),
        ],
        out_specs=pl.BlockSpec((None, TQ, nv), lambda b, i: (b, i, 0)),
        compiler_params=pltpu.CompilerParams(
            dimension_semantics=("parallel", "parallel"), vmem_limit_bytes=VMEM_LIMIT),
        name="mla_attention",
    )(qt, k, vt)


def _relayout_w_gate(wt_ref, out_ref):
    o_gate = Q_LORA + KV_LORA + QK_ROPE + 2 * CONV_CH
    _transpose_rows(wt_ref, o_gate, out_ref.shape[1], out_ref, 0)


def _mix_kernel(x_ref, mod_ref, z_ref, zprev_ref, znext_ref, o_ref, h_ref, w_ref,
                wdw_ref, bdw_ref, gln_ref, bln_ref, wco_ref, wao_ref, wout_ref, *rest, n_cast):
    cast_in = rest[:n_cast]
    out_ref = rest[n_cast]
    cast_out = rest[n_cast + 1:2 * n_cast + 1]
    zp_ref, zs_ref, conv_ref, wgate_ref = rest[2 * n_cast + 1:]
    for src, dst in zip(cast_in, cast_out):
        dst[...] = src[...].astype(BF16)
    D = D_MODEL
    ts = TS_MIX
    i = pl.program_id(1)
    n_i = pl.num_programs(1)
    pl.when(_first_grid_step())(lambda: _relayout_w_gate(w_ref, wgate_ref))

    zp_ref[0:HALO, :] = jnp.where(i > 0, zprev_ref[...], 0.0)
    zp_ref[HALO:HALO + ts, :] = z_ref[...]
    zp_ref[HALO + ts:2 * HALO + ts, :] = jnp.where(i < n_i - 1, znext_ref[...], 0.0)

    n_shift = zs_ref.shape[1]
    for s in range(1, SUBLANES):
        zs_ref[s - 1] = zp_ref[s:s + n_shift, :]

    gate_logits = _dot(h_ref[...], wgate_ref[...])
    y_a = _dot(o_ref[...], wao_ref[...])

    row_chunk = 64
    base = HALO - CONV_K // 2
    for cb in range(CONV_CH // LANES):
        cs = slice(cb * LANES, (cb + 1) * LANES)
        for rb in range(ts // row_chunk):
            r0 = rb * row_chunk
            acc = jnp.broadcast_to(bdw_ref[:, cs], (row_chunk, LANES))
            for kk in range(CONV_K):
                s = (base + kk) % SUBLANES
                a = r0 + base + kk - s
                src = zp_ref if s == 0 else zs_ref.at[s - 1]
                acc = acc + wdw_ref[kk:kk + 1, cs] * src[a:a + row_chunk, cs]
            conv_ref[r0:r0 + row_chunk, cs] = acc

    zc = conv_ref[...]
    mu = jnp.mean(zc, axis=-1, keepdims=True)
    zd = zc - mu
    var = jnp.mean(zd * zd, axis=-1, keepdims=True)
    zn = zd * lax.rsqrt(var + EPS_LN) * gln_ref[...] + bln_ref[...]
    zs = (zn * _sigmoid(zn)).astype(BF16)
    y_b = _dot(zs, wco_ref[...])
    gates = _sigmoid(gate_logits)
    merged = (gates[:, 0:D] * y_a + gates[:, D:2 * D] * y_b).astype(BF16)
    gate_m = mod_ref[:, 2 * D:3 * D]
    out_ref[...] = x_ref[...] + gate_m * _dot(merged, wout_ref[...])


def _cast_chunk_spec(rows, cols, n_i, n_steps):
    every = 1
    while (rows * every) % n_steps or (rows * every // n_steps) % (2 * SUBLANES):
        every *= 2
    chunk = rows * every // n_steps
    return pl.BlockSpec((chunk, cols), lambda b, i: ((b * n_i + i) // every, 0))


def _mix(x, mod3, z, o, h, w_in, w_dw, b_dw, g_ln, b_ln, w_co, w_ao, w_out, cast_ws):
    B, S, D = x.shape
    ts = TS_MIX
    hb = ts // HALO
    n_halo = S // HALO
    n_i = S // ts
    tok = lambda w: pl.BlockSpec((None, ts, w), lambda b, i: (b, i, 0))
    cast_specs = [_cast_chunk_spec(w.shape[0], w.shape[1], n_i, B * n_i) for w in cast_ws]
    return pl.pallas_call(
        functools.partial(_mix_kernel, n_cast=len(cast_ws)),
        out_shape=(jax.ShapeDtypeStruct((B, S, D), F32),
                   *[jax.ShapeDtypeStruct(w.shape, BF16) for w in cast_ws]),
        grid=(B, S // ts),
        in_specs=[
            tok(D),
            pl.BlockSpec((None, 1, N_MOD * D), lambda b, i: (b, 0, 0)),
            tok(CONV_CH),
            pl.BlockSpec((None, HALO, CONV_CH), lambda b, i: (b, jnp.maximum(i * hb - 1, 0), 0)),
            pl.BlockSpec((None, HALO, CONV_CH),
                         lambda b, i: (b, jnp.minimum((i + 1) * hb, n_halo - 1), 0)),
            tok(N_HEADS * V_HEAD),
            tok(D),
            _const_spec(w_in.shape),
            _const_spec(w_dw.shape),
            _const_spec(b_dw.shape),
            _const_spec(g_ln.shape),
            _const_spec(b_ln.shape),
            _const_spec(w_co.shape),
            _const_spec(w_ao.shape),
            _const_spec(w_out.shape),
            *cast_specs,
        ],
        out_specs=(tok(D), *cast_specs),
        scratch_shapes=[
            pltpu.VMEM((ts + 2 * HALO, CONV_CH), F32),
            pltpu.VMEM((SUBLANES - 1, ts + 2 * HALO - SUBLANES, CONV_CH), F32),
            pltpu.VMEM((ts, CONV_CH), F32),
            pltpu.VMEM((D, 2 * D), BF16),
        ],
        compiler_params=pltpu.CompilerParams(
            dimension_semantics=("arbitrary", "arbitrary"), vmem_limit_bytes=VMEM_LIMIT),
        name="conv_merge_out",
    )(x, mod3, z, z, z, o, h, w_in, w_dw, b_dw, g_ln, b_ln, w_co, w_ao, w_out, *cast_ws)


def _ffn_kernel(x_ref, mod_ref, gffn_ref, wg_ref, wu_ref, wd_ref, gfin_ref, out_ref, *, final_norm):
    D = D_MODEL
    shift = mod_ref[:, 3 * D:4 * D]
    scale = mod_ref[:, 4 * D:5 * D]
    gate = mod_ref[:, 5 * D:6 * D]
    rows = x_ref.shape[0] // ROW_SUBTILES
    for j in range(ROW_SUBTILES):
        rs = slice(j * rows, (j + 1) * rows)
        x = x_ref[rs, :]
        h = (_rms(x, gffn_ref[...]) * (1.0 + scale) + shift).astype(BF16)
        g = _dot(h, wg_ref[...])
        u = _dot(h, wu_ref[...])
        a = (g * _sigmoid(g) * u).astype(BF16)
        x2 = x + gate * _dot(a, wd_ref[...])
        out_ref[rs, :] = _rms(x2, gfin_ref[...]) if final_norm else x2


def _ffn(x, mod3, g_ffn, w_gate, w_up, w_down, g_final, final_norm):
    B, S, D = x.shape
    tm = TM_FFN
    tok = pl.BlockSpec((None, tm, D), lambda b, i: (b, i, 0))
    return pl.pallas_call(
        functools.partial(_ffn_kernel, final_norm=final_norm),
        out_shape=jax.ShapeDtypeStruct((B, S, D), F32),
        grid=(B, S // tm),
        in_specs=[
            tok,
            pl.BlockSpec((None, 1, N_MOD * D), lambda b, i: (b, 0, 0)),
            _const_spec(g_ffn.shape),
            _const_spec(w_gate.shape),
            _const_spec(w_up.shape),
            _const_spec(w_down.shape),
            _const_spec(g_final.shape),
        ],
        out_specs=tok,
        compiler_params=pltpu.CompilerParams(
            dimension_semantics=("parallel", "parallel"), vmem_limit_bytes=VMEM_LIMIT),
        name="swiglu_final_norm",
    )(x, mod3, g_ffn, w_gate, w_up, w_down, g_final)


def _prep_w_q(w_q_up):
    r = w_q_up.shape[0]
    w = w_q_up.reshape(r, N_HEADS, QK_HEAD)
    nope = w[..., :QK_NOPE]
    r1 = w[..., QK_NOPE:QK_NOPE + HALF_ROPE]
    r2 = w[..., QK_NOPE + HALF_ROPE:]
    return jnp.concatenate([nope, r1, r2, -r2, r1], axis=-1).reshape(r, N_HEADS * HEAD_PAD).T.astype(BF16)


def _prep_w_kv(w_kv_up):
    r = w_kv_up.shape[0]
    w = w_kv_up.reshape(r, N_HEADS, QK_NOPE + V_HEAD)
    k_nope = w[..., :QK_NOPE]
    v = w[..., QK_NOPE:]
    k_pad = jnp.concatenate(
        [k_nope, jnp.zeros((r, N_HEADS, HEAD_PAD - QK_NOPE), w.dtype)], axis=-1)
    wk = k_pad.reshape(r, N_HEADS * HEAD_PAD).astype(BF16)
    wvt = v.reshape(r, N_HEADS * V_HEAD).T.astype(BF16)
    return wk, wvt


def kernel(x, c, positions, w_ada, b_ada, g_norm_mix, w_in, g_q_a, w_q_up, g_kv_a, w_kv_up,
           w_attn_o, w_dw, b_dw, g_conv_ln, b_conv_ln, w_conv_out, w_out, g_norm_ffn,
           w_ffn_gate, w_ffn_up, w_ffn_down, g_final):
    B, S, D = x.shape
    depth = w_ada.shape[0]
    cos_c, sin_c = _rope_tables(positions)
    for l in range(depth):
        mod3 = _modulation(c, w_ada[l], b_ada[l][None, :]).reshape(B, 1, N_MOD * D)
        wk, wvt = _prep_w_kv(w_kv_up[l])
        w_in_t = w_in[l].T
        q, k, vt, z, h, w_co_b, w_ao_b, w_out_b = _inproj(
            x, mod3, g_norm_mix[l][None, :], w_in_t, g_q_a[l][None, :],
            _prep_w_q(w_q_up[l]), g_kv_a[l][None, :], wk, wvt, cos_c, sin_c,
            (w_conv_out[l], w_attn_o[l], w_out[l]))
        o = _attention(q, k, vt)
        x, w_gate_b, w_up_b, w_down_b = _mix(
            x, mod3, z, o, h, w_in_t, w_dw[l], b_dw[l][None, :], g_conv_ln[l][None, :],
            b_conv_ln[l][None, :], w_co_b, w_ao_b, w_out_b,
            (w_ffn_gate[l], w_ffn_up[l], w_ffn_down[l]))
        x = _ffn(x, mod3, g_norm_ffn[l][None, :], w_gate_b, w_up_b, w_down_b, g_final[None, :],
                 final_norm=(l == depth - 1))
    return x
```

```python
import functools
import math

import jax
import jax.numpy as jnp
from jax import lax
from jax.experimental import pallas as pl
from jax.experimental.pallas import tpu as pltpu

F32 = jnp.float32
BF16 = jnp.bfloat16

D_MODEL = 1024
N_HEADS = 8
Q_LORA = 256
KV_LORA = 128
QK_NOPE = 64
QK_ROPE = 32
HALF_ROPE = QK_ROPE // 2
V_HEAD = 64
QK_HEAD = QK_NOPE + QK_ROPE
ATTN_SCALE = 1.0 / math.sqrt(QK_HEAD)
Q_SCALE = ATTN_SCALE * math.log2(math.e)
ROPE_THETA = 10000.0
CONV_CH = 512
CONV_K = 31
N_MOD = 6
EPS_RMS = 1e-6
EPS_LN = 1e-5

LANES = 128
SUBLANES = 8
HEAD_PAD = LANES
VMEM_LIMIT = 56 * 1024 * 1024

C_QA = 0
C_KVA = C_QA + Q_LORA
C_KR = C_KVA + KV_LORA
C_CONV = C_KR + HEAD_PAD

TS_IN = 1024
TQ = 512
KEY_CHUNK = 1024
TS_MIX = 512
HALO = 16
TM_FFN = 512
ROW_SUBTILES = 2


def _sigmoid(x):
    return 1.0 / (1.0 + jnp.exp(-x))


def _rms(x, g):
    return x * lax.rsqrt(jnp.mean(x * x, axis=-1, keepdims=True) + EPS_RMS) * g


def _dot(a, b):
    return jnp.dot(a, b, preferred_element_type=F32)


def _const_spec(shape):
    nd = len(shape)
    return pl.BlockSpec(shape, lambda *_: (0,) * nd, pipeline_mode=pl.Buffered(1))


def _rope_kernel(pos_ref, freq_ref, cos_ref, sin_ref):
    ang = pos_ref[...] * freq_ref[...]
    cos_ref[...] = jnp.cos(ang)
    sin_ref[...] = jnp.sin(ang)


def _rope_tables(positions):
    B, S = positions.shape
    inv_freq = ROPE_THETA ** (-jnp.arange(0, QK_ROPE, 2, dtype=F32) / QK_ROPE)
    pos = positions.reshape(1, B * S).astype(F32)
    cos_t, sin_t = pl.pallas_call(
        _rope_kernel,
        out_shape=(jax.ShapeDtypeStruct((HALF_ROPE, B * S), F32),) * 2,
        name="rope_tables",
    )(pos, inv_freq[:, None])
    return cos_t, sin_t


def _rope_tile_t(t, ctab_t, stab_t):
    swapped = jnp.concatenate([t[0:QK_NOPE], t[QK_HEAD:HEAD_PAD], t[QK_HEAD:HEAD_PAD]], axis=0)
    return t * ctab_t + swapped * stab_t


def _mod_kernel(c_ref, w_ref, b_ref, o_ref):
    c = c_ref[...]
    c_act = (c * _sigmoid(c)).astype(BF16)
    o_ref[...] = _dot(c_act, w_ref[...].astype(BF16)) + b_ref[...]


def _modulation(c, w_ada, b_ada):
    B, D = c.shape
    n = w_ada.shape[1]
    bn = 1536
    return pl.pallas_call(
        _mod_kernel,
        out_shape=jax.ShapeDtypeStruct((B, n), F32),
        grid=(n // bn,),
        in_specs=[
            pl.BlockSpec((B, D), lambda j: (0, 0)),
            pl.BlockSpec((D, bn), lambda j: (0, j)),
            pl.BlockSpec((1, bn), lambda j: (0, j)),
        ],
        out_specs=pl.BlockSpec((B, bn), lambda j: (0, j)),
        compiler_params=pltpu.CompilerParams(
            dimension_semantics=("parallel",), vmem_limit_bytes=VMEM_LIMIT),
        name="adaln_modulation",
    )(c, w_ada, b_ada)


def _mixer_input(x, mod_ref, gmix_ref):
    D = D_MODEL
    shift = mod_ref[:, 0:D]
    scale = mod_ref[:, D:2 * D]
    return (_rms(x, gmix_ref[...]) * (1.0 + scale) + shift).astype(BF16)


def _first_grid_step():
    return jnp.logical_and(pl.program_id(0) == 0, pl.program_id(1) == 0)


def _transpose_rows(wt_ref, row0, n_rows, out_ref, col0):
    for j in range(n_rows // LANES):
        rows = wt_ref[row0 + j * LANES:row0 + (j + 1) * LANES, :]
        out_ref[:, col0 + j * LANES:col0 + (j + 1) * LANES] = rows.T.astype(BF16)


def _relayout_w_proj(wt_ref, out_ref):
    o_kr = Q_LORA + KV_LORA
    o_conv = o_kr + QK_ROPE
    _transpose_rows(wt_ref, 0, o_kr, out_ref, C_QA)
    k1 = wt_ref[o_kr:o_kr + HALF_ROPE, :]
    k2 = wt_ref[o_kr + HALF_ROPE:o_conv, :]
    key_tile_t = jnp.concatenate([jnp.zeros((QK_NOPE, k1.shape[1]), F32), k1, k2, -k2, k1], axis=0)
    out_ref[:, C_KR:C_KR + HEAD_PAD] = key_tile_t.T.astype(BF16)
    _transpose_rows(wt_ref, o_conv, 2 * CONV_CH, out_ref, C_CONV)


def _inproj_kernel(x_ref, mod_ref, gmix_ref, w_ref, gq_ref, wq_ref, gkv_ref, wk_ref, wvt_ref,
                   cost_ref, sint_ref, *rest, n_cast):
    cast_in = rest[:n_cast]
    q_ref, k_ref, vt_ref, z_ref, h_ref = rest[n_cast:n_cast + 5]
    cast_out = rest[n_cast + 5:2 * n_cast + 5]
    (win_ref,) = rest[2 * n_cast + 5:]
    for src, dst in zip(cast_in, cast_out):
        dst[...] = src[...].astype(BF16)
    pl.when(_first_grid_step())(lambda: _relayout_w_proj(w_ref, win_ref))
    h = _mixer_input(x_ref[...], mod_ref, gmix_ref)
    h_ref[...] = h
    cos_t = cost_ref[...]
    sin_t = sint_ref[...]
    n_pos = cos_t.shape[1]
    ctab_t = jnp.concatenate([jnp.ones((QK_NOPE, n_pos), F32), cos_t, cos_t,
                              jnp.zeros((HEAD_PAD - QK_HEAD, n_pos), F32)], axis=0)
    stab_t = jnp.concatenate([jnp.zeros((QK_NOPE, n_pos), F32), sin_t, sin_t,
                              jnp.zeros((HEAD_PAD - QK_HEAD, n_pos), F32)], axis=0)

    proj = _dot(h, win_ref[...])

    z_ref[...] = (proj[:, C_CONV:C_CONV + CONV_CH]
                  * _sigmoid(proj[:, C_CONV + CONV_CH:C_CONV + 2 * CONV_CH]))

    qn = _rms(proj[:, C_QA:C_QA + Q_LORA], gq_ref[...]).astype(BF16)
    q_all_t = lax.dot_general(wq_ref[...], qn, (((1,), (1,)), ((), ())), preferred_element_type=F32)
    for hd in range(N_HEADS):
        sl = slice(hd * HEAD_PAD, (hd + 1) * HEAD_PAD)
        q_ref[sl, :] = (_rope_tile_t(q_all_t[sl, :], ctab_t, stab_t) * Q_SCALE).astype(BF16)

    kvn = _rms(proj[:, C_KVA:C_KVA + KV_LORA], gkv_ref[...]).astype(BF16)
    k_pad = _dot(kvn, wk_ref[...])
    vt_ref[...] = lax.dot_general(wvt_ref[...], kvn, (((1,), (1,)), ((), ())),
                                  preferred_element_type=F32).astype(BF16)
    kr_rot = _rope_tile_t(proj[:, C_KR:C_KR + HEAD_PAD].T, ctab_t, stab_t).T
    for hd in range(N_HEADS):
        sl = slice(hd * HEAD_PAD, (hd + 1) * HEAD_PAD)
        k_ref[:, sl] = (k_pad[:, sl] + kr_rot).astype(BF16)


def _inproj(x, mod3, g_mix, w_in, g_q, wq_t, g_kv, wk, wvt, cos_t, sin_t, cast_ws):
    B, S, D = x.shape
    ts = TS_IN
    nq = N_HEADS * HEAD_PAD
    nv = N_HEADS * V_HEAD
    tok = lambda w: pl.BlockSpec((None, ts, w), lambda b, i: (b, i, 0))
    n_i = S // ts
    table = pl.BlockSpec((HALF_ROPE, ts), lambda b, i: (0, b * n_i + i))
    cast_specs = [_cast_chunk_spec(w.shape[0], w.shape[1], n_i, B * n_i) for w in cast_ws]
    return pl.pallas_call(
        functools.partial(_inproj_kernel, n_cast=len(cast_ws)),
        out_shape=(
            jax.ShapeDtypeStruct((B, nq, S), BF16),
            jax.ShapeDtypeStruct((B, S, nq), BF16),
            jax.ShapeDtypeStruct((B, nv, S), BF16),
            jax.ShapeDtypeStruct((B, S, CONV_CH), F32),
            jax.ShapeDtypeStruct((B, S, D), BF16),
            *[jax.ShapeDtypeStruct(w.shape, BF16) for w in cast_ws],
        ),
        grid=(B, S // ts),
        in_specs=[
            tok(D),
            pl.BlockSpec((None, 1, N_MOD * D), lambda b, i: (b, 0, 0)),
            _const_spec(g_mix.shape),
            _const_spec(w_in.shape),
            _const_spec(g_q.shape),
            _const_spec(wq_t.shape),
            _const_spec(g_kv.shape),
            _const_spec(wk.shape),
            _const_spec(wvt.shape),
            table,
            table,
            *cast_specs,
        ],
        out_specs=(pl.BlockSpec((None, nq, ts), lambda b, i: (b, 0, i)), tok(nq),
                   pl.BlockSpec((None, nv, ts), lambda b, i: (b, 0, i)),
                   tok(CONV_CH), tok(D), *cast_specs),
        scratch_shapes=[pltpu.VMEM((D, C_CONV + 2 * CONV_CH), BF16)],
        compiler_params=pltpu.CompilerParams(
            dimension_semantics=("arbitrary", "arbitrary"), vmem_limit_bytes=VMEM_LIMIT),
        name="input_projection",
    )(x, mod3, g_mix, w_in, g_q, wq_t, g_kv, wk, wvt, cos_t, sin_t, *cast_ws)


def _attn_kernel(qt_ref, k_ref, vt_ref, o_ref):
    n_kc = k_ref.shape[0] // KEY_CHUNK

    def score_chunk(hd, c):
        sl = slice(hd * HEAD_PAD, (hd + 1) * HEAD_PAD)
        ks = slice(c * KEY_CHUNK, (c + 1) * KEY_CHUNK)
        return _dot(k_ref[ks, sl], qt_ref[sl, :])

    def col_max(chunks):
        m = jnp.max(chunks[0], axis=0, keepdims=True)
        for st in chunks[1:]:
            m = jnp.maximum(m, jnp.max(st, axis=0, keepdims=True))
        return m

    st_next = [score_chunk(0, c) for c in range(n_kc)]
    outs = []
    for hd in range(N_HEADS):
        st_cur, st_next = st_next, []
        m = col_max(st_cur)
        l = acc = None
        for c in range(n_kc):
            if hd + 1 < N_HEADS:
                st_next.append(score_chunk(hd + 1, c))
            ks = slice(c * KEY_CHUNK, (c + 1) * KEY_CHUNK)
            p = jnp.exp2(st_cur[c] - m)
            ls = jnp.sum(p, axis=0, keepdims=True)
            pv = _dot(vt_ref[hd * V_HEAD:(hd + 1) * V_HEAD, ks], p.astype(BF16))
            l, acc = (ls, pv) if c == 0 else (l + ls, acc + pv)
        outs.append(acc / l)
        if hd % 2 == 1:
            o_pair = jnp.concatenate(outs, axis=0).T
            o_ref[:, (hd // 2) * 2 * V_HEAD:(hd // 2 + 1) * 2 * V_HEAD] = o_pair.astype(BF16)
            outs = []


def _attention(qt, k, vt):
    B, nq, S = qt.shape
    nv = N_HEADS * V_HEAD
    return pl.pallas_call(
        _attn_kernel,
        out_shape=jax.ShapeDtypeStruct((B, S, nv), BF16),
        grid=(B, S // TQ),
        in_specs=[
            pl.BlockSpec((None, nq, TQ), lambda b, i: (b, 0, i)),
            pl.BlockSpec((None, S, nq), lambda b, i: (b, 0, 0)),
            pl.BlockSpec((None, nv, S), lambda b, i: (b, 0, 0)),
        ],
        out_specs=pl.BlockSpec((None, TQ, nv), lambda b, i: (b, i, 0)),
        compiler_params=pltpu.CompilerParams(
            dimension_semantics=("parallel", "parallel"), vmem_limit_bytes=VMEM_LIMIT),
        name="mla_attention",
    )(qt, k, vt)


def _relayout_w_gate(wt_ref, out_ref):
    o_gate = Q_LORA + KV_LORA + QK_ROPE + 2 * CONV_CH
    _transpose_rows(wt_ref, o_gate, out_ref.shape[1], out_ref, 0)


def _mix_kernel(x_ref, mod_ref, z_ref, zprev_ref, znext_ref, o_ref, h_ref, w_ref,
                wdw_ref, bdw_ref, gln_ref, bln_ref, wco_ref, wao_ref, wout_ref, *rest, n_cast):
    cast_in = rest[:n_cast]
    out_ref = rest[n_cast]
    cast_out = rest[n_cast + 1:2 * n_cast + 1]
    zp_ref, zs_ref, conv_ref, wgate_ref = rest[2 * n_cast + 1:]
    for src, dst in zip(cast_in, cast_out):
        dst[...] = src[...].astype(BF16)
    D = D_MODEL
    ts = TS_MIX
    i = pl.program_id(1)
    n_i = pl.num_programs(1)
    pl.when(_first_grid_step())(lambda: _relayout_w_gate(w_ref, wgate_ref))

    zp_ref[0:HALO, :] = jnp.where(i > 0, zprev_ref[...], 0.0)
    zp_ref[HALO:HALO + ts, :] = z_ref[...]
    zp_ref[HALO + ts:2 * HALO + ts, :] = jnp.where(i < n_i - 1, znext_ref[...], 0.0)

    n_shift = zs_ref.shape[1]
    for s in range(1, SUBLANES):
        zs_ref[s - 1] = zp_ref[s:s + n_shift, :]

    gate_logits = _dot(h_ref[...], wgate_ref[...])
    y_a = _dot(o_ref[...], wao_ref[...])

    row_chunk = 64
    base = HALO - CONV_K // 2
    for cb in range(CONV_CH // LANES):
        cs = slice(cb * LANES, (cb + 1) * LANES)
        for rb in range(ts // row_chunk):
            r0 = rb * row_chunk
            acc = jnp.broadcast_to(bdw_ref[:, cs], (row_chunk, LANES))
            for kk in range(CONV_K):
                s = (base + kk) % SUBLANES
                a = r0 + base + kk - s
                src = zp_ref if s == 0 else zs_ref.at[s - 1]
                acc = acc + wdw_ref[kk:kk + 1, cs] * src[a:a + row_chunk, cs]
            conv_ref[r0:r0 + row_chunk, cs] = acc

    zc = conv_ref[...]
    mu = jnp.mean(zc, axis=-1, keepdims=True)
    zd = zc - mu
    var = jnp.mean(zd * zd, axis=-1, keepdims=True)
    zn = zd * lax.rsqrt(var + EPS_LN) * gln_ref[...] + bln_ref[...]
    zs = (zn * _sigmoid(zn)).astype(BF16)
    y_b = _dot(zs, wco_ref[...])
    gates = _sigmoid(gate_logits)
    merged = (gates[:, 0:D] * y_a + gates[:, D:2 * D] * y_b).astype(BF16)
    gate_m = mod_ref[:, 2 * D:3 * D]
    out_ref[...] = x_ref[...] + gate_m * _dot(merged, wout_ref[...])


def _cast_chunk_spec(rows, cols, n_i, n_steps):
    every = 1
    while (rows * every) % n_steps or (rows * every // n_steps) % (2 * SUBLANES):
        every *= 2
    chunk = rows * every // n_steps
    return pl.BlockSpec((chunk, cols), lambda b, i: ((b * n_i + i) // every, 0))


def _mix(x, mod3, z, o, h, w_in, w_dw, b_dw, g_ln, b_ln, w_co, w_ao, w_out, cast_ws):
    B, S, D = x.shape
    ts = TS_MIX
    hb = ts // HALO
    n_halo = S // HALO
    n_i = S // ts
    tok = lambda w: pl.BlockSpec((None, ts, w), lambda b, i: (b, i, 0))
    cast_specs = [_cast_chunk_spec(w.shape[0], w.shape[1], n_i, B * n_i) for w in cast_ws]
    return pl.pallas_call(
        functools.partial(_mix_kernel, n_cast=len(cast_ws)),
        out_shape=(jax.ShapeDtypeStruct((B, S, D), F32),
                   *[jax.ShapeDtypeStruct(w.shape, BF16) for w in cast_ws]),
        grid=(B, S // ts),
        in_specs=[
            tok(D),
            pl.BlockSpec((None, 1, N_MOD * D), lambda b, i: (b, 0, 0)),
            tok(CONV_CH),
            pl.BlockSpec((None, HALO, CONV_CH), lambda b, i: (b, jnp.maximum(i * hb - 1, 0), 0)),
            pl.BlockSpec((None, HALO, CONV_CH),
                         lambda b, i: (b, jnp.minimum((i + 1) * hb, n_halo - 1), 0)),
            tok(N_HEADS * V_HEAD),
            tok(D),
            _const_spec(w_in.shape),
            _const_spec(w_dw.shape),
            _const_spec(b_dw.shape),
            _const_spec(g_ln.shape),
            _const_spec(b_ln.shape),
            _const_spec(w_co.shape),
            _const_spec(w_ao.shape),
            _const_spec(w_out.shape),
            *cast_specs,
        ],
        out_specs=(tok(D), *cast_specs),
        scratch_shapes=[
            pltpu.VMEM((ts + 2 * HALO, CONV_CH), F32),
            pltpu.VMEM((SUBLANES - 1, ts + 2 * HALO - SUBLANES, CONV_CH), F32),
            pltpu.VMEM((ts, CONV_CH), F32),
            pltpu.VMEM((D, 2 * D), BF16),
        ],
        compiler_params=pltpu.CompilerParams(
            dimension_semantics=("arbitrary", "arbitrary"), vmem_limit_bytes=VMEM_LIMIT),
        name="conv_merge_out",
    )(x, mod3, z, z, z, o, h, w_in, w_dw, b_dw, g_ln, b_ln, w_co, w_ao, w_out, *cast_ws)


def _ffn_kernel(x_ref, mod_ref, gffn_ref, wg_ref, wu_ref, wd_ref, gfin_ref, out_ref, *, final_norm):
    D = D_MODEL
    shift = mod_ref[:, 3 * D:4 * D]
    scale = mod_ref[:, 4 * D:5 * D]
    gate = mod_ref[:, 5 * D:6 * D]
    rows = x_ref.shape[0] // ROW_SUBTILES
    for j in range(ROW_SUBTILES):
        rs = slice(j * rows, (j + 1) * rows)
        x = x_ref[rs, :]
        h = (_rms(x, gffn_ref[...]) * (1.0 + scale) + shift).astype(BF16)
        g = _dot(h, wg_ref[...])
        u = _dot(h, wu_ref[...])
        a = (g * _sigmoid(g) * u).astype(BF16)
        x2 = x + gate * _dot(a, wd_ref[...])
        out_ref[rs, :] = _rms(x2, gfin_ref[...]) if final_norm else x2


def _ffn(x, mod3, g_ffn, w_gate, w_up, w_down, g_final, final_norm):
    B, S, D = x.shape
    tm = TM_FFN
    tok = pl.BlockSpec((None, tm, D), lambda b, i: (b, i, 0))
    return pl.pallas_call(
        functools.partial(_ffn_kernel, final_norm=final_norm),
        out_shape=jax.ShapeDtypeStruct((B, S, D), F32),
        grid=(B, S // tm),
        in_specs=[
            tok,
            pl.BlockSpec((None, 1, N_MOD * D), lambda b, i: (b, 0, 0)),
            _const_spec(g_ffn.shape),
            _const_spec(w_gate.shape),
            _const_spec(w_up.shape),
            _const_spec(w_down.shape),
            _const_spec(g_final.shape),
        ],
        out_specs=tok,
        compiler_params=pltpu.CompilerParams(
            dimension_semantics=("parallel", "parallel"), vmem_limit_bytes=VMEM_LIMIT),
        name="swiglu_final_norm",
    )(x, mod3, g_ffn, w_gate, w_up, w_down, g_final)


def _prep_w_q(w_q_up):
    r = w_q_up.shape[0]
    w = w_q_up.reshape(r, N_HEADS, QK_HEAD)
    nope = w[..., :QK_NOPE]
    r1 = w[..., QK_NOPE:QK_NOPE + HALF_ROPE]
    r2 = w[..., QK_NOPE + HALF_ROPE:]
    return jnp.concatenate([nope, r1, r2, -r2, r1], axis=-1).reshape(r, N_HEADS * HEAD_PAD).T.astype(BF16)


def _prep_w_kv(w_kv_up):
    r = w_kv_up.shape[0]
    w = w_kv_up.reshape(r, N_HEADS, QK_NOPE + V_HEAD)
    k_nope = w[..., :QK_NOPE]
    v = w[..., QK_NOPE:]
    k_pad = jnp.concatenate(
        [k_nope, jnp.zeros((r, N_HEADS, HEAD_PAD - QK_NOPE), w.dtype)], axis=-1)
    wk = k_pad.reshape(r, N_HEADS * HEAD_PAD).astype(BF16)
    wvt = v.reshape(r, N_HEADS * V_HEAD).T.astype(BF16)
    return wk, wvt


def kernel(x, c, positions, w_ada, b_ada, g_norm_mix, w_in, g_q_a, w_q_up, g_kv_a, w_kv_up,
           w_attn_o, w_dw, b_dw, g_conv_ln, b_conv_ln, w_conv_out, w_out, g_norm_ffn,
           w_ffn_gate, w_ffn_up, w_ffn_down, g_final):
    B, S, D = x.shape
    depth = w_ada.shape[0]
    cos_t, sin_t = _rope_tables(positions)
    for l in range(depth):
        mod3 = _modulation(c, w_ada[l], b_ada[l][None, :]).reshape(B, 1, N_MOD * D)
        wk, wvt = _prep_w_kv(w_kv_up[l])
        w_in_t = w_in[l].T
        q, k, vt, z, h, w_co_b, w_ao_b, w_out_b = _inproj(
            x, mod3, g_norm_mix[l][None, :], w_in_t, g_q_a[l][None, :],
            _prep_w_q(w_q_up[l]), g_kv_a[l][None, :], wk, wvt, cos_t, sin_t,
            (w_conv_out[l], w_attn_o[l], w_out[l]))
        o = _attention(q, k, vt)
        x, w_gate_b, w_up_b, w_down_b = _mix(
            x, mod3, z, o, h, w_in_t, w_dw[l], b_dw[l][None, :], g_conv_ln[l][None, :],
            b_conv_ln[l][None, :], w_co_b, w_ao_b, w_out_b,
            (w_ffn_gate[l], w_ffn_up[l], w_ffn_down[l]))
        x = _ffn(x, mod3, g_norm_ffn[l][None, :], w_gate_b, w_up_b, w_down_b, g_final[None, :],
                 final_norm=(l == depth - 1))
    return x
```

```python
import functools
import math

import jax
import jax.numpy as jnp
from jax import lax
from jax.experimental import pallas as pl
from jax.experimental.pallas import tpu as pltpu

F32 = jnp.float32
BF16 = jnp.bfloat16

D_MODEL = 1024
N_HEADS = 8
Q_LORA = 256
KV_LORA = 128
QK_NOPE = 64
QK_ROPE = 32
HALF_ROPE = QK_ROPE // 2
V_HEAD = 64
QK_HEAD = QK_NOPE + QK_ROPE
ATTN_SCALE = 1.0 / math.sqrt(QK_HEAD)
Q_SCALE = ATTN_SCALE * math.log2(math.e)
ROPE_THETA = 10000.0
CONV_CH = 512
CONV_K = 31
N_MOD = 6
EPS_RMS = 1e-6
EPS_LN = 1e-5

LANES = 128
SUBLANES = 8
HEAD_PAD = LANES
VMEM_LIMIT = 56 * 1024 * 1024

C_QA = 0
C_KVA = C_QA + Q_LORA
C_KR = C_KVA + KV_LORA
C_CONV = C_KR + HEAD_PAD

TS_IN = 1024
TQ = 512
KEY_CHUNK = 1024
TS_MIX = 512
MIX_SUBTILES = 4
HALO = 16
TM_FFN = 512
ROW_SUBTILES = 2


def _sigmoid(x):
    return 1.0 / (1.0 + jnp.exp(-x))


def _rms(x, g):
    return x * lax.rsqrt(jnp.mean(x * x, axis=-1, keepdims=True) + EPS_RMS) * g


def _dot(a, b):
    return jnp.dot(a, b, preferred_element_type=F32)


def _const_spec(shape):
    nd = len(shape)
    return pl.BlockSpec(shape, lambda *_: (0,) * nd, pipeline_mode=pl.Buffered(1))


def _rope_kernel(pos_ref, freq_ref, cos_ref, sin_ref):
    ang = pos_ref[...] * freq_ref[...]
    cos_ref[...] = jnp.cos(ang)
    sin_ref[...] = jnp.sin(ang)


def _rope_tables(positions):
    B, S = positions.shape
    inv_freq = ROPE_THETA ** (-jnp.arange(0, QK_ROPE, 2, dtype=F32) / QK_ROPE)
    pos = positions.reshape(1, B * S).astype(F32)
    cos_t, sin_t = pl.pallas_call(
        _rope_kernel,
        out_shape=(jax.ShapeDtypeStruct((HALF_ROPE, B * S), F32),) * 2,
        name="rope_tables",
    )(pos, inv_freq[:, None])
    return cos_t, sin_t


def _rope_tile_t(t, ctab_t, stab_t):
    swapped = jnp.concatenate([t[0:QK_NOPE], t[QK_HEAD:HEAD_PAD], t[QK_HEAD:HEAD_PAD]], axis=0)
    return t * ctab_t + swapped * stab_t


def _mod_kernel(c_ref, w_ref, b_ref, o_ref):
    c = c_ref[...]
    c_act = (c * _sigmoid(c)).astype(BF16)
    o_ref[...] = _dot(c_act, w_ref[...].astype(BF16)) + b_ref[...]


def _modulation(c, w_ada, b_ada):
    B, D = c.shape
    n = w_ada.shape[1]
    bn = 1536
    return pl.pallas_call(
        _mod_kernel,
        out_shape=jax.ShapeDtypeStruct((B, n), F32),
        grid=(n // bn,),
        in_specs=[
            pl.BlockSpec((B, D), lambda j: (0, 0)),
            pl.BlockSpec((D, bn), lambda j: (0, j)),
            pl.BlockSpec((1, bn), lambda j: (0, j)),
        ],
        out_specs=pl.BlockSpec((B, bn), lambda j: (0, j)),
        compiler_params=pltpu.CompilerParams(
            dimension_semantics=("parallel",), vmem_limit_bytes=VMEM_LIMIT),
        name="adaln_modulation",
    )(c, w_ada, b_ada)


def _mixer_input(x, mod_ref, gmix_ref):
    D = D_MODEL
    shift = mod_ref[:, 0:D]
    scale = mod_ref[:, D:2 * D]
    return (_rms(x, gmix_ref[...]) * (1.0 + scale) + shift).astype(BF16)


def _first_grid_step():
    return jnp.logical_and(pl.program_id(0) == 0, pl.program_id(1) == 0)


def _transpose_rows(wt_ref, row0, n_rows, out_ref, col0):
    for j in range(n_rows // LANES):
        rows = wt_ref[row0 + j * LANES:row0 + (j + 1) * LANES, :]
        out_ref[:, col0 + j * LANES:col0 + (j + 1) * LANES] = rows.T.astype(BF16)


def _relayout_w_proj(wt_ref, out_ref):
    o_kr = Q_LORA + KV_LORA
    o_conv = o_kr + QK_ROPE
    _transpose_rows(wt_ref, 0, o_kr, out_ref, C_QA)
    k1 = wt_ref[o_kr:o_kr + HALF_ROPE, :]
    k2 = wt_ref[o_kr + HALF_ROPE:o_conv, :]
    key_tile_t = jnp.concatenate([jnp.zeros((QK_NOPE, k1.shape[1]), F32), k1, k2, -k2, k1], axis=0)
    out_ref[:, C_KR:C_KR + HEAD_PAD] = key_tile_t.T.astype(BF16)
    _transpose_rows(wt_ref, o_conv, 2 * CONV_CH, out_ref, C_CONV)


def _inproj_kernel(x_ref, mod_ref, gmix_ref, w_ref, gq_ref, wq_ref, gkv_ref, wk_ref, wvt_ref,
                   cost_ref, sint_ref, *rest, n_cast):
    cast_in = rest[:n_cast]
    q_ref, k_ref, vt_ref, z_ref, h_ref = rest[n_cast:n_cast + 5]
    cast_out = rest[n_cast + 5:2 * n_cast + 5]
    (win_ref,) = rest[2 * n_cast + 5:]
    for src, dst in zip(cast_in, cast_out):
        dst[...] = src[...].astype(BF16)
    pl.when(_first_grid_step())(lambda: _relayout_w_proj(w_ref, win_ref))
    h = _mixer_input(x_ref[...], mod_ref, gmix_ref)
    h_ref[...] = h
    cos_t = cost_ref[...]
    sin_t = sint_ref[...]
    n_pos = cos_t.shape[1]
    ctab_t = jnp.concatenate([jnp.ones((QK_NOPE, n_pos), F32), cos_t, cos_t,
                              jnp.zeros((HEAD_PAD - QK_HEAD, n_pos), F32)], axis=0)
    stab_t = jnp.concatenate([jnp.zeros((QK_NOPE, n_pos), F32), sin_t, sin_t,
                              jnp.zeros((HEAD_PAD - QK_HEAD, n_pos), F32)], axis=0)

    proj = _dot(h, win_ref[...])

    z_ref[...] = (proj[:, C_CONV:C_CONV + CONV_CH]
                  * _sigmoid(proj[:, C_CONV + CONV_CH:C_CONV + 2 * CONV_CH]))

    qn = _rms(proj[:, C_QA:C_QA + Q_LORA], gq_ref[...]).astype(BF16)
    q_all_t = lax.dot_general(wq_ref[...], qn, (((1,), (1,)), ((), ())), preferred_element_type=F32)
    for hd in range(N_HEADS):
        sl = slice(hd * HEAD_PAD, (hd + 1) * HEAD_PAD)
        q_ref[sl, :] = (_rope_tile_t(q_all_t[sl, :], ctab_t, stab_t) * Q_SCALE).astype(BF16)

    kvn = _rms(proj[:, C_KVA:C_KVA + KV_LORA], gkv_ref[...]).astype(BF16)
    k_pad = _dot(kvn, wk_ref[...])
    vt_ref[...] = lax.dot_general(wvt_ref[...], kvn, (((1,), (1,)), ((), ())),
                                  preferred_element_type=F32).astype(BF16)
    kr_rot = _rope_tile_t(proj[:, C_KR:C_KR + HEAD_PAD].T, ctab_t, stab_t).T
    for hd in range(N_HEADS):
        sl = slice(hd * HEAD_PAD, (hd + 1) * HEAD_PAD)
        k_ref[:, sl] = (k_pad[:, sl] + kr_rot).astype(BF16)


def _inproj(x, mod3, g_mix, w_in, g_q, wq_t, g_kv, wk, wvt, cos_t, sin_t, cast_ws):
    B, S, D = x.shape
    ts = TS_IN
    nq = N_HEADS * HEAD_PAD
    nv = N_HEADS * V_HEAD
    tok = lambda w: pl.BlockSpec((None, ts, w), lambda b, i: (b, i, 0))
    n_i = S // ts
    table = pl.BlockSpec((HALF_ROPE, ts), lambda b, i: (0, b * n_i + i))
    cast_specs = [_cast_chunk_spec(w.shape[0], w.shape[1], n_i, B * n_i) for w in cast_ws]
    return pl.pallas_call(
        functools.partial(_inproj_kernel, n_cast=len(cast_ws)),
        out_shape=(
            jax.ShapeDtypeStruct((B, nq, S), BF16),
            jax.ShapeDtypeStruct((B, S, nq), BF16),
            jax.ShapeDtypeStruct((B, nv, S), BF16),
            jax.ShapeDtypeStruct((B, S, CONV_CH), F32),
            jax.ShapeDtypeStruct((B, S, D), BF16),
            *[jax.ShapeDtypeStruct(w.shape, BF16) for w in cast_ws],
        ),
        grid=(B, S // ts),
        in_specs=[
            tok(D),
            pl.BlockSpec((None, 1, N_MOD * D), lambda b, i: (b, 0, 0)),
            _const_spec(g_mix.shape),
            _const_spec(w_in.shape),
            _const_spec(g_q.shape),
            _const_spec(wq_t.shape),
            _const_spec(g_kv.shape),
            _const_spec(wk.shape),
            _const_spec(wvt.shape),
            table,
            table,
            *cast_specs,
        ],
        out_specs=(pl.BlockSpec((None, nq, ts), lambda b, i: (b, 0, i)), tok(nq),
                   pl.BlockSpec((None, nv, ts), lambda b, i: (b, 0, i)),
                   tok(CONV_CH), tok(D), *cast_specs),
        scratch_shapes=[pltpu.VMEM((D, C_CONV + 2 * CONV_CH), BF16)],
        compiler_params=pltpu.CompilerParams(
            dimension_semantics=("arbitrary", "arbitrary"), vmem_limit_bytes=VMEM_LIMIT),
        name="input_projection",
    )(x, mod3, g_mix, w_in, g_q, wq_t, g_kv, wk, wvt, cos_t, sin_t, *cast_ws)


def _attn_kernel(qt_ref, k_ref, vt_ref, o_ref):
    n_kc = k_ref.shape[0] // KEY_CHUNK

    def score_chunk(hd, c):
        sl = slice(hd * HEAD_PAD, (hd + 1) * HEAD_PAD)
        ks = slice(c * KEY_CHUNK, (c + 1) * KEY_CHUNK)
        return _dot(k_ref[ks, sl], qt_ref[sl, :])

    def col_max(chunks):
        m = jnp.max(chunks[0], axis=0, keepdims=True)
        for st in chunks[1:]:
            m = jnp.maximum(m, jnp.max(st, axis=0, keepdims=True))
        return m

    st_next = [score_chunk(0, c) for c in range(n_kc)]
    outs = []
    for hd in range(N_HEADS):
        st_cur, st_next = st_next, []
        m = col_max(st_cur)
        l = acc = None
        for c in range(n_kc):
            if hd + 1 < N_HEADS:
                st_next.append(score_chunk(hd + 1, c))
            ks = slice(c * KEY_CHUNK, (c + 1) * KEY_CHUNK)
            p = jnp.exp2(st_cur[c] - m)
            ls = jnp.sum(p, axis=0, keepdims=True)
            pv = _dot(vt_ref[hd * V_HEAD:(hd + 1) * V_HEAD, ks], p.astype(BF16))
            l, acc = (ls, pv) if c == 0 else (l + ls, acc + pv)
        outs.append(acc / l)
        if hd % 2 == 1:
            o_pair = jnp.concatenate(outs, axis=0).T
            o_ref[:, (hd // 2) * 2 * V_HEAD:(hd // 2 + 1) * 2 * V_HEAD] = o_pair.astype(BF16)
            outs = []


def _attention(qt, k, vt):
    B, nq, S = qt.shape
    nv = N_HEADS * V_HEAD
    return pl.pallas_call(
        _attn_kernel,
        out_shape=jax.ShapeDtypeStruct((B, S, nv), BF16),
        grid=(B, S // TQ),
        in_specs=[
            pl.BlockSpec((None, nq, TQ), lambda b, i: (b, 0, i)),
            pl.BlockSpec((None, S, nq), lambda b, i: (b, 0, 0)),
            pl.BlockSpec((None, nv, S), lambda b, i: (b, 0, 0)),
        ],
        out_specs=pl.BlockSpec((None, TQ, nv), lambda b, i: (b, i, 0)),
        compiler_params=pltpu.CompilerParams(
            dimension_semantics=("parallel", "parallel"), vmem_limit_bytes=VMEM_LIMIT),
        name="mla_attention",
    )(qt, k, vt)


def _relayout_w_gate(wt_ref, out_ref):
    o_gate = Q_LORA + KV_LORA + QK_ROPE + 2 * CONV_CH
    _transpose_rows(wt_ref, o_gate, out_ref.shape[1], out_ref, 0)


def _mix_kernel(x_ref, mod_ref, z_ref, zprev_ref, znext_ref, o_ref, h_ref, w_ref,
                wdw_ref, bdw_ref, gln_ref, bln_ref, wco_ref, wao_ref, wout_ref, *rest, n_cast):
    cast_in = rest[:n_cast]
    out_ref = rest[n_cast]
    cast_out = rest[n_cast + 1:2 * n_cast + 1]
    zp_ref, zs_ref, conv_ref, wgate_ref = rest[2 * n_cast + 1:]
    for src, dst in zip(cast_in, cast_out):
        dst[...] = src[...].astype(BF16)
    D = D_MODEL
    ts = TS_MIX
    i = pl.program_id(1)
    n_i = pl.num_programs(1)
    pl.when(_first_grid_step())(lambda: _relayout_w_gate(w_ref, wgate_ref))

    zp_ref[0:HALO, :] = jnp.where(i > 0, zprev_ref[...], 0.0)
    zp_ref[HALO:HALO + ts, :] = z_ref[...]
    zp_ref[HALO + ts:2 * HALO + ts, :] = jnp.where(i < n_i - 1, znext_ref[...], 0.0)

    n_shift = zs_ref.shape[1]
    for s in range(1, SUBLANES):
        zs_ref[s - 1] = zp_ref[s:s + n_shift, :]

    row_chunk = 64
    base = HALO - CONV_K // 2
    gate_m = mod_ref[:, 2 * D:3 * D]
    rows = ts // MIX_SUBTILES

    def conv_chunk(r0, cb):
        cs = slice(cb * LANES, (cb + 1) * LANES)
        acc = jnp.broadcast_to(bdw_ref[:, cs], (row_chunk, LANES))
        for kk in range(CONV_K):
            s = (base + kk) % SUBLANES
            a = r0 + base + kk - s
            src = zp_ref if s == 0 else zs_ref.at[s - 1]
            acc = acc + wdw_ref[kk:kk + 1, cs] * src[a:a + row_chunk, cs]
        conv_ref[r0:r0 + row_chunk, cs] = acc

    def conv_pieces(q):
        chunks = [(q * rows + rb * row_chunk, cb)
                  for cb in range(CONV_CH // LANES) for rb in range(rows // row_chunk)]
        n = -(-len(chunks) // 4)
        return [chunks[j * n:(j + 1) * n] for j in range(4)]

    def run(piece):
        for r0, cb in piece:
            conv_chunk(r0, cb)

    for piece in conv_pieces(0):
        run(piece)
    for q in range(MIX_SUBTILES):
        nxt = conv_pieces(q + 1) if q + 1 < MIX_SUBTILES else [[], [], [], []]
        rs = slice(q * rows, (q + 1) * rows)
        zc = conv_ref[rs, :]
        mu = jnp.mean(zc, axis=-1, keepdims=True)
        zd = zc - mu
        var = jnp.mean(zd * zd, axis=-1, keepdims=True)
        zn = zd * lax.rsqrt(var + EPS_LN) * gln_ref[...] + bln_ref[...]
        zs = (zn * _sigmoid(zn)).astype(BF16)
        y_b = _dot(zs, wco_ref[...])
        run(nxt[0])
        y_a = _dot(o_ref[rs, :], wao_ref[...])
        run(nxt[1])
        gates = _sigmoid(_dot(h_ref[rs, :], wgate_ref[...]))
        run(nxt[2])
        merged = (gates[:, 0:D] * y_a + gates[:, D:2 * D] * y_b).astype(BF16)
        out_ref[rs, :] = x_ref[rs, :] + gate_m * _dot(merged, wout_ref[...])
        run(nxt[3])


def _cast_chunk_spec(rows, cols, n_i, n_steps):
    every = 1
    while (rows * every) % n_steps or (rows * every // n_steps) % (2 * SUBLANES):
        every *= 2
    chunk = rows * every // n_steps
    return pl.BlockSpec((chunk, cols), lambda b, i: ((b * n_i + i) // every, 0))


def _mix(x, mod3, z, o, h, w_in, w_dw, b_dw, g_ln, b_ln, w_co, w_ao, w_out, cast_ws):
    B, S, D = x.shape
    ts = TS_MIX
    hb = ts // HALO
    n_halo = S // HALO
    n_i = S // ts
    tok = lambda w: pl.BlockSpec((None, ts, w), lambda b, i: (b, i, 0))
    cast_specs = [_cast_chunk_spec(w.shape[0], w.shape[1], n_i, B * n_i) for w in cast_ws]
    return pl.pallas_call(
        functools.partial(_mix_kernel, n_cast=len(cast_ws)),
        out_shape=(jax.ShapeDtypeStruct((B, S, D), F32),
                   *[jax.ShapeDtypeStruct(w.shape, BF16) for w in cast_ws]),
        grid=(B, S // ts),
        in_specs=[
            tok(D),
            pl.BlockSpec((None, 1, N_MOD * D), lambda b, i: (b, 0, 0)),
            tok(CONV_CH),
            pl.BlockSpec((None, HALO, CONV_CH), lambda b, i: (b, jnp.maximum(i * hb - 1, 0), 0)),
            pl.BlockSpec((None, HALO, CONV_CH),
                         lambda b, i: (b, jnp.minimum((i + 1) * hb, n_halo - 1), 0)),
            tok(N_HEADS * V_HEAD),
            tok(D),
            _const_spec(w_in.shape),
            _const_spec(w_dw.shape),
            _const_spec(b_dw.shape),
            _const_spec(g_ln.shape),
            _const_spec(b_ln.shape),
            _const_spec(w_co.shape),
            _const_spec(w_ao.shape),
            _const_spec(w_out.shape),
            *cast_specs,
        ],
        out_specs=(tok(D), *cast_specs),
        scratch_shapes=[
            pltpu.VMEM((ts + 2 * HALO, CONV_CH), F32),
            pltpu.VMEM((SUBLANES - 1, ts + 2 * HALO - SUBLANES, CONV_CH), F32),
            pltpu.VMEM((ts, CONV_CH), F32),
            pltpu.VMEM((D, 2 * D), BF16),
        ],
        compiler_params=pltpu.CompilerParams(
            dimension_semantics=("arbitrary", "arbitrary"), vmem_limit_bytes=VMEM_LIMIT),
        name="conv_merge_out",
    )(x, mod3, z, z, z, o, h, w_in, w_dw, b_dw, g_ln, b_ln, w_co, w_ao, w_out, *cast_ws)


def _ffn_kernel(x_ref, mod_ref, gffn_ref, wg_ref, wu_ref, wd_ref, gfin_ref, out_ref, *, final_norm):
    D = D_MODEL
    shift = mod_ref[:, 3 * D:4 * D]
    scale = mod_ref[:, 4 * D:5 * D]
    gate = mod_ref[:, 5 * D:6 * D]
    rows = x_ref.shape[0] // ROW_SUBTILES
    for j in range(ROW_SUBTILES):
        rs = slice(j * rows, (j + 1) * rows)
        x = x_ref[rs, :]
        h = (_rms(x, gffn_ref[...]) * (1.0 + scale) + shift).astype(BF16)
        g = _dot(h, wg_ref[...])
        u = _dot(h, wu_ref[...])
        a = (g * _sigmoid(g) * u).astype(BF16)
        x2 = x + gate * _dot(a, wd_ref[...])
        out_ref[rs, :] = _rms(x2, gfin_ref[...]) if final_norm else x2


def _ffn(x, mod3, g_ffn, w_gate, w_up, w_down, g_final, final_norm):
    B, S, D = x.shape
    tm = TM_FFN
    tok = pl.BlockSpec((None, tm, D), lambda b, i: (b, i, 0))
    return pl.pallas_call(
        functools.partial(_ffn_kernel, final_norm=final_norm),
        out_shape=jax.ShapeDtypeStruct((B, S, D), F32),
        grid=(B, S // tm),
        in_specs=[
            tok,
            pl.BlockSpec((None, 1, N_MOD * D), lambda b, i: (b, 0, 0)),
            _const_spec(g_ffn.shape),
            _const_spec(w_gate.shape),
            _const_spec(w_up.shape),
            _const_spec(w_down.shape),
            _const_spec(g_final.shape),
        ],
        out_specs=tok,
        compiler_params=pltpu.CompilerParams(
            dimension_semantics=("parallel", "parallel"), vmem_limit_bytes=VMEM_LIMIT),
        name="swiglu_final_norm",
    )(x, mod3, g_ffn, w_gate, w_up, w_down, g_final)


def _prep_w_q(w_q_up):
    r = w_q_up.shape[0]
    w = w_q_up.reshape(r, N_HEADS, QK_HEAD)
    nope = w[..., :QK_NOPE]
    r1 = w[..., QK_NOPE:QK_NOPE + HALF_ROPE]
    r2 = w[..., QK_NOPE + HALF_ROPE:]
    return jnp.concatenate([nope, r1, r2, -r2, r1], axis=-1).reshape(r, N_HEADS * HEAD_PAD).T.astype(BF16)


def _prep_w_kv(w_kv_up):
    r = w_kv_up.shape[0]
    w = w_kv_up.reshape(r, N_HEADS, QK_NOPE + V_HEAD)
    k_nope = w[..., :QK_NOPE]
    v = w[..., QK_NOPE:]
    k_pad = jnp.concatenate(
        [k_nope, jnp.zeros((r, N_HEADS, HEAD_PAD - QK_NOPE), w.dtype)], axis=-1)
    wk = k_pad.reshape(r, N_HEADS * HEAD_PAD).astype(BF16)
    wvt = v.reshape(r, N_HEADS * V_HEAD).T.astype(BF16)
    return wk, wvt


def kernel(x, c, positions, w_ada, b_ada, g_norm_mix, w_in, g_q_a, w_q_up, g_kv_a, w_kv_up,
           w_attn_o, w_dw, b_dw, g_conv_ln, b_conv_ln, w_conv_out, w_out, g_norm_ffn,
           w_ffn_gate, w_ffn_up, w_ffn_down, g_final):
    B, S, D = x.shape
    depth = w_ada.shape[0]
    cos_t, sin_t = _rope_tables(positions)
    for l in range(depth):
        mod3 = _modulation(c, w_ada[l], b_ada[l][None, :]).reshape(B, 1, N_MOD * D)
        wk, wvt = _prep_w_kv(w_kv_up[l])
        w_in_t = w_in[l].T
        q, k, vt, z, h, w_co_b, w_ao_b, w_out_b = _inproj(
            x, mod3, g_norm_mix[l][None, :], w_in_t, g_q_a[l][None, :],
            _prep_w_q(w_q_up[l]), g_kv_a[l][None, :], wk, wvt, cos_t, sin_t,
            (w_conv_out[l], w_attn_o[l], w_out[l]))
        o = _attention(q, k, vt)
        x, w_gate_b, w_up_b, w_down_b = _mix(
            x, mod3, z, o, h, w_in_t, w_dw[l], b_dw[l][None, :], g_conv_ln[l][None, :],
            b_conv_ln[l][None, :], w_co_b, w_ao_b, w_out_b,
            (w_ffn_gate[l], w_ffn_up[l], w_ffn_down[l]))
        x = _ffn(x, mod3, g_norm_ffn[l][None, :], w_gate_b, w_up_b, w_down_b, g_final[None, :],
                 final_norm=(l == depth - 1))
    return x
```

```python
import functools
import math

import jax
import jax.numpy as jnp
from jax import lax
from jax.experimental import pallas as pl
from jax.experimental.pallas import tpu as pltpu

F32 = jnp.float32
BF16 = jnp.bfloat16

D_MODEL = 1024
N_HEADS = 8
Q_LORA = 256
KV_LORA = 128
QK_NOPE = 64
QK_ROPE = 32
HALF_ROPE = QK_ROPE // 2
V_HEAD = 64
QK_HEAD = QK_NOPE + QK_ROPE
ATTN_SCALE = 1.0 / math.sqrt(QK_HEAD)
Q_SCALE = ATTN_SCALE * math.log2(math.e)
ROPE_THETA = 10000.0
CONV_CH = 512
CONV_K = 31
N_MOD = 6
EPS_RMS = 1e-6
EPS_LN = 1e-5

LANES = 128
SUBLANES = 8
HEAD_PAD = LANES
VMEM_LIMIT = 56 * 1024 * 1024

C_QA = 0
C_KVA = C_QA + Q_LORA
C_KR = C_KVA + KV_LORA
C_CONV = C_KR + HEAD_PAD

TS_IN = 1024
TQ = 512
KEY_CHUNK = 1024
TS_MIX = 512
HALO = 16
TM_FFN = 512
ROW_SUBTILES = 2


def _sigmoid(x):
    return 1.0 / (1.0 + jnp.exp(-x))


def _rms(x, g):
    return x * lax.rsqrt(jnp.mean(x * x, axis=-1, keepdims=True) + EPS_RMS) * g


def _dot(a, b):
    return jnp.dot(a, b, preferred_element_type=F32)


def _const_spec(shape):
    nd = len(shape)
    return pl.BlockSpec(shape, lambda *_: (0,) * nd, pipeline_mode=pl.Buffered(1))


def _rope_tile_t(t, ctab_t, stab_t):
    swapped = jnp.concatenate([t[0:QK_NOPE], t[QK_HEAD:HEAD_PAD], t[QK_HEAD:HEAD_PAD]], axis=0)
    return t * ctab_t + swapped * stab_t


def _mod_kernel(c_ref, w_ref, b_ref, *rest, with_rope):
    if with_rope:
        pos_ref, freq_ref, o_ref, cos_ref, sin_ref = rest
        ang = pos_ref[...] * freq_ref[...]
        cos_ref[...] = jnp.cos(ang)
        sin_ref[...] = jnp.sin(ang)
    else:
        (o_ref,) = rest
    c = c_ref[...]
    c_act = (c * _sigmoid(c)).astype(BF16)
    o_ref[...] = _dot(c_act, w_ref[...].astype(BF16)) + b_ref[...]


def _modulation(c, w_ada, b_ada, positions=None):
    B, D = c.shape
    n = w_ada.shape[1]
    bn = 1536
    steps = n // bn
    with_rope = positions is not None
    in_specs = [
        pl.BlockSpec((B, D), lambda j: (0, 0)),
        pl.BlockSpec((D, bn), lambda j: (0, j)),
        pl.BlockSpec((1, bn), lambda j: (0, j)),
    ]
    out_shape = [jax.ShapeDtypeStruct((B, n), F32)]
    out_specs = [pl.BlockSpec((B, bn), lambda j: (0, j))]
    operands = [c, w_ada, b_ada]
    if with_rope:
        t = positions.size
        inv_freq = ROPE_THETA ** (-jnp.arange(0, QK_ROPE, 2, dtype=F32) / QK_ROPE)
        in_specs += [pl.BlockSpec((1, t // steps), lambda j: (0, j)),
                     pl.BlockSpec((HALF_ROPE, 1), lambda j: (0, 0))]
        table = pl.BlockSpec((HALF_ROPE, t // steps), lambda j: (0, j))
        out_shape += [jax.ShapeDtypeStruct((HALF_ROPE, t), F32)] * 2
        out_specs += [table, table]
        operands += [positions.reshape(1, t).astype(F32), inv_freq[:, None]]
    out = pl.pallas_call(
        functools.partial(_mod_kernel, with_rope=with_rope),
        out_shape=tuple(out_shape),
        grid=(steps,),
        in_specs=in_specs,
        out_specs=tuple(out_specs),
        compiler_params=pltpu.CompilerParams(
            dimension_semantics=("parallel",), vmem_limit_bytes=VMEM_LIMIT),
        name="adaln_modulation",
    )(*operands)
    return out if with_rope else out[0]


def _mixer_input(x, mod_ref, gmix_ref):
    D = D_MODEL
    shift = mod_ref[:, 0:D]
    scale = mod_ref[:, D:2 * D]
    return (_rms(x, gmix_ref[...]) * (1.0 + scale) + shift).astype(BF16)


def _first_grid_step():
    return jnp.logical_and(pl.program_id(0) == 0, pl.program_id(1) == 0)


def _transpose_rows(wt_ref, row0, n_rows, out_ref, col0):
    for j in range(n_rows // LANES):
        rows = wt_ref[row0 + j * LANES:row0 + (j + 1) * LANES, :]
        out_ref[:, col0 + j * LANES:col0 + (j + 1) * LANES] = rows.T.astype(BF16)


def _relayout_w_proj(wt_ref, out_ref):
    o_kr = Q_LORA + KV_LORA
    o_conv = o_kr + QK_ROPE
    _transpose_rows(wt_ref, 0, o_kr, out_ref, C_QA)
    k1 = wt_ref[o_kr:o_kr + HALF_ROPE, :]
    k2 = wt_ref[o_kr + HALF_ROPE:o_conv, :]
    key_tile_t = jnp.concatenate([jnp.zeros((QK_NOPE, k1.shape[1]), F32), k1, k2, -k2, k1], axis=0)
    out_ref[:, C_KR:C_KR + HEAD_PAD] = key_tile_t.T.astype(BF16)
    _transpose_rows(wt_ref, o_conv, 2 * CONV_CH, out_ref, C_CONV)


def _inproj_kernel(x_ref, mod_ref, gmix_ref, w_ref, gq_ref, wq_ref, gkv_ref, wk_ref, wvt_ref,
                   cost_ref, sint_ref, *rest, n_cast):
    cast_in = rest[:n_cast]
    q_ref, k_ref, vt_ref, z_ref, h_ref = rest[n_cast:n_cast + 5]
    cast_out = rest[n_cast + 5:2 * n_cast + 5]
    (win_ref,) = rest[2 * n_cast + 5:]
    for src, dst in zip(cast_in, cast_out):
        dst[...] = src[...].astype(BF16)
    pl.when(_first_grid_step())(lambda: _relayout_w_proj(w_ref, win_ref))
    h = _mixer_input(x_ref[...], mod_ref, gmix_ref)
    h_ref[...] = h
    cos_t = cost_ref[...]
    sin_t = sint_ref[...]
    n_pos = cos_t.shape[1]
    ctab_t = jnp.concatenate([jnp.ones((QK_NOPE, n_pos), F32), cos_t, cos_t,
                              jnp.zeros((HEAD_PAD - QK_HEAD, n_pos), F32)], axis=0)
    stab_t = jnp.concatenate([jnp.zeros((QK_NOPE, n_pos), F32), sin_t, sin_t,
                              jnp.zeros((HEAD_PAD - QK_HEAD, n_pos), F32)], axis=0)

    proj = _dot(h, win_ref[...])

    z_ref[...] = (proj[:, C_CONV:C_CONV + CONV_CH]
                  * _sigmoid(proj[:, C_CONV + CONV_CH:C_CONV + 2 * CONV_CH]))

    qn = _rms(proj[:, C_QA:C_QA + Q_LORA], gq_ref[...]).astype(BF16)
    q_all_t = lax.dot_general(wq_ref[...], qn, (((1,), (1,)), ((), ())), preferred_element_type=F32)
    for hd in range(N_HEADS):
        sl = slice(hd * HEAD_PAD, (hd + 1) * HEAD_PAD)
        q_ref[sl, :] = (_rope_tile_t(q_all_t[sl, :], ctab_t, stab_t) * Q_SCALE).astype(BF16)

    kvn = _rms(proj[:, C_KVA:C_KVA + KV_LORA], gkv_ref[...]).astype(BF16)
    k_pad = _dot(kvn, wk_ref[...])
    vt_ref[...] = lax.dot_general(wvt_ref[...], kvn, (((1,), (1,)), ((), ())),
                                  preferred_element_type=F32).astype(BF16)
    kr_rot = _rope_tile_t(proj[:, C_KR:C_KR + HEAD_PAD].T, ctab_t, stab_t).T
    for hd in range(N_HEADS):
        sl = slice(hd * HEAD_PAD, (hd + 1) * HEAD_PAD)
        k_ref[:, sl] = (k_pad[:, sl] + kr_rot).astype(BF16)


def _inproj(x, mod3, g_mix, w_in, g_q, wq_t, g_kv, wk, wvt, cos_t, sin_t, cast_ws):
    B, S, D = x.shape
    ts = TS_IN
    nq = N_HEADS * HEAD_PAD
    nv = N_HEADS * V_HEAD
    tok = lambda w: pl.BlockSpec((None, ts, w), lambda b, i: (b, i, 0))
    n_i = S // ts
    table = pl.BlockSpec((HALF_ROPE, ts), lambda b, i: (0, b * n_i + i))
    cast_specs = [_cast_chunk_spec(w.shape[0], w.shape[1], n_i, B * n_i) for w in cast_ws]
    return pl.pallas_call(
        functools.partial(_inproj_kernel, n_cast=len(cast_ws)),
        out_shape=(
            jax.ShapeDtypeStruct((B, nq, S), BF16),
            jax.ShapeDtypeStruct((B, S, nq), BF16),
            jax.ShapeDtypeStruct((B, nv, S), BF16),
            jax.ShapeDtypeStruct((B, S, CONV_CH), F32),
            jax.ShapeDtypeStruct((B, S, D), BF16),
            *[jax.ShapeDtypeStruct(w.shape, BF16) for w in cast_ws],
        ),
        grid=(B, S // ts),
        in_specs=[
            tok(D),
            pl.BlockSpec((None, 1, N_MOD * D), lambda b, i: (b, 0, 0)),
            _const_spec(g_mix.shape),
            _const_spec(w_in.shape),
            _const_spec(g_q.shape),
            _const_spec(wq_t.shape),
            _const_spec(g_kv.shape),
            _const_spec(wk.shape),
            _const_spec(wvt.shape),
            table,
            table,
            *cast_specs,
        ],
        out_specs=(pl.BlockSpec((None, nq, ts), lambda b, i: (b, 0, i)), tok(nq),
                   pl.BlockSpec((None, nv, ts), lambda b, i: (b, 0, i)),
                   tok(CONV_CH), tok(D), *cast_specs),
        scratch_shapes=[pltpu.VMEM((D, C_CONV + 2 * CONV_CH), BF16)],
        compiler_params=pltpu.CompilerParams(
            dimension_semantics=("arbitrary", "arbitrary"), vmem_limit_bytes=VMEM_LIMIT),
        name="input_projection",
    )(x, mod3, g_mix, w_in, g_q, wq_t, g_kv, wk, wvt, cos_t, sin_t, *cast_ws)


def _attn_kernel(qt_ref, k_ref, vt_ref, o_ref):
    n_kc = k_ref.shape[0] // KEY_CHUNK

    def score_chunk(hd, c):
        sl = slice(hd * HEAD_PAD, (hd + 1) * HEAD_PAD)
        ks = slice(c * KEY_CHUNK, (c + 1) * KEY_CHUNK)
        return _dot(k_ref[ks, sl], qt_ref[sl, :])

    def col_max(chunks):
        m = jnp.max(chunks[0], axis=0, keepdims=True)
        for st in chunks[1:]:
            m = jnp.maximum(m, jnp.max(st, axis=0, keepdims=True))
        return m

    st_next = [score_chunk(0, c) for c in range(n_kc)]
    outs = []
    for hd in range(N_HEADS):
        st_cur, st_next = st_next, []
        m = col_max(st_cur)
        l = acc = None
        for c in range(n_kc):
            if hd + 1 < N_HEADS:
                st_next.append(score_chunk(hd + 1, c))
            ks = slice(c * KEY_CHUNK, (c + 1) * KEY_CHUNK)
            p = jnp.exp2(st_cur[c] - m)
            ls = jnp.sum(p, axis=0, keepdims=True)
            pv = _dot(vt_ref[hd * V_HEAD:(hd + 1) * V_HEAD, ks], p.astype(BF16))
            l, acc = (ls, pv) if c == 0 else (l + ls, acc + pv)
        outs.append(acc / l)
        if hd % 2 == 1:
            o_pair = jnp.concatenate(outs, axis=0).T
            o_ref[:, (hd // 2) * 2 * V_HEAD:(hd // 2 + 1) * 2 * V_HEAD] = o_pair.astype(BF16)
            outs = []


def _attention(qt, k, vt):
    B, nq, S = qt.shape
    nv = N_HEADS * V_HEAD
    return pl.pallas_call(
        _attn_kernel,
        out_shape=jax.ShapeDtypeStruct((B, S, nv), BF16),
        grid=(B, S // TQ),
        in_specs=[
            pl.BlockSpec((None, nq, TQ), lambda b, i: (b, 0, i)),
            pl.BlockSpec((None, S, nq), lambda b, i: (b, 0, 0)),
            pl.BlockSpec((None, nv, S), lambda b, i: (b, 0, 0)),
        ],
        out_specs=pl.BlockSpec((None, TQ, nv), lambda b, i: (b, i, 0)),
        compiler_params=pltpu.CompilerParams(
            dimension_semantics=("parallel", "parallel"), vmem_limit_bytes=VMEM_LIMIT),
        name="mla_attention",
    )(qt, k, vt)


def _relayout_w_gate(wt_ref, out_ref):
    o_gate = Q_LORA + KV_LORA + QK_ROPE + 2 * CONV_CH
    _transpose_rows(wt_ref, o_gate, out_ref.shape[1], out_ref, 0)


def _mix_kernel(x_ref, mod_ref, z_ref, zprev_ref, znext_ref, o_ref, h_ref, w_ref,
                wdw_ref, bdw_ref, gln_ref, bln_ref, wco_ref, wao_ref, wout_ref, *rest, n_cast):
    cast_in = rest[:n_cast]
    out_ref = rest[n_cast]
    cast_out = rest[n_cast + 1:2 * n_cast + 1]
    zp_ref, zs_ref, conv_ref, wgate_ref = rest[2 * n_cast + 1:]
    for src, dst in zip(cast_in, cast_out):
        dst[...] = src[...].astype(BF16)
    D = D_MODEL
    ts = TS_MIX
    i = pl.program_id(1)
    n_i = pl.num_programs(1)
    pl.when(_first_grid_step())(lambda: _relayout_w_gate(w_ref, wgate_ref))

    zp_ref[0:HALO, :] = jnp.where(i > 0, zprev_ref[...], 0.0)
    zp_ref[HALO:HALO + ts, :] = z_ref[...]
    zp_ref[HALO + ts:2 * HALO + ts, :] = jnp.where(i < n_i - 1, znext_ref[...], 0.0)

    n_shift = zs_ref.shape[1]
    for s in range(1, SUBLANES):
        zs_ref[s - 1] = zp_ref[s:s + n_shift, :]

    gate_logits = _dot(h_ref[...], wgate_ref[...])
    y_a = _dot(o_ref[...], wao_ref[...])

    row_chunk = 64
    base = HALO - CONV_K // 2
    for cb in range(CONV_CH // LANES):
        cs = slice(cb * LANES, (cb + 1) * LANES)
        for rb in range(ts // row_chunk):
            r0 = rb * row_chunk
            acc = jnp.broadcast_to(bdw_ref[:, cs], (row_chunk, LANES))
            for kk in range(CONV_K):
                s = (base + kk) % SUBLANES
                a = r0 + base + kk - s
                src = zp_ref if s == 0 else zs_ref.at[s - 1]
                acc = acc + wdw_ref[kk:kk + 1, cs] * src[a:a + row_chunk, cs]
            conv_ref[r0:r0 + row_chunk, cs] = acc

    zc = conv_ref[...]
    mu = jnp.mean(zc, axis=-1, keepdims=True)
    zd = zc - mu
    var = jnp.mean(zd * zd, axis=-1, keepdims=True)
    zn = zd * lax.rsqrt(var + EPS_LN) * gln_ref[...] + bln_ref[...]
    zs = (zn * _sigmoid(zn)).astype(BF16)
    y_b = _dot(zs, wco_ref[...])
    gates = _sigmoid(gate_logits)
    merged = (gates[:, 0:D] * y_a + gates[:, D:2 * D] * y_b).astype(BF16)
    gate_m = mod_ref[:, 2 * D:3 * D]
    out_ref[...] = x_ref[...] + gate_m * _dot(merged, wout_ref[...])


def _cast_chunk_spec(rows, cols, n_i, n_steps):
    every = 1
    while (rows * every) % n_steps or (rows * every // n_steps) % (2 * SUBLANES):
        every *= 2
    chunk = rows * every // n_steps
    return pl.BlockSpec((chunk, cols), lambda b, i: ((b * n_i + i) // every, 0))


def _mix(x, mod3, z, o, h, w_in, w_dw, b_dw, g_ln, b_ln, w_co, w_ao, w_out, cast_ws):
    B, S, D = x.shape
    ts = TS_MIX
    hb = ts // HALO
    n_halo = S // HALO
    n_i = S // ts
    tok = lambda w: pl.BlockSpec((None, ts, w), lambda b, i: (b, i, 0))
    cast_specs = [_cast_chunk_spec(w.shape[0], w.shape[1], n_i, B * n_i) for w in cast_ws]
    return pl.pallas_call(
        functools.partial(_mix_kernel, n_cast=len(cast_ws)),
        out_shape=(jax.ShapeDtypeStruct((B, S, D), F32),
                   *[jax.ShapeDtypeStruct(w.shape, BF16) for w in cast_ws]),
        grid=(B, S // ts),
        in_specs=[
            tok(D),
            pl.BlockSpec((None, 1, N_MOD * D), lambda b, i: (b, 0, 0)),
            tok(CONV_CH),
            pl.BlockSpec((None, HALO, CONV_CH), lambda b, i: (b, jnp.maximum(i * hb - 1, 0), 0)),
            pl.BlockSpec((None, HALO, CONV_CH),
                         lambda b, i: (b, jnp.minimum((i + 1) * hb, n_halo - 1), 0)),
            tok(N_HEADS * V_HEAD),
            tok(D),
            _const_spec(w_in.shape),
            _const_spec(w_dw.shape),
            _const_spec(b_dw.shape),
            _const_spec(g_ln.shape),
            _const_spec(b_ln.shape),
            _const_spec(w_co.shape),
            _const_spec(w_ao.shape),
            _const_spec(w_out.shape),
            *cast_specs,
        ],
        out_specs=(tok(D), *cast_specs),
        scratch_shapes=[
            pltpu.VMEM((ts + 2 * HALO, CONV_CH), F32),
            pltpu.VMEM((SUBLANES - 1, ts + 2 * HALO - SUBLANES, CONV_CH), F32),
            pltpu.VMEM((ts, CONV_CH), F32),
            pltpu.VMEM((D, 2 * D), BF16),
        ],
        compiler_params=pltpu.CompilerParams(
            dimension_semantics=("arbitrary", "arbitrary"), vmem_limit_bytes=VMEM_LIMIT),
        name="conv_merge_out",
    )(x, mod3, z, z, z, o, h, w_in, w_dw, b_dw, g_ln, b_ln, w_co, w_ao, w_out, *cast_ws)


def _ffn_kernel(x_ref, mod_ref, gffn_ref, wg_ref, wu_ref, wd_ref, gfin_ref, out_ref, *, final_norm):
    D = D_MODEL
    shift = mod_ref[:, 3 * D:4 * D]
    scale = mod_ref[:, 4 * D:5 * D]
    gate = mod_ref[:, 5 * D:6 * D]
    rows = x_ref.shape[0] // ROW_SUBTILES
    for j in range(ROW_SUBTILES):
        rs = slice(j * rows, (j + 1) * rows)
        x = x_ref[rs, :]
        h = (_rms(x, gffn_ref[...]) * (1.0 + scale) + shift).astype(BF16)
        g = _dot(h, wg_ref[...])
        u = _dot(h, wu_ref[...])
        a = (g * _sigmoid(g) * u).astype(BF16)
        x2 = x + gate * _dot(a, wd_ref[...])
        out_ref[rs, :] = _rms(x2, gfin_ref[...]) if final_norm else x2


def _ffn(x, mod3, g_ffn, w_gate, w_up, w_down, g_final, final_norm):
    B, S, D = x.shape
    tm = TM_FFN
    tok = pl.BlockSpec((None, tm, D), lambda b, i: (b, i, 0))
    return pl.pallas_call(
        functools.partial(_ffn_kernel, final_norm=final_norm),
        out_shape=jax.ShapeDtypeStruct((B, S, D), F32),
        grid=(B, S // tm),
        in_specs=[
            tok,
            pl.BlockSpec((None, 1, N_MOD * D), lambda b, i: (b, 0, 0)),
            _const_spec(g_ffn.shape),
            _const_spec(w_gate.shape),
            _const_spec(w_up.shape),
            _const_spec(w_down.shape),
            _const_spec(g_final.shape),
        ],
        out_specs=tok,
        compiler_params=pltpu.CompilerParams(
            dimension_semantics=("parallel", "parallel"), vmem_limit_bytes=VMEM_LIMIT),
        name="swiglu_final_norm",
    )(x, mod3, g_ffn, w_gate, w_up, w_down, g_final)


def _prep_w_q(w_q_up):
    r = w_q_up.shape[0]
    w = w_q_up.reshape(r, N_HEADS, QK_HEAD)
    nope = w[..., :QK_NOPE]
    r1 = w[..., QK_NOPE:QK_NOPE + HALF_ROPE]
    r2 = w[..., QK_NOPE + HALF_ROPE:]
    return jnp.concatenate([nope, r1, r2, -r2, r1], axis=-1).reshape(r, N_HEADS * HEAD_PAD).T.astype(BF16)


def _prep_w_kv(w_kv_up):
    r = w_kv_up.shape[0]
    w = w_kv_up.reshape(r, N_HEADS, QK_NOPE + V_HEAD)
    k_nope = w[..., :QK_NOPE]
    v = w[..., QK_NOPE:]
    k_pad = jnp.concatenate(
        [k_nope, jnp.zeros((r, N_HEADS, HEAD_PAD - QK_NOPE), w.dtype)], axis=-1)
    wk = k_pad.reshape(r, N_HEADS * HEAD_PAD).astype(BF16)
    wvt = v.reshape(r, N_HEADS * V_HEAD).T.astype(BF16)
    return wk, wvt


def kernel(x, c, positions, w_ada, b_ada, g_norm_mix, w_in, g_q_a, w_q_up, g_kv_a, w_kv_up,
           w_attn_o, w_dw, b_dw, g_conv_ln, b_conv_ln, w_conv_out, w_out, g_norm_ffn,
           w_ffn_gate, w_ffn_up, w_ffn_down, g_final):
    B, S, D = x.shape
    depth = w_ada.shape[0]
    for l in range(depth):
        if l == 0:
            mod, cos_t, sin_t = _modulation(c, w_ada[l], b_ada[l][None, :], positions)
        else:
            mod = _modulation(c, w_ada[l], b_ada[l][None, :])
        mod3 = mod.reshape(B, 1, N_MOD * D)
        wk, wvt = _prep_w_kv(w_kv_up[l])
        w_in_t = w_in[l].T
        q, k, vt, z, h, w_co_b, w_ao_b, w_out_b = _inproj(
            x, mod3, g_norm_mix[l][None, :], w_in_t, g_q_a[l][None, :],
            _prep_w_q(w_q_up[l]), g_kv_a[l][None, :], wk, wvt, cos_t, sin_t,
            (w_conv_out[l], w_attn_o[l], w_out[l]))
        o = _attention(q, k, vt)
        x, w_gate_b, w_up_b, w_down_b = _mix(
            x, mod3, z, o, h, w_in_t, w_dw[l], b_dw[l][None, :], g_conv_ln[l][None, :],
            b_conv_ln[l][None, :], w_co_b, w_ao_b, w_out_b,
            (w_ffn_gate[l], w_ffn_up[l], w_ffn_down[l]))
        x = _ffn(x, mod3, g_norm_ffn[l][None, :], w_gate_b, w_up_b, w_down_b, g_final[None, :],
                 final_norm=(l == depth - 1))
    return x
```

```python
import functools
import math

import jax
import jax.numpy as jnp
from jax import lax
from jax.experimental import pallas as pl
from jax.experimental.pallas import tpu as pltpu

F32 = jnp.float32
BF16 = jnp.bfloat16

D_MODEL = 1024
N_HEADS = 8
Q_LORA = 256
KV_LORA = 128
QK_NOPE = 64
QK_ROPE = 32
HALF_ROPE = QK_ROPE // 2
V_HEAD = 64
QK_HEAD = QK_NOPE + QK_ROPE
ATTN_SCALE = 1.0 / math.sqrt(QK_HEAD)
Q_SCALE = ATTN_SCALE * math.log2(math.e)
ROPE_THETA = 10000.0
CONV_CH = 512
CONV_K = 31
N_MOD = 6
EPS_RMS = 1e-6
EPS_LN = 1e-5

LANES = 128
SUBLANES = 8
HEAD_PAD = LANES
VMEM_LIMIT = 56 * 1024 * 1024

C_QA = 0
C_KVA = C_QA + Q_LORA
C_KR = C_KVA + KV_LORA
C_CONV = C_KR + HEAD_PAD

TS_IN = 1024
TQ = 512
KEY_CHUNK = 1024
TS_MIX = 512
HALO = 16
TM_FFN = 512
ROW_SUBTILES = 2


def _sigmoid(x):
    return 1.0 / (1.0 + jnp.exp(-x))


def _rms(x, g):
    return x * lax.rsqrt(jnp.mean(x * x, axis=-1, keepdims=True) + EPS_RMS) * g


def _dot(a, b):
    return jnp.dot(a, b, preferred_element_type=F32)


def _const_spec(shape):
    nd = len(shape)
    return pl.BlockSpec(shape, lambda *_: (0,) * nd, pipeline_mode=pl.Buffered(1))


def _rope_tile_t(t, ctab_t, stab_t):
    swapped = jnp.concatenate([t[0:QK_NOPE], t[QK_HEAD:HEAD_PAD], t[QK_HEAD:HEAD_PAD]], axis=0)
    return t * ctab_t + swapped * stab_t


def _mod_kernel(c_ref, w_ref, b_ref, *rest, with_rope):
    if with_rope:
        pos_ref, freq_ref, o_ref, cos_ref, sin_ref = rest
        seq = pos_ref.shape[1]
        rows = cos_ref.shape[1] // seq
        for r in range(rows):
            pos = pos_ref[pl.ds(pl.program_id(0) * rows + r, 1), :].astype(F32)
            ang = pos * freq_ref[...]
            cos_ref[:, r * seq:(r + 1) * seq] = jnp.cos(ang)
            sin_ref[:, r * seq:(r + 1) * seq] = jnp.sin(ang)
    else:
        (o_ref,) = rest
    c = c_ref[...]
    c_act = (c * _sigmoid(c)).astype(BF16)
    mod = _dot(c_act, w_ref[...].astype(BF16)) + b_ref[...]
    for b in range(mod.shape[0]):
        o_ref[b] = mod[b:b + 1, :]


def _modulation(c, w_ada, b_ada, positions=None):
    B, D = c.shape
    n = w_ada.shape[1]
    bn = 1536
    steps = n // bn
    with_rope = positions is not None
    in_specs = [
        pl.BlockSpec((B, D), lambda j: (0, 0)),
        pl.BlockSpec((D, bn), lambda j: (0, j)),
        pl.BlockSpec((1, bn), lambda j: (0, j)),
    ]
    out_shape = [jax.ShapeDtypeStruct((B, 1, n), F32)]
    out_specs = [pl.BlockSpec((B, 1, bn), lambda j: (0, 0, j))]
    operands = [c, w_ada, b_ada]
    if with_rope:
        t = positions.size
        assert positions.shape[0] % steps == 0
        inv_freq = ROPE_THETA ** (-jnp.arange(0, QK_ROPE, 2, dtype=F32) / QK_ROPE)
        in_specs += [pl.BlockSpec(positions.shape, lambda j: (0, 0)),
                     pl.BlockSpec((HALF_ROPE, 1), lambda j: (0, 0))]
        table = pl.BlockSpec((HALF_ROPE, t // steps), lambda j: (0, j))
        out_shape += [jax.ShapeDtypeStruct((HALF_ROPE, t), F32)] * 2
        out_specs += [table, table]
        operands += [positions, inv_freq[:, None]]
    out = pl.pallas_call(
        functools.partial(_mod_kernel, with_rope=with_rope),
        out_shape=tuple(out_shape),
        grid=(steps,),
        in_specs=in_specs,
        out_specs=tuple(out_specs),
        compiler_params=pltpu.CompilerParams(
            dimension_semantics=("parallel",), vmem_limit_bytes=VMEM_LIMIT),
        name="adaln_modulation",
    )(*operands)
    return out if with_rope else out[0]


def _mixer_input(x, mod_ref, gmix_ref):
    D = D_MODEL
    shift = mod_ref[:, 0:D]
    scale = mod_ref[:, D:2 * D]
    return (_rms(x, gmix_ref[...]) * (1.0 + scale) + shift).astype(BF16)


def _first_grid_step():
    return jnp.logical_and(pl.program_id(0) == 0, pl.program_id(1) == 0)


def _transpose_rows(wt_ref, row0, n_rows, out_ref, col0):
    for j in range(n_rows // LANES):
        rows = wt_ref[row0 + j * LANES:row0 + (j + 1) * LANES, :]
        out_ref[:, col0 + j * LANES:col0 + (j + 1) * LANES] = rows.T.astype(BF16)


def _relayout_w_proj(wt_ref, out_ref):
    o_kr = Q_LORA + KV_LORA
    o_conv = o_kr + QK_ROPE
    _transpose_rows(wt_ref, 0, o_kr, out_ref, C_QA)
    k1 = wt_ref[o_kr:o_kr + HALF_ROPE, :]
    k2 = wt_ref[o_kr + HALF_ROPE:o_conv, :]
    key_tile_t = jnp.concatenate([jnp.zeros((QK_NOPE, k1.shape[1]), F32), k1, k2, -k2, k1], axis=0)
    out_ref[:, C_KR:C_KR + HEAD_PAD] = key_tile_t.T.astype(BF16)
    _transpose_rows(wt_ref, o_conv, 2 * CONV_CH, out_ref, C_CONV)


def _inproj_kernel(x_ref, mod_ref, gmix_ref, w_ref, gq_ref, wq_ref, gkv_ref, wk_ref, wvt_ref,
                   cost_ref, sint_ref, *rest, n_cast):
    cast_in = rest[:n_cast]
    q_ref, k_ref, vt_ref, z_ref, h_ref = rest[n_cast:n_cast + 5]
    cast_out = rest[n_cast + 5:2 * n_cast + 5]
    (win_ref,) = rest[2 * n_cast + 5:]
    for src, dst in zip(cast_in, cast_out):
        dst[...] = src[...].astype(BF16)
    pl.when(_first_grid_step())(lambda: _relayout_w_proj(w_ref, win_ref))
    h = _mixer_input(x_ref[...], mod_ref, gmix_ref)
    h_ref[...] = h
    cos_t = cost_ref[...]
    sin_t = sint_ref[...]
    n_pos = cos_t.shape[1]
    ctab_t = jnp.concatenate([jnp.ones((QK_NOPE, n_pos), F32), cos_t, cos_t,
                              jnp.zeros((HEAD_PAD - QK_HEAD, n_pos), F32)], axis=0)
    stab_t = jnp.concatenate([jnp.zeros((QK_NOPE, n_pos), F32), sin_t, sin_t,
                              jnp.zeros((HEAD_PAD - QK_HEAD, n_pos), F32)], axis=0)

    proj = _dot(h, win_ref[...])

    z_ref[...] = (proj[:, C_CONV:C_CONV + CONV_CH]
                  * _sigmoid(proj[:, C_CONV + CONV_CH:C_CONV + 2 * CONV_CH]))

    qn = _rms(proj[:, C_QA:C_QA + Q_LORA], gq_ref[...]).astype(BF16)
    q_all_t = lax.dot_general(wq_ref[...], qn, (((1,), (1,)), ((), ())), preferred_element_type=F32)
    for hd in range(N_HEADS):
        sl = slice(hd * HEAD_PAD, (hd + 1) * HEAD_PAD)
        q_ref[sl, :] = (_rope_tile_t(q_all_t[sl, :], ctab_t, stab_t) * Q_SCALE).astype(BF16)

    kvn = _rms(proj[:, C_KVA:C_KVA + KV_LORA], gkv_ref[...]).astype(BF16)
    k_pad = _dot(kvn, wk_ref[...])
    vt_ref[...] = lax.dot_general(wvt_ref[...], kvn, (((1,), (1,)), ((), ())),
                                  preferred_element_type=F32).astype(BF16)
    kr_rot = _rope_tile_t(proj[:, C_KR:C_KR + HEAD_PAD].T, ctab_t, stab_t).T
    for hd in range(N_HEADS):
        sl = slice(hd * HEAD_PAD, (hd + 1) * HEAD_PAD)
        k_ref[:, sl] = (k_pad[:, sl] + kr_rot).astype(BF16)


def _inproj(x, mod3, g_mix, w_in, g_q, wq_t, g_kv, wk, wvt, cos_t, sin_t, cast_ws):
    B, S, D = x.shape
    ts = TS_IN
    nq = N_HEADS * HEAD_PAD
    nv = N_HEADS * V_HEAD
    tok = lambda w: pl.BlockSpec((None, ts, w), lambda b, i: (b, i, 0))
    n_i = S // ts
    table = pl.BlockSpec((HALF_ROPE, ts), lambda b, i: (0, b * n_i + i))
    cast_specs = [_cast_chunk_spec(w.shape[0], w.shape[1], n_i, B * n_i) for w in cast_ws]
    return pl.pallas_call(
        functools.partial(_inproj_kernel, n_cast=len(cast_ws)),
        out_shape=(
            jax.ShapeDtypeStruct((B, nq, S), BF16),
            jax.ShapeDtypeStruct((B, S, nq), BF16),
            jax.ShapeDtypeStruct((B, nv, S), BF16),
            jax.ShapeDtypeStruct((B, S, CONV_CH), F32),
            jax.ShapeDtypeStruct((B, S, D), BF16),
            *[jax.ShapeDtypeStruct(w.shape, BF16) for w in cast_ws],
        ),
        grid=(B, S // ts),
        in_specs=[
            tok(D),
            pl.BlockSpec((None, 1, N_MOD * D), lambda b, i: (b, 0, 0)),
            _const_spec(g_mix.shape),
            _const_spec(w_in.shape),
            _const_spec(g_q.shape),
            _const_spec(wq_t.shape),
            _const_spec(g_kv.shape),
            _const_spec(wk.shape),
            _const_spec(wvt.shape),
            table,
            table,
            *cast_specs,
        ],
        out_specs=(pl.BlockSpec((None, nq, ts), lambda b, i: (b, 0, i)), tok(nq),
                   pl.BlockSpec((None, nv, ts), lambda b, i: (b, 0, i)),
                   tok(CONV_CH), tok(D), *cast_specs),
        scratch_shapes=[pltpu.VMEM((D, C_CONV + 2 * CONV_CH), BF16)],
        compiler_params=pltpu.CompilerParams(
            dimension_semantics=("arbitrary", "arbitrary"), vmem_limit_bytes=VMEM_LIMIT),
        name="input_projection",
    )(x, mod3, g_mix, w_in, g_q, wq_t, g_kv, wk, wvt, cos_t, sin_t, *cast_ws)


def _attn_kernel(qt_ref, k_ref, vt_ref, o_ref):
    n_kc = k_ref.shape[0] // KEY_CHUNK

    def score_chunk(hd, c):
        sl = slice(hd * HEAD_PAD, (hd + 1) * HEAD_PAD)
        ks = slice(c * KEY_CHUNK, (c + 1) * KEY_CHUNK)
        return _dot(k_ref[ks, sl], qt_ref[sl, :])

    def col_max(chunks):
        m = jnp.max(chunks[0], axis=0, keepdims=True)
        for st in chunks[1:]:
            m = jnp.maximum(m, jnp.max(st, axis=0, keepdims=True))
        return m

    st_next = [score_chunk(0, c) for c in range(n_kc)]
    outs = []
    for hd in range(N_HEADS):
        st_cur, st_next = st_next, []
        m = col_max(st_cur)
        l = acc = None
        for c in range(n_kc):
            if hd + 1 < N_HEADS:
                st_next.append(score_chunk(hd + 1, c))
            ks = slice(c * KEY_CHUNK, (c + 1) * KEY_CHUNK)
            p = jnp.exp2(st_cur[c] - m)
            ls = jnp.sum(p, axis=0, keepdims=True)
            pv = _dot(vt_ref[hd * V_HEAD:(hd + 1) * V_HEAD, ks], p.astype(BF16))
            l, acc = (ls, pv) if c == 0 else (l + ls, acc + pv)
        outs.append(acc / l)
        if hd % 2 == 1:
            o_pair = jnp.concatenate(outs, axis=0).T
            o_ref[:, (hd // 2) * 2 * V_HEAD:(hd // 2 + 1) * 2 * V_HEAD] = o_pair.astype(BF16)
            outs = []


def _attention(qt, k, vt):
    B, nq, S = qt.shape
    nv = N_HEADS * V_HEAD
    return pl.pallas_call(
        _attn_kernel,
        out_shape=jax.ShapeDtypeStruct((B, S, nv), BF16),
        grid=(B, S // TQ),
        in_specs=[
            pl.BlockSpec((None, nq, TQ), lambda b, i: (b, 0, i)),
            pl.BlockSpec((None, S, nq), lambda b, i: (b, 0, 0)),
            pl.BlockSpec((None, nv, S), lambda b, i: (b, 0, 0)),
        ],
        out_specs=pl.BlockSpec((None, TQ, nv), lambda b, i: (b, i, 0)),
        compiler_params=pltpu.CompilerParams(
            dimension_semantics=("parallel", "parallel"), vmem_limit_bytes=VMEM_LIMIT),
        name="mla_attention",
    )(qt, k, vt)


def _relayout_w_gate(wt_ref, out_ref):
    o_gate = Q_LORA + KV_LORA + QK_ROPE + 2 * CONV_CH
    _transpose_rows(wt_ref, o_gate, out_ref.shape[1], out_ref, 0)


def _mix_kernel(x_ref, mod_ref, z_ref, zprev_ref, znext_ref, o_ref, h_ref, w_ref,
                wdw_ref, bdw_ref, gln_ref, bln_ref, wco_ref, wao_ref, wout_ref, *rest, n_cast):
    cast_in = rest[:n_cast]
    out_ref = rest[n_cast]
    cast_out = rest[n_cast + 1:2 * n_cast + 1]
    zp_ref, zs_ref, conv_ref, wgate_ref = rest[2 * n_cast + 1:]
    for src, dst in zip(cast_in, cast_out):
        dst[...] = src[...].astype(BF16)
    D = D_MODEL
    ts = TS_MIX
    i = pl.program_id(1)
    n_i = pl.num_programs(1)
    pl.when(_first_grid_step())(lambda: _relayout_w_gate(w_ref, wgate_ref))

    zp_ref[0:HALO, :] = jnp.where(i > 0, zprev_ref[...], 0.0)
    zp_ref[HALO:HALO + ts, :] = z_ref[...]
    zp_ref[HALO + ts:2 * HALO + ts, :] = jnp.where(i < n_i - 1, znext_ref[...], 0.0)

    n_shift = zs_ref.shape[1]
    for s in range(1, SUBLANES):
        zs_ref[s - 1] = zp_ref[s:s + n_shift, :]

    gate_logits = _dot(h_ref[...], wgate_ref[...])
    y_a = _dot(o_ref[...], wao_ref[...])

    row_chunk = 64
    base = HALO - CONV_K // 2
    for cb in range(CONV_CH // LANES):
        cs = slice(cb * LANES, (cb + 1) * LANES)
        for rb in range(ts // row_chunk):
            r0 = rb * row_chunk
            acc = jnp.broadcast_to(bdw_ref[:, cs], (row_chunk, LANES))
            for kk in range(CONV_K):
                s = (base + kk) % SUBLANES
                a = r0 + base + kk - s
                src = zp_ref if s == 0 else zs_ref.at[s - 1]
                acc = acc + wdw_ref[kk:kk + 1, cs] * src[a:a + row_chunk, cs]
            conv_ref[r0:r0 + row_chunk, cs] = acc

    zc = conv_ref[...]
    mu = jnp.mean(zc, axis=-1, keepdims=True)
    zd = zc - mu
    var = jnp.mean(zd * zd, axis=-1, keepdims=True)
    zn = zd * lax.rsqrt(var + EPS_LN) * gln_ref[...] + bln_ref[...]
    zs = (zn * _sigmoid(zn)).astype(BF16)
    y_b = _dot(zs, wco_ref[...])
    gates = _sigmoid(gate_logits)
    merged = (gates[:, 0:D] * y_a + gates[:, D:2 * D] * y_b).astype(BF16)
    gate_m = mod_ref[:, 2 * D:3 * D]
    out_ref[...] = x_ref[...] + gate_m * _dot(merged, wout_ref[...])


def _cast_chunk_spec(rows, cols, n_i, n_steps):
    every = 1
    while (rows * every) % n_steps or (rows * every // n_steps) % (2 * SUBLANES):
        every *= 2
    chunk = rows * every // n_steps
    return pl.BlockSpec((chunk, cols), lambda b, i: ((b * n_i + i) // every, 0))


def _mix(x, mod3, z, o, h, w_in, w_dw, b_dw, g_ln, b_ln, w_co, w_ao, w_out, cast_ws):
    B, S, D = x.shape
    ts = TS_MIX
    hb = ts // HALO
    n_halo = S // HALO
    n_i = S // ts
    tok = lambda w: pl.BlockSpec((None, ts, w), lambda b, i: (b, i, 0))
    cast_specs = [_cast_chunk_spec(w.shape[0], w.shape[1], n_i, B * n_i) for w in cast_ws]
    return pl.pallas_call(
        functools.partial(_mix_kernel, n_cast=len(cast_ws)),
        out_shape=(jax.ShapeDtypeStruct((B, S, D), F32),
                   *[jax.ShapeDtypeStruct(w.shape, BF16) for w in cast_ws]),
        grid=(B, S // ts),
        in_specs=[
            tok(D),
            pl.BlockSpec((None, 1, N_MOD * D), lambda b, i: (b, 0, 0)),
            tok(CONV_CH),
            pl.BlockSpec((None, HALO, CONV_CH), lambda b, i: (b, jnp.maximum(i * hb - 1, 0), 0)),
            pl.BlockSpec((None, HALO, CONV_CH),
                         lambda b, i: (b, jnp.minimum((i + 1) * hb, n_halo - 1), 0)),
            tok(N_HEADS * V_HEAD),
            tok(D),
            _const_spec(w_in.shape),
            _const_spec(w_dw.shape),
            _const_spec(b_dw.shape),
            _const_spec(g_ln.shape),
            _const_spec(b_ln.shape),
            _const_spec(w_co.shape),
            _const_spec(w_ao.shape),
            _const_spec(w_out.shape),
            *cast_specs,
        ],
        out_specs=(tok(D), *cast_specs),
        scratch_shapes=[
            pltpu.VMEM((ts + 2 * HALO, CONV_CH), F32),
            pltpu.VMEM((SUBLANES - 1, ts + 2 * HALO - SUBLANES, CONV_CH), F32),
            pltpu.VMEM((ts, CONV_CH), F32),
            pltpu.VMEM((D, 2 * D), BF16),
        ],
        compiler_params=pltpu.CompilerParams(
            dimension_semantics=("arbitrary", "arbitrary"), vmem_limit_bytes=VMEM_LIMIT),
        name="conv_merge_out",
    )(x, mod3, z, z, z, o, h, w_in, w_dw, b_dw, g_ln, b_ln, w_co, w_ao, w_out, *cast_ws)


def _ffn_kernel(x_ref, mod_ref, gffn_ref, wg_ref, wu_ref, wd_ref, gfin_ref, out_ref, *, final_norm):
    D = D_MODEL
    shift = mod_ref[:, 3 * D:4 * D]
    scale = mod_ref[:, 4 * D:5 * D]
    gate = mod_ref[:, 5 * D:6 * D]
    rows = x_ref.shape[0] // ROW_SUBTILES
    for j in range(ROW_SUBTILES):
        rs = slice(j * rows, (j + 1) * rows)
        x = x_ref[rs, :]
        h = (_rms(x, gffn_ref[...]) * (1.0 + scale) + shift).astype(BF16)
        g = _dot(h, wg_ref[...])
        u = _dot(h, wu_ref[...])
        a = (g * _sigmoid(g) * u).astype(BF16)
        x2 = x + gate * _dot(a, wd_ref[...])
        out_ref[rs, :] = _rms(x2, gfin_ref[...]) if final_norm else x2


def _ffn(x, mod3, g_ffn, w_gate, w_up, w_down, g_final, final_norm):
    B, S, D = x.shape
    tm = TM_FFN
    tok = pl.BlockSpec((None, tm, D), lambda b, i: (b, i, 0))
    return pl.pallas_call(
        functools.partial(_ffn_kernel, final_norm=final_norm),
        out_shape=jax.ShapeDtypeStruct((B, S, D), F32),
        grid=(B, S // tm),
        in_specs=[
            tok,
            pl.BlockSpec((None, 1, N_MOD * D), lambda b, i: (b, 0, 0)),
            _const_spec(g_ffn.shape),
            _const_spec(w_gate.shape),
            _const_spec(w_up.shape),
            _const_spec(w_down.shape),
            _const_spec(g_final.shape),
        ],
        out_specs=tok,
        compiler_params=pltpu.CompilerParams(
            dimension_semantics=("parallel", "parallel"), vmem_limit_bytes=VMEM_LIMIT),
        name="swiglu_final_norm",
    )(x, mod3, g_ffn, w_gate, w_up, w_down, g_final)


def _prep_w_q(w_q_up):
    r = w_q_up.shape[0]
    w = w_q_up.reshape(r, N_HEADS, QK_HEAD)
    nope = w[..., :QK_NOPE]
    r1 = w[..., QK_NOPE:QK_NOPE + HALF_ROPE]
    r2 = w[..., QK_NOPE + HALF_ROPE:]
    return jnp.concatenate([nope, r1, r2, -r2, r1], axis=-1).reshape(r, N_HEADS * HEAD_PAD).T.astype(BF16)


def _prep_w_kv(w_kv_up):
    r = w_kv_up.shape[0]
    w = w_kv_up.reshape(r, N_HEADS, QK_NOPE + V_HEAD)
    k_nope = w[..., :QK_NOPE]
    v = w[..., QK_NOPE:]
    k_pad = jnp.concatenate(
        [k_nope, jnp.zeros((r, N_HEADS, HEAD_PAD - QK_NOPE), w.dtype)], axis=-1)
    wk = k_pad.reshape(r, N_HEADS * HEAD_PAD).astype(BF16)
    wvt = v.reshape(r, N_HEADS * V_HEAD).T.astype(BF16)
    return wk, wvt


def kernel(x, c, positions, w_ada, b_ada, g_norm_mix, w_in, g_q_a, w_q_up, g_kv_a, w_kv_up,
           w_attn_o, w_dw, b_dw, g_conv_ln, b_conv_ln, w_conv_out, w_out, g_norm_ffn,
           w_ffn_gate, w_ffn_up, w_ffn_down, g_final):
    B, S, D = x.shape
    depth = w_ada.shape[0]
    for l in range(depth):
        if l == 0:
            mod3, cos_t, sin_t = _modulation(c, w_ada[l], b_ada[l][None, :], positions)
        else:
            mod3 = _modulation(c, w_ada[l], b_ada[l][None, :])
        wk, wvt = _prep_w_kv(w_kv_up[l])
        w_in_t = w_in[l].T
        q, k, vt, z, h, w_co_b, w_ao_b, w_out_b = _inproj(
            x, mod3, g_norm_mix[l][None, :], w_in_t, g_q_a[l][None, :],
            _prep_w_q(w_q_up[l]), g_kv_a[l][None, :], wk, wvt, cos_t, sin_t,
            (w_conv_out[l], w_attn_o[l], w_out[l]))
        o = _attention(q, k, vt)
        x, w_gate_b, w_up_b, w_down_b = _mix(
            x, mod3, z, o, h, w_in_t, w_dw[l], b_dw[l][None, :], g_conv_ln[l][None, :],
            b_conv_ln[l][None, :], w_co_b, w_ao_b, w_out_b,
            (w_ffn_gate[l], w_ffn_up[l], w_ffn_down[l]))
        x = _ffn(x, mod3, g_norm_ffn[l][None, :], w_gate_b, w_up_b, w_down_b, g_final[None, :],
                 final_norm=(l == depth - 1))
    return x
```

```python
import functools
import math

import jax
import jax.numpy as jnp
from jax import lax
from jax.experimental import pallas as pl
from jax.experimental.pallas import tpu as pltpu

F32 = jnp.float32
BF16 = jnp.bfloat16

D_MODEL = 1024
N_HEADS = 8
Q_LORA = 256
KV_LORA = 128
QK_NOPE = 64
QK_ROPE = 32
HALF_ROPE = QK_ROPE // 2
V_HEAD = 64
QK_HEAD = QK_NOPE + QK_ROPE
ATTN_SCALE = 1.0 / math.sqrt(QK_HEAD)
Q_SCALE = ATTN_SCALE * math.log2(math.e)
ROPE_THETA = 10000.0
CONV_CH = 512
CONV_K = 31
N_MOD = 6
EPS_RMS = 1e-6
EPS_LN = 1e-5

LANES = 128
SUBLANES = 8
HEAD_PAD = LANES
VMEM_LIMIT = 56 * 1024 * 1024

C_QA = 0
C_KVA = C_QA + Q_LORA
C_KR = C_KVA + KV_LORA
C_CONV = C_KR + HEAD_PAD

TS_IN = 1024
TQ = 512
KEY_CHUNK = 1024
TS_MIX = 512
HALO = 16
TM_FFN = 512
ROW_SUBTILES = 2


def _sigmoid(x):
    return 1.0 / (1.0 + jnp.exp(-x))


def _rms(x, g):
    return x * lax.rsqrt(jnp.mean(x * x, axis=-1, keepdims=True) + EPS_RMS) * g


def _dot(a, b):
    return jnp.dot(a, b, preferred_element_type=F32)


def _const_spec(shape):
    nd = len(shape)
    return pl.BlockSpec(shape, lambda *_: (0,) * nd, pipeline_mode=pl.Buffered(1))


def _rope_tile_t(t, ctab_t, stab_t):
    swapped = jnp.concatenate([t[0:QK_NOPE], t[QK_HEAD:HEAD_PAD], t[QK_HEAD:HEAD_PAD]], axis=0)
    return t * ctab_t + swapped * stab_t


def _mod_kernel(c_ref, w_ref, b_ref, *rest, with_rope):
    if with_rope:
        pos_ref, freq_ref, o_ref, cos_ref, sin_ref = rest
        seq = pos_ref.shape[1]
        rows = cos_ref.shape[1] // seq
        for r in range(rows):
            pos = pos_ref[pl.ds(pl.program_id(0) * rows + r, 1), :].astype(F32)
            ang = pos * freq_ref[...]
            cos_ref[:, r * seq:(r + 1) * seq] = jnp.cos(ang)
            sin_ref[:, r * seq:(r + 1) * seq] = jnp.sin(ang)
    else:
        (o_ref,) = rest
    c = c_ref[...]
    c_act = (c * _sigmoid(c)).astype(BF16)
    mod = _dot(c_act, w_ref[...].astype(BF16)) + b_ref[...]
    for b in range(mod.shape[0]):
        o_ref[b] = mod[b:b + 1, :]


def _modulation(c, w_ada, b_ada, positions=None):
    B, D = c.shape
    n = w_ada.shape[1]
    bn = 1536
    steps = n // bn
    with_rope = positions is not None
    in_specs = [
        pl.BlockSpec((B, D), lambda j: (0, 0)),
        pl.BlockSpec((D, bn), lambda j: (0, j)),
        pl.BlockSpec((1, bn), lambda j: (0, j)),
    ]
    out_shape = [jax.ShapeDtypeStruct((B, 1, n), F32)]
    out_specs = [pl.BlockSpec((B, 1, bn), lambda j: (0, 0, j))]
    operands = [c, w_ada, b_ada]
    if with_rope:
        t = positions.size
        assert positions.shape[0] % steps == 0
        inv_freq = ROPE_THETA ** (-jnp.arange(0, QK_ROPE, 2, dtype=F32) / QK_ROPE)
        in_specs += [pl.BlockSpec(positions.shape, lambda j: (0, 0)),
                     pl.BlockSpec((HALF_ROPE, 1), lambda j: (0, 0))]
        table = pl.BlockSpec((HALF_ROPE, t // steps), lambda j: (0, j))
        out_shape += [jax.ShapeDtypeStruct((HALF_ROPE, t), F32)] * 2
        out_specs += [table, table]
        operands += [positions, inv_freq[:, None]]
    out = pl.pallas_call(
        functools.partial(_mod_kernel, with_rope=with_rope),
        out_shape=tuple(out_shape),
        grid=(steps,),
        in_specs=in_specs,
        out_specs=tuple(out_specs),
        compiler_params=pltpu.CompilerParams(
            dimension_semantics=("parallel",), vmem_limit_bytes=VMEM_LIMIT),
        name="adaln_modulation",
    )(*operands)
    return out if with_rope else out[0]


def _mixer_input(x, mod_ref, gmix_ref):
    D = D_MODEL
    shift = mod_ref[:, 0:D]
    scale = mod_ref[:, D:2 * D]
    return (_rms(x, gmix_ref[...]) * (1.0 + scale) + shift).astype(BF16)


def _first_grid_step():
    return jnp.logical_and(pl.program_id(0) == 0, pl.program_id(1) == 0)


def _transpose_rows(wt_ref, row0, n_rows, out_ref, col0):
    for j in range(n_rows // LANES):
        rows = wt_ref[row0 + j * LANES:row0 + (j + 1) * LANES, :]
        out_ref[:, col0 + j * LANES:col0 + (j + 1) * LANES] = rows.T.astype(BF16)


def _relayout_w_proj(wt_ref, out_ref):
    o_kr = Q_LORA + KV_LORA
    o_conv = o_kr + QK_ROPE
    _transpose_rows(wt_ref, 0, o_kr, out_ref, C_QA)
    k1 = wt_ref[o_kr:o_kr + HALF_ROPE, :]
    k2 = wt_ref[o_kr + HALF_ROPE:o_conv, :]
    key_tile_t = jnp.concatenate([jnp.zeros((QK_NOPE, k1.shape[1]), F32), k1, k2, -k2, k1], axis=0)
    out_ref[:, C_KR:C_KR + HEAD_PAD] = key_tile_t.T.astype(BF16)
    _transpose_rows(wt_ref, o_conv, 2 * CONV_CH, out_ref, C_CONV)


def _relayout_w_up(wqu_ref, wkvu_ref, wq_ref, wk_ref, wvt_ref):
    wq_t = wqu_ref[...].T
    for hd in range(N_HEADS):
        r0 = hd * QK_HEAD
        r1 = wq_t[r0 + QK_NOPE:r0 + QK_NOPE + HALF_ROPE]
        r2 = wq_t[r0 + QK_NOPE + HALF_ROPE:r0 + QK_HEAD]
        tile = jnp.concatenate([wq_t[r0:r0 + QK_HEAD], -r2, r1], axis=0)
        wq_ref[hd * HEAD_PAD:(hd + 1) * HEAD_PAD, :] = tile.astype(BF16)
    assert QK_NOPE + V_HEAD == HEAD_PAD
    wkv = wkvu_ref[...]
    lane = lax.broadcasted_iota(jnp.int32, wkv.shape, 1)
    wk_ref[...] = jnp.where(lane % HEAD_PAD < QK_NOPE, wkv, 0.0).astype(BF16)
    for hd in range(N_HEADS):
        head_t = wkv[:, hd * HEAD_PAD:(hd + 1) * HEAD_PAD].T
        wvt_ref[hd * V_HEAD:(hd + 1) * V_HEAD, :] = head_t[QK_NOPE:, :].astype(BF16)


def _inproj_kernel(x_ref, mod_ref, gmix_ref, w_ref, gq_ref, wqu_ref, gkv_ref, wkvu_ref,
                   cost_ref, sint_ref, *rest, n_cast):
    cast_in = rest[:n_cast]
    q_ref, k_ref, vt_ref, z_ref, h_ref = rest[n_cast:n_cast + 5]
    cast_out = rest[n_cast + 5:2 * n_cast + 5]
    win_ref, wq_ref, wk_ref, wvt_ref = rest[2 * n_cast + 5:]
    for src, dst in zip(cast_in, cast_out):
        dst[...] = src[...].astype(BF16)

    @pl.when(_first_grid_step())
    def _():
        _relayout_w_proj(w_ref, win_ref)
        _relayout_w_up(wqu_ref, wkvu_ref, wq_ref, wk_ref, wvt_ref)

    h = _mixer_input(x_ref[...], mod_ref, gmix_ref)
    h_ref[...] = h
    cos_t = cost_ref[...]
    sin_t = sint_ref[...]
    n_pos = cos_t.shape[1]
    ctab_t = jnp.concatenate([jnp.ones((QK_NOPE, n_pos), F32), cos_t, cos_t,
                              jnp.zeros((HEAD_PAD - QK_HEAD, n_pos), F32)], axis=0)
    stab_t = jnp.concatenate([jnp.zeros((QK_NOPE, n_pos), F32), sin_t, sin_t,
                              jnp.zeros((HEAD_PAD - QK_HEAD, n_pos), F32)], axis=0)

    proj = _dot(h, win_ref[...])

    z_ref[...] = (proj[:, C_CONV:C_CONV + CONV_CH]
                  * _sigmoid(proj[:, C_CONV + CONV_CH:C_CONV + 2 * CONV_CH]))

    qn = _rms(proj[:, C_QA:C_QA + Q_LORA], gq_ref[...]).astype(BF16)
    q_all_t = lax.dot_general(wq_ref[...], qn, (((1,), (1,)), ((), ())), preferred_element_type=F32)
    for hd in range(N_HEADS):
        sl = slice(hd * HEAD_PAD, (hd + 1) * HEAD_PAD)
        q_ref[sl, :] = (_rope_tile_t(q_all_t[sl, :], ctab_t, stab_t) * Q_SCALE).astype(BF16)

    kvn = _rms(proj[:, C_KVA:C_KVA + KV_LORA], gkv_ref[...]).astype(BF16)
    k_pad = _dot(kvn, wk_ref[...])
    vt_ref[...] = lax.dot_general(wvt_ref[...], kvn, (((1,), (1,)), ((), ())),
                                  preferred_element_type=F32).astype(BF16)
    kr_rot = _rope_tile_t(proj[:, C_KR:C_KR + HEAD_PAD].T, ctab_t, stab_t).T
    for hd in range(N_HEADS):
        sl = slice(hd * HEAD_PAD, (hd + 1) * HEAD_PAD)
        k_ref[:, sl] = (k_pad[:, sl] + kr_rot).astype(BF16)


def _inproj(x, mod3, g_mix, w_in, g_q, w_q_up, g_kv, w_kv_up, cos_t, sin_t, cast_ws):
    B, S, D = x.shape
    ts = TS_IN
    nq = N_HEADS * HEAD_PAD
    nv = N_HEADS * V_HEAD
    tok = lambda w: pl.BlockSpec((None, ts, w), lambda b, i: (b, i, 0))
    n_i = S // ts
    table = pl.BlockSpec((HALF_ROPE, ts), lambda b, i: (0, b * n_i + i))
    cast_specs = [_cast_chunk_spec(w.shape[0], w.shape[1], n_i, B * n_i) for w in cast_ws]
    return pl.pallas_call(
        functools.partial(_inproj_kernel, n_cast=len(cast_ws)),
        out_shape=(
            jax.ShapeDtypeStruct((B, nq, S), BF16),
            jax.ShapeDtypeStruct((B, S, nq), BF16),
            jax.ShapeDtypeStruct((B, nv, S), BF16),
            jax.ShapeDtypeStruct((B, S, CONV_CH), F32),
            jax.ShapeDtypeStruct((B, S, D), BF16),
            *[jax.ShapeDtypeStruct(w.shape, BF16) for w in cast_ws],
        ),
        grid=(B, S // ts),
        in_specs=[
            tok(D),
            pl.BlockSpec((None, 1, N_MOD * D), lambda b, i: (b, 0, 0)),
            _const_spec(g_mix.shape),
            _const_spec(w_in.shape),
            _const_spec(g_q.shape),
            _const_spec(w_q_up.shape),
            _const_spec(g_kv.shape),
            _const_spec(w_kv_up.shape),
            table,
            table,
            *cast_specs,
        ],
        out_specs=(pl.BlockSpec((None, nq, ts), lambda b, i: (b, 0, i)), tok(nq),
                   pl.BlockSpec((None, nv, ts), lambda b, i: (b, 0, i)),
                   tok(CONV_CH), tok(D), *cast_specs),
        scratch_shapes=[pltpu.VMEM((D, C_CONV + 2 * CONV_CH), BF16),
                        pltpu.VMEM((nq, w_q_up.shape[0]), BF16),
                        pltpu.VMEM((w_kv_up.shape[0], nq), BF16),
                        pltpu.VMEM((nv, w_kv_up.shape[0]), BF16)],
        compiler_params=pltpu.CompilerParams(
            dimension_semantics=("arbitrary", "arbitrary"), vmem_limit_bytes=VMEM_LIMIT),
        name="input_projection",
    )(x, mod3, g_mix, w_in, g_q, w_q_up, g_kv, w_kv_up, cos_t, sin_t, *cast_ws)


def _attn_kernel(qt_ref, k_ref, vt_ref, o_ref):
    n_kc = k_ref.shape[0] // KEY_CHUNK

    def score_chunk(hd, c):
        sl = slice(hd * HEAD_PAD, (hd + 1) * HEAD_PAD)
        ks = slice(c * KEY_CHUNK, (c + 1) * KEY_CHUNK)
        return _dot(k_ref[ks, sl], qt_ref[sl, :])

    def col_max(chunks):
        m = jnp.max(chunks[0], axis=0, keepdims=True)
        for st in chunks[1:]:
            m = jnp.maximum(m, jnp.max(st, axis=0, keepdims=True))
        return m

    st_next = [score_chunk(0, c) for c in range(n_kc)]
    outs = []
    for hd in range(N_HEADS):
        st_cur, st_next = st_next, []
        m = col_max(st_cur)
        l = acc = None
        for c in range(n_kc):
            if hd + 1 < N_HEADS:
                st_next.append(score_chunk(hd + 1, c))
            ks = slice(c * KEY_CHUNK, (c + 1) * KEY_CHUNK)
            p = jnp.exp2(st_cur[c] - m)
            ls = jnp.sum(p, axis=0, keepdims=True)
            pv = _dot(vt_ref[hd * V_HEAD:(hd + 1) * V_HEAD, ks], p.astype(BF16))
            l, acc = (ls, pv) if c == 0 else (l + ls, acc + pv)
        outs.append(acc / l)
        if hd % 2 == 1:
            o_pair = jnp.concatenate(outs, axis=0).T
            o_ref[:, (hd // 2) * 2 * V_HEAD:(hd // 2 + 1) * 2 * V_HEAD] = o_pair.astype(BF16)
            outs = []


def _attention(qt, k, vt):
    B, nq, S = qt.shape
    nv = N_HEADS * V_HEAD
    return pl.pallas_call(
        _attn_kernel,
        out_shape=jax.ShapeDtypeStruct((B, S, nv), BF16),
        grid=(B, S // TQ),
        in_specs=[
            pl.BlockSpec((None, nq, TQ), lambda b, i: (b, 0, i)),
            pl.BlockSpec((None, S, nq), lambda b, i: (b, 0, 0)),
            pl.BlockSpec((None, nv, S), lambda b, i: (b, 0, 0)),
        ],
        out_specs=pl.BlockSpec((None, TQ, nv), lambda b, i: (b, i, 0)),
        compiler_params=pltpu.CompilerParams(
            dimension_semantics=("parallel", "parallel"), vmem_limit_bytes=VMEM_LIMIT),
        name="mla_attention",
    )(qt, k, vt)


def _relayout_w_gate(wt_ref, out_ref):
    o_gate = Q_LORA + KV_LORA + QK_ROPE + 2 * CONV_CH
    _transpose_rows(wt_ref, o_gate, out_ref.shape[1], out_ref, 0)


def _mix_kernel(x_ref, mod_ref, z_ref, zprev_ref, znext_ref, o_ref, h_ref, w_ref,
                wdw_ref, bdw_ref, gln_ref, bln_ref, wco_ref, wao_ref, wout_ref, *rest, n_cast):
    cast_in = rest[:n_cast]
    out_ref = rest[n_cast]
    cast_out = rest[n_cast + 1:2 * n_cast + 1]
    zp_ref, zs_ref, conv_ref, wgate_ref = rest[2 * n_cast + 1:]
    for src, dst in zip(cast_in, cast_out):
        dst[...] = src[...].astype(BF16)
    D = D_MODEL
    ts = TS_MIX
    i = pl.program_id(1)
    n_i = pl.num_programs(1)
    pl.when(_first_grid_step())(lambda: _relayout_w_gate(w_ref, wgate_ref))

    zp_ref[0:HALO, :] = jnp.where(i > 0, zprev_ref[...], 0.0)
    zp_ref[HALO:HALO + ts, :] = z_ref[...]
    zp_ref[HALO + ts:2 * HALO + ts, :] = jnp.where(i < n_i - 1, znext_ref[...], 0.0)

    n_shift = zs_ref.shape[1]
    for s in range(1, SUBLANES):
        zs_ref[s - 1] = zp_ref[s:s + n_shift, :]

    gate_logits = _dot(h_ref[...], wgate_ref[...])
    y_a = _dot(o_ref[...], wao_ref[...])

    row_chunk = 64
    base = HALO - CONV_K // 2
    for cb in range(CONV_CH // LANES):
        cs = slice(cb * LANES, (cb + 1) * LANES)
        for rb in range(ts // row_chunk):
            r0 = rb * row_chunk
            acc = jnp.broadcast_to(bdw_ref[:, cs], (row_chunk, LANES))
            for kk in range(CONV_K):
                s = (base + kk) % SUBLANES
                a = r0 + base + kk - s
                src = zp_ref if s == 0 else zs_ref.at[s - 1]
                acc = acc + wdw_ref[kk:kk + 1, cs] * src[a:a + row_chunk, cs]
            conv_ref[r0:r0 + row_chunk, cs] = acc

    zc = conv_ref[...]
    mu = jnp.mean(zc, axis=-1, keepdims=True)
    zd = zc - mu
    var = jnp.mean(zd * zd, axis=-1, keepdims=True)
    zn = zd * lax.rsqrt(var + EPS_LN) * gln_ref[...] + bln_ref[...]
    zs = (zn * _sigmoid(zn)).astype(BF16)
    y_b = _dot(zs, wco_ref[...])
    gates = _sigmoid(gate_logits)
    merged = (gates[:, 0:D] * y_a + gates[:, D:2 * D] * y_b).astype(BF16)
    gate_m = mod_ref[:, 2 * D:3 * D]
    out_ref[...] = x_ref[...] + gate_m * _dot(merged, wout_ref[...])


def _cast_chunk_spec(rows, cols, n_i, n_steps):
    every = 1
    while (rows * every) % n_steps or (rows * every // n_steps) % (2 * SUBLANES):
        every *= 2
    chunk = rows * every // n_steps
    return pl.BlockSpec((chunk, cols), lambda b, i: ((b * n_i + i) // every, 0))


def _mix(x, mod3, z, o, h, w_in, w_dw, b_dw, g_ln, b_ln, w_co, w_ao, w_out, cast_ws):
    B, S, D = x.shape
    ts = TS_MIX
    hb = ts // HALO
    n_halo = S // HALO
    n_i = S // ts
    tok = lambda w: pl.BlockSpec((None, ts, w), lambda b, i: (b, i, 0))
    cast_specs = [_cast_chunk_spec(w.shape[0], w.shape[1], n_i, B * n_i) for w in cast_ws]
    return pl.pallas_call(
        functools.partial(_mix_kernel, n_cast=len(cast_ws)),
        out_shape=(jax.ShapeDtypeStruct((B, S, D), F32),
                   *[jax.ShapeDtypeStruct(w.shape, BF16) for w in cast_ws]),
        grid=(B, S // ts),
        in_specs=[
            tok(D),
            pl.BlockSpec((None, 1, N_MOD * D), lambda b, i: (b, 0, 0)),
            tok(CONV_CH),
            pl.BlockSpec((None, HALO, CONV_CH), lambda b, i: (b, jnp.maximum(i * hb - 1, 0), 0)),
            pl.BlockSpec((None, HALO, CONV_CH),
                         lambda b, i: (b, jnp.minimum((i + 1) * hb, n_halo - 1), 0)),
            tok(N_HEADS * V_HEAD),
            tok(D),
            _const_spec(w_in.shape),
            _const_spec(w_dw.shape),
            _const_spec(b_dw.shape),
            _const_spec(g_ln.shape),
            _const_spec(b_ln.shape),
            _const_spec(w_co.shape),
            _const_spec(w_ao.shape),
            _const_spec(w_out.shape),
            *cast_specs,
        ],
        out_specs=(tok(D), *cast_specs),
        scratch_shapes=[
            pltpu.VMEM((ts + 2 * HALO, CONV_CH), F32),
            pltpu.VMEM((SUBLANES - 1, ts + 2 * HALO - SUBLANES, CONV_CH), F32),
            pltpu.VMEM((ts, CONV_CH), F32),
            pltpu.VMEM((D, 2 * D), BF16),
        ],
        compiler_params=pltpu.CompilerParams(
            dimension_semantics=("arbitrary", "arbitrary"), vmem_limit_bytes=VMEM_LIMIT),
        name="conv_merge_out",
    )(x, mod3, z, z, z, o, h, w_in, w_dw, b_dw, g_ln, b_ln, w_co, w_ao, w_out, *cast_ws)


def _ffn_kernel(x_ref, mod_ref, gffn_ref, wg_ref, wu_ref, wd_ref, gfin_ref, out_ref, *, final_norm):
    D = D_MODEL
    shift = mod_ref[:, 3 * D:4 * D]
    scale = mod_ref[:, 4 * D:5 * D]
    gate = mod_ref[:, 5 * D:6 * D]
    rows = x_ref.shape[0] // ROW_SUBTILES
    for j in range(ROW_SUBTILES):
        rs = slice(j * rows, (j + 1) * rows)
        x = x_ref[rs, :]
        h = (_rms(x, gffn_ref[...]) * (1.0 + scale) + shift).astype(BF16)
        g = _dot(h, wg_ref[...])
        u = _dot(h, wu_ref[...])
        a = (g * _sigmoid(g) * u).astype(BF16)
        x2 = x + gate * _dot(a, wd_ref[...])
        out_ref[rs, :] = _rms(x2, gfin_ref[...]) if final_norm else x2


def _ffn(x, mod3, g_ffn, w_gate, w_up, w_down, g_final, final_norm):
    B, S, D = x.shape
    tm = TM_FFN
    tok = pl.BlockSpec((None, tm, D), lambda b, i: (b, i, 0))
    return pl.pallas_call(
        functools.partial(_ffn_kernel, final_norm=final_norm),
        out_shape=jax.ShapeDtypeStruct((B, S, D), F32),
        grid=(B, S // tm),
        in_specs=[
            tok,
            pl.BlockSpec((None, 1, N_MOD * D), lambda b, i: (b, 0, 0)),
            _const_spec(g_ffn.shape),
            _const_spec(w_gate.shape),
            _const_spec(w_up.shape),
            _const_spec(w_down.shape),
            _const_spec(g_final.shape),
        ],
        out_specs=tok,
        compiler_params=pltpu.CompilerParams(
            dimension_semantics=("parallel", "parallel"), vmem_limit_bytes=VMEM_LIMIT),
        name="swiglu_final_norm",
    )(x, mod3, g_ffn, w_gate, w_up, w_down, g_final)


def kernel(x, c, positions, w_ada, b_ada, g_norm_mix, w_in, g_q_a, w_q_up, g_kv_a, w_kv_up,
           w_attn_o, w_dw, b_dw, g_conv_ln, b_conv_ln, w_conv_out, w_out, g_norm_ffn,
           w_ffn_gate, w_ffn_up, w_ffn_down, g_final):
    B, S, D = x.shape
    depth = w_ada.shape[0]
    for l in range(depth):
        if l == 0:
            mod3, cos_t, sin_t = _modulation(c, w_ada[l], b_ada[l][None, :], positions)
        else:
            mod3 = _modulation(c, w_ada[l], b_ada[l][None, :])
        w_in_t = w_in[l].T
        q, k, vt, z, h, w_co_b, w_ao_b, w_out_b = _inproj(
            x, mod3, g_norm_mix[l][None, :], w_in_t, g_q_a[l][None, :],
            w_q_up[l], g_kv_a[l][None, :], w_kv_up[l], cos_t, sin_t,
            (w_conv_out[l], w_attn_o[l], w_out[l]))
        o = _attention(q, k, vt)
        x, w_gate_b, w_up_b, w_down_b = _mix(
            x, mod3, z, o, h, w_in_t, w_dw[l], b_dw[l][None, :], g_conv_ln[l][None, :],
            b_conv_ln[l][None, :], w_co_b, w_ao_b, w_out_b,
            (w_ffn_gate[l], w_ffn_up[l], w_ffn_down[l]))
        x = _ffn(x, mod3, g_norm_ffn[l][None, :], w_gate_b, w_up_b, w_down_b, g_final[None, :],
                 final_norm=(l == depth - 1))
    return x
```

```python
import functools
import math

import jax
import jax.numpy as jnp
from jax import lax
from jax.experimental import pallas as pl
from jax.experimental.pallas import tpu as pltpu

F32 = jnp.float32
BF16 = jnp.bfloat16

D_MODEL = 1024
N_HEADS = 8
Q_LORA = 256
KV_LORA = 128
QK_NOPE = 64
QK_ROPE = 32
HALF_ROPE = QK_ROPE // 2
V_HEAD = 64
QK_HEAD = QK_NOPE + QK_ROPE
ATTN_SCALE = 1.0 / math.sqrt(QK_HEAD)
Q_SCALE = ATTN_SCALE * math.log2(math.e)
ROPE_THETA = 10000.0
CONV_CH = 512
CONV_K = 31
N_MOD = 6
EPS_RMS = 1e-6
EPS_LN = 1e-5

LANES = 128
SUBLANES = 8
HEAD_PAD = LANES
VMEM_LIMIT = 56 * 1024 * 1024

C_QA = 0
C_KVA = C_QA + Q_LORA
C_KR = C_KVA + KV_LORA
C_CONV = C_KR + HEAD_PAD

TS_IN = 1024
TQ = 512
KEY_CHUNK = 1024
TS_MIX = 512
HALO = 16
TM_FFN = 512
ROW_SUBTILES = 2


def _sigmoid(x):
    return 1.0 / (1.0 + jnp.exp(-x))


def _rms(x, g):
    return x * lax.rsqrt(jnp.mean(x * x, axis=-1, keepdims=True) + EPS_RMS) * g


def _dot(a, b):
    return jnp.dot(a, b, preferred_element_type=F32)


def _const_spec(shape):
    nd = len(shape)
    return pl.BlockSpec(shape, lambda *_: (0,) * nd, pipeline_mode=pl.Buffered(1))


def _rope_tile_t(t, ctab_t, stab_t):
    swapped = jnp.concatenate([t[0:QK_NOPE], t[QK_HEAD:HEAD_PAD], t[QK_HEAD:HEAD_PAD]], axis=0)
    return t * ctab_t + swapped * stab_t


def _mod_kernel(c_ref, w_ref, b_ref, *rest, with_rope):
    if with_rope:
        pos_ref, freq_ref, o_ref, cos_ref, sin_ref = rest
        seq = pos_ref.shape[1]
        rows = cos_ref.shape[1] // seq
        for r in range(rows):
            pos = pos_ref[pl.ds(pl.program_id(0) * rows + r, 1), :].astype(F32)
            ang = pos * freq_ref[...]
            cos_ref[:, r * seq:(r + 1) * seq] = jnp.cos(ang)
            sin_ref[:, r * seq:(r + 1) * seq] = jnp.sin(ang)
    else:
        (o_ref,) = rest
    c = c_ref[...]
    c_act = (c * _sigmoid(c)).astype(BF16)
    mod = _dot(c_act, w_ref[...].astype(BF16)) + b_ref[...]
    for b in range(mod.shape[0]):
        o_ref[b] = mod[b:b + 1, :]


def _modulation(c, w_ada, b_ada, positions=None):
    B, D = c.shape
    n = w_ada.shape[1]
    bn = 768
    steps = n // bn
    with_rope = positions is not None
    in_specs = [
        pl.BlockSpec((B, D), lambda j: (0, 0)),
        pl.BlockSpec((D, bn), lambda j: (0, j)),
        pl.BlockSpec((1, bn), lambda j: (0, j)),
    ]
    out_shape = [jax.ShapeDtypeStruct((B, 1, n), F32)]
    out_specs = [pl.BlockSpec((B, 1, bn), lambda j: (0, 0, j))]
    operands = [c, w_ada, b_ada]
    if with_rope:
        t = positions.size
        assert positions.shape[0] % steps == 0
        inv_freq = ROPE_THETA ** (-jnp.arange(0, QK_ROPE, 2, dtype=F32) / QK_ROPE)
        in_specs += [pl.BlockSpec(positions.shape, lambda j: (0, 0)),
                     pl.BlockSpec((HALF_ROPE, 1), lambda j: (0, 0))]
        table = pl.BlockSpec((HALF_ROPE, t // steps), lambda j: (0, j))
        out_shape += [jax.ShapeDtypeStruct((HALF_ROPE, t), F32)] * 2
        out_specs += [table, table]
        operands += [positions, inv_freq[:, None]]
    out = pl.pallas_call(
        functools.partial(_mod_kernel, with_rope=with_rope),
        out_shape=tuple(out_shape),
        grid=(steps,),
        in_specs=in_specs,
        out_specs=tuple(out_specs),
        compiler_params=pltpu.CompilerParams(
            dimension_semantics=("parallel",), vmem_limit_bytes=VMEM_LIMIT),
        name="adaln_modulation",
    )(*operands)
    return out if with_rope else out[0]


def _mixer_input(x, mod_ref, gmix_ref):
    D = D_MODEL
    shift = mod_ref[:, 0:D]
    scale = mod_ref[:, D:2 * D]
    return (_rms(x, gmix_ref[...]) * (1.0 + scale) + shift).astype(BF16)


def _first_grid_step():
    return jnp.logical_and(pl.program_id(0) == 0, pl.program_id(1) == 0)


def _transpose_rows(wt_ref, row0, n_rows, out_ref, col0):
    for j in range(n_rows // LANES):
        rows = wt_ref[row0 + j * LANES:row0 + (j + 1) * LANES, :]
        out_ref[:, col0 + j * LANES:col0 + (j + 1) * LANES] = rows.T.astype(BF16)


def _relayout_w_proj(wt_ref, out_ref):
    o_kr = Q_LORA + KV_LORA
    o_conv = o_kr + QK_ROPE
    _transpose_rows(wt_ref, 0, o_kr, out_ref, C_QA)
    k1 = wt_ref[o_kr:o_kr + HALF_ROPE, :]
    k2 = wt_ref[o_kr + HALF_ROPE:o_conv, :]
    key_tile_t = jnp.concatenate([jnp.zeros((QK_NOPE, k1.shape[1]), F32), k1, k2, -k2, k1], axis=0)
    out_ref[:, C_KR:C_KR + HEAD_PAD] = key_tile_t.T.astype(BF16)
    _transpose_rows(wt_ref, o_conv, 2 * CONV_CH, out_ref, C_CONV)


def _relayout_w_up(wqu_ref, wkvu_ref, wq_ref, wk_ref, wvt_ref):
    wq_t = wqu_ref[...].T
    for hd in range(N_HEADS):
        r0 = hd * QK_HEAD
        r1 = wq_t[r0 + QK_NOPE:r0 + QK_NOPE + HALF_ROPE]
        r2 = wq_t[r0 + QK_NOPE + HALF_ROPE:r0 + QK_HEAD]
        tile = jnp.concatenate([wq_t[r0:r0 + QK_HEAD], -r2, r1], axis=0)
        wq_ref[hd * HEAD_PAD:(hd + 1) * HEAD_PAD, :] = tile.astype(BF16)
    assert QK_NOPE + V_HEAD == HEAD_PAD
    wkv = wkvu_ref[...]
    lane = lax.broadcasted_iota(jnp.int32, wkv.shape, 1)
    wk_ref[...] = jnp.where(lane % HEAD_PAD < QK_NOPE, wkv, 0.0).astype(BF16)
    for hd in range(N_HEADS):
        head_t = wkv[:, hd * HEAD_PAD:(hd + 1) * HEAD_PAD].T
        wvt_ref[hd * V_HEAD:(hd + 1) * V_HEAD, :] = head_t[QK_NOPE:, :].astype(BF16)


def _inproj_kernel(x_ref, mod_ref, gmix_ref, w_ref, gq_ref, wqu_ref, gkv_ref, wkvu_ref,
                   cost_ref, sint_ref, *rest, n_cast):
    cast_in = rest[:n_cast]
    q_ref, k_ref, vt_ref, z_ref, h_ref = rest[n_cast:n_cast + 5]
    cast_out = rest[n_cast + 5:2 * n_cast + 5]
    win_ref, wq_ref, wk_ref, wvt_ref = rest[2 * n_cast + 5:]
    for src, dst in zip(cast_in, cast_out):
        dst[...] = src[...].astype(BF16)

    @pl.when(_first_grid_step())
    def _():
        _relayout_w_proj(w_ref, win_ref)
        _relayout_w_up(wqu_ref, wkvu_ref, wq_ref, wk_ref, wvt_ref)

    h = _mixer_input(x_ref[...], mod_ref, gmix_ref)
    h_ref[...] = h
    cos_t = cost_ref[...]
    sin_t = sint_ref[...]
    n_pos = cos_t.shape[1]
    ctab_t = jnp.concatenate([jnp.ones((QK_NOPE, n_pos), F32), cos_t, cos_t,
                              jnp.zeros((HEAD_PAD - QK_HEAD, n_pos), F32)], axis=0)
    stab_t = jnp.concatenate([jnp.zeros((QK_NOPE, n_pos), F32), sin_t, sin_t,
                              jnp.zeros((HEAD_PAD - QK_HEAD, n_pos), F32)], axis=0)

    proj = _dot(h, win_ref[...])

    z_ref[...] = (proj[:, C_CONV:C_CONV + CONV_CH]
                  * _sigmoid(proj[:, C_CONV + CONV_CH:C_CONV + 2 * CONV_CH]))

    qn = _rms(proj[:, C_QA:C_QA + Q_LORA], gq_ref[...]).astype(BF16)
    q_all_t = lax.dot_general(wq_ref[...], qn, (((1,), (1,)), ((), ())), preferred_element_type=F32)
    for hd in range(N_HEADS):
        sl = slice(hd * HEAD_PAD, (hd + 1) * HEAD_PAD)
        q_ref[sl, :] = (_rope_tile_t(q_all_t[sl, :], ctab_t, stab_t) * Q_SCALE).astype(BF16)

    kvn = _rms(proj[:, C_KVA:C_KVA + KV_LORA], gkv_ref[...]).astype(BF16)
    k_pad = _dot(kvn, wk_ref[...])
    vt_ref[...] = lax.dot_general(wvt_ref[...], kvn, (((1,), (1,)), ((), ())),
                                  preferred_element_type=F32).astype(BF16)
    kr_rot = _rope_tile_t(proj[:, C_KR:C_KR + HEAD_PAD].T, ctab_t, stab_t).T
    for hd in range(N_HEADS):
        sl = slice(hd * HEAD_PAD, (hd + 1) * HEAD_PAD)
        k_ref[:, sl] = (k_pad[:, sl] + kr_rot).astype(BF16)


def _inproj(x, mod3, g_mix, w_in, g_q, w_q_up, g_kv, w_kv_up, cos_t, sin_t, cast_ws):
    B, S, D = x.shape
    ts = TS_IN
    nq = N_HEADS * HEAD_PAD
    nv = N_HEADS * V_HEAD
    tok = lambda w: pl.BlockSpec((None, ts, w), lambda b, i: (b, i, 0))
    n_i = S // ts
    table = pl.BlockSpec((HALF_ROPE, ts), lambda b, i: (0, b * n_i + i))
    cast_specs = [_cast_chunk_spec(w.shape[0], w.shape[1], n_i, B * n_i) for w in cast_ws]
    return pl.pallas_call(
        functools.partial(_inproj_kernel, n_cast=len(cast_ws)),
        out_shape=(
            jax.ShapeDtypeStruct((B, nq, S), BF16),
            jax.ShapeDtypeStruct((B, S, nq), BF16),
            jax.ShapeDtypeStruct((B, nv, S), BF16),
            jax.ShapeDtypeStruct((B, S, CONV_CH), F32),
            jax.ShapeDtypeStruct((B, S, D), BF16),
            *[jax.ShapeDtypeStruct(w.shape, BF16) for w in cast_ws],
        ),
        grid=(B, S // ts),
        in_specs=[
            tok(D),
            pl.BlockSpec((None, 1, N_MOD * D), lambda b, i: (b, 0, 0)),
            _const_spec(g_mix.shape),
            _const_spec(w_in.shape),
            _const_spec(g_q.shape),
            _const_spec(w_q_up.shape),
            _const_spec(g_kv.shape),
            _const_spec(w_kv_up.shape),
            table,
            table,
            *cast_specs,
        ],
        out_specs=(pl.BlockSpec((None, nq, ts), lambda b, i: (b, 0, i)), tok(nq),
                   pl.BlockSpec((None, nv, ts), lambda b, i: (b, 0, i)),
                   tok(CONV_CH), tok(D), *cast_specs),
        scratch_shapes=[pltpu.VMEM((D, C_CONV + 2 * CONV_CH), BF16),
                        pltpu.VMEM((nq, w_q_up.shape[0]), BF16),
                        pltpu.VMEM((w_kv_up.shape[0], nq), BF16),
                        pltpu.VMEM((nv, w_kv_up.shape[0]), BF16)],
        compiler_params=pltpu.CompilerParams(
            dimension_semantics=("arbitrary", "arbitrary"), vmem_limit_bytes=VMEM_LIMIT),
        name="input_projection",
    )(x, mod3, g_mix, w_in, g_q, w_q_up, g_kv, w_kv_up, cos_t, sin_t, *cast_ws)


def _attn_kernel(qt_ref, k_ref, vt_ref, o_ref):
    n_kc = k_ref.shape[0] // KEY_CHUNK

    def score_chunk(hd, c):
        sl = slice(hd * HEAD_PAD, (hd + 1) * HEAD_PAD)
        ks = slice(c * KEY_CHUNK, (c + 1) * KEY_CHUNK)
        return _dot(k_ref[ks, sl], qt_ref[sl, :])

    def col_max(chunks):
        m = jnp.max(chunks[0], axis=0, keepdims=True)
        for st in chunks[1:]:
            m = jnp.maximum(m, jnp.max(st, axis=0, keepdims=True))
        return m

    st_next = [score_chunk(0, c) for c in range(n_kc)]
    outs = []
    for hd in range(N_HEADS):
        st_cur, st_next = st_next, []
        m = col_max(st_cur)
        l = acc = None
        for c in range(n_kc):
            if hd + 1 < N_HEADS:
                st_next.append(score_chunk(hd + 1, c))
            ks = slice(c * KEY_CHUNK, (c + 1) * KEY_CHUNK)
            p = jnp.exp2(st_cur[c] - m)
            ls = jnp.sum(p, axis=0, keepdims=True)
            pv = _dot(vt_ref[hd * V_HEAD:(hd + 1) * V_HEAD, ks], p.astype(BF16))
            l, acc = (ls, pv) if c == 0 else (l + ls, acc + pv)
        outs.append(acc / l)
        if hd % 2 == 1:
            o_pair = jnp.concatenate(outs, axis=0).T
            o_ref[:, (hd // 2) * 2 * V_HEAD:(hd // 2 + 1) * 2 * V_HEAD] = o_pair.astype(BF16)
            outs = []


def _attention(qt, k, vt):
    B, nq, S = qt.shape
    nv = N_HEADS * V_HEAD
    return pl.pallas_call(
        _attn_kernel,
        out_shape=jax.ShapeDtypeStruct((B, S, nv), BF16),
        grid=(B, S // TQ),
        in_specs=[
            pl.BlockSpec((None, nq, TQ), lambda b, i: (b, 0, i)),
            pl.BlockSpec((None, S, nq), lambda b, i: (b, 0, 0)),
            pl.BlockSpec((None, nv, S), lambda b, i: (b, 0, 0)),
        ],
        out_specs=pl.BlockSpec((None, TQ, nv), lambda b, i: (b, i, 0)),
        compiler_params=pltpu.CompilerParams(
            dimension_semantics=("parallel", "parallel"), vmem_limit_bytes=VMEM_LIMIT),
        name="mla_attention",
    )(qt, k, vt)


def _relayout_w_gate(wt_ref, out_ref):
    o_gate = Q_LORA + KV_LORA + QK_ROPE + 2 * CONV_CH
    _transpose_rows(wt_ref, o_gate, out_ref.shape[1], out_ref, 0)


def _mix_kernel(x_ref, mod_ref, z_ref, zprev_ref, znext_ref, o_ref, h_ref, w_ref,
                wdw_ref, bdw_ref, gln_ref, bln_ref, wco_ref, wao_ref, wout_ref, *rest, n_cast):
    cast_in = rest[:n_cast]
    out_ref = rest[n_cast]
    cast_out = rest[n_cast + 1:2 * n_cast + 1]
    zp_ref, zs_ref, conv_ref, wgate_ref = rest[2 * n_cast + 1:]
    for src, dst in zip(cast_in, cast_out):
        dst[...] = src[...].astype(BF16)
    D = D_MODEL
    ts = TS_MIX
    i = pl.program_id(1)
    n_i = pl.num_programs(1)
    pl.when(_first_grid_step())(lambda: _relayout_w_gate(w_ref, wgate_ref))

    zp_ref[0:HALO, :] = jnp.where(i > 0, zprev_ref[...], 0.0)
    zp_ref[HALO:HALO + ts, :] = z_ref[...]
    zp_ref[HALO + ts:2 * HALO + ts, :] = jnp.where(i < n_i - 1, znext_ref[...], 0.0)

    n_shift = zs_ref.shape[1]
    for s in range(1, SUBLANES):
        zs_ref[s - 1] = zp_ref[s:s + n_shift, :]

    gate_logits = _dot(h_ref[...], wgate_ref[...])
    y_a = _dot(o_ref[...], wao_ref[...])

    row_chunk = 64
    base = HALO - CONV_K // 2
    for cb in range(CONV_CH // LANES):
        cs = slice(cb * LANES, (cb + 1) * LANES)
        for rb in range(ts // row_chunk):
            r0 = rb * row_chunk
            acc = jnp.broadcast_to(bdw_ref[:, cs], (row_chunk, LANES))
            for kk in range(CONV_K):
                s = (base + kk) % SUBLANES
                a = r0 + base + kk - s
                src = zp_ref if s == 0 else zs_ref.at[s - 1]
                acc = acc + wdw_ref[kk:kk + 1, cs] * src[a:a + row_chunk, cs]
            conv_ref[r0:r0 + row_chunk, cs] = acc

    zc = conv_ref[...]
    mu = jnp.mean(zc, axis=-1, keepdims=True)
    zd = zc - mu
    var = jnp.mean(zd * zd, axis=-1, keepdims=True)
    zn = zd * lax.rsqrt(var + EPS_LN) * gln_ref[...] + bln_ref[...]
    zs = (zn * _sigmoid(zn)).astype(BF16)
    y_b = _dot(zs, wco_ref[...])
    gates = _sigmoid(gate_logits)
    merged = (gates[:, 0:D] * y_a + gates[:, D:2 * D] * y_b).astype(BF16)
    gate_m = mod_ref[:, 2 * D:3 * D]
    out_ref[...] = x_ref[...] + gate_m * _dot(merged, wout_ref[...])


def _cast_chunk_spec(rows, cols, n_i, n_steps):
    every = 1
    while (rows * every) % n_steps or (rows * every // n_steps) % (2 * SUBLANES):
        every *= 2
    chunk = rows * every // n_steps
    return pl.BlockSpec((chunk, cols), lambda b, i: ((b * n_i + i) // every, 0))


def _mix(x, mod3, z, o, h, w_in, w_dw, b_dw, g_ln, b_ln, w_co, w_ao, w_out, cast_ws):
    B, S, D = x.shape
    ts = TS_MIX
    hb = ts // HALO
    n_halo = S // HALO
    n_i = S // ts
    tok = lambda w: pl.BlockSpec((None, ts, w), lambda b, i: (b, i, 0))
    cast_specs = [_cast_chunk_spec(w.shape[0], w.shape[1], n_i, B * n_i) for w in cast_ws]
    return pl.pallas_call(
        functools.partial(_mix_kernel, n_cast=len(cast_ws)),
        out_shape=(jax.ShapeDtypeStruct((B, S, D), F32),
                   *[jax.ShapeDtypeStruct(w.shape, BF16) for w in cast_ws]),
        grid=(B, S // ts),
        in_specs=[
            tok(D),
            pl.BlockSpec((None, 1, N_MOD * D), lambda b, i: (b, 0, 0)),
            tok(CONV_CH),
            pl.BlockSpec((None, HALO, CONV_CH), lambda b, i: (b, jnp.maximum(i * hb - 1, 0), 0)),
            pl.BlockSpec((None, HALO, CONV_CH),
                         lambda b, i: (b, jnp.minimum((i + 1) * hb, n_halo - 1), 0)),
            tok(N_HEADS * V_HEAD),
            tok(D),
            _const_spec(w_in.shape),
            _const_spec(w_dw.shape),
            _const_spec(b_dw.shape),
            _const_spec(g_ln.shape),
            _const_spec(b_ln.shape),
            _const_spec(w_co.shape),
            _const_spec(w_ao.shape),
            _const_spec(w_out.shape),
            *cast_specs,
        ],
        out_specs=(tok(D), *cast_specs),
        scratch_shapes=[
            pltpu.VMEM((ts + 2 * HALO, CONV_CH), F32),
            pltpu.VMEM((SUBLANES - 1, ts + 2 * HALO - SUBLANES, CONV_CH), F32),
            pltpu.VMEM((ts, CONV_CH), F32),
            pltpu.VMEM((D, 2 * D), BF16),
        ],
        compiler_params=pltpu.CompilerParams(
            dimension_semantics=("arbitrary", "arbitrary"), vmem_limit_bytes=VMEM_LIMIT),
        name="conv_merge_out",
    )(x, mod3, z, z, z, o, h, w_in, w_dw, b_dw, g_ln, b_ln, w_co, w_ao, w_out, *cast_ws)


def _ffn_kernel(x_ref, mod_ref, gffn_ref, wg_ref, wu_ref, wd_ref, gfin_ref, out_ref, *, final_norm):
    D = D_MODEL
    shift = mod_ref[:, 3 * D:4 * D]
    scale = mod_ref[:, 4 * D:5 * D]
    gate = mod_ref[:, 5 * D:6 * D]
    rows = x_ref.shape[0] // ROW_SUBTILES
    for j in range(ROW_SUBTILES):
        rs = slice(j * rows, (j + 1) * rows)
        x = x_ref[rs, :]
        h = (_rms(x, gffn_ref[...]) * (1.0 + scale) + shift).astype(BF16)
        g = _dot(h, wg_ref[...])
        u = _dot(h, wu_ref[...])
        a = (g * _sigmoid(g) * u).astype(BF16)
        x2 = x + gate * _dot(a, wd_ref[...])
        out_ref[rs, :] = _rms(x2, gfin_ref[...]) if final_norm else x2


def _ffn(x, mod3, g_ffn, w_gate, w_up, w_down, g_final, final_norm):
    B, S, D = x.shape
    tm = TM_FFN
    tok = pl.BlockSpec((None, tm, D), lambda b, i: (b, i, 0))
    return pl.pallas_call(
        functools.partial(_ffn_kernel, final_norm=final_norm),
        out_shape=jax.ShapeDtypeStruct((B, S, D), F32),
        grid=(B, S // tm),
        in_specs=[
            tok,
            pl.BlockSpec((None, 1, N_MOD * D), lambda b, i: (b, 0, 0)),
            _const_spec(g_ffn.shape),
            _const_spec(w_gate.shape),
            _const_spec(w_up.shape),
            _const_spec(w_down.shape),
            _const_spec(g_final.shape),
        ],
        out_specs=tok,
        compiler_params=pltpu.CompilerParams(
            dimension_semantics=("parallel", "parallel"), vmem_limit_bytes=VMEM_LIMIT),
        name="swiglu_final_norm",
    )(x, mod3, g_ffn, w_gate, w_up, w_down, g_final)


def kernel(x, c, positions, w_ada, b_ada, g_norm_mix, w_in, g_q_a, w_q_up, g_kv_a, w_kv_up,
           w_attn_o, w_dw, b_dw, g_conv_ln, b_conv_ln, w_conv_out, w_out, g_norm_ffn,
           w_ffn_gate, w_ffn_up, w_ffn_down, g_final):
    B, S, D = x.shape
    depth = w_ada.shape[0]
    for l in range(depth):
        if l == 0:
            mod3, cos_t, sin_t = _modulation(c, w_ada[l], b_ada[l][None, :], positions)
        else:
            mod3 = _modulation(c, w_ada[l], b_ada[l][None, :])
        w_in_t = w_in[l].T
        q, k, vt, z, h, w_co_b, w_ao_b, w_out_b = _inproj(
            x, mod3, g_norm_mix[l][None, :], w_in_t, g_q_a[l][None, :],
            w_q_up[l], g_kv_a[l][None, :], w_kv_up[l], cos_t, sin_t,
            (w_conv_out[l], w_attn_o[l], w_out[l]))
        o = _attention(q, k, vt)
        x, w_gate_b, w_up_b, w_down_b = _mix(
            x, mod3, z, o, h, w_in_t, w_dw[l], b_dw[l][None, :], g_conv_ln[l][None, :],
            b_conv_ln[l][None, :], w_co_b, w_ao_b, w_out_b,
            (w_ffn_gate[l], w_ffn_up[l], w_ffn_down[l]))
        x = _ffn(x, mod3, g_norm_ffn[l][None, :], w_gate_b, w_up_b, w_down_b, g_final[None, :],
                 final_norm=(l == depth - 1))
    return x
```
